```python
import math
import jax, jax.numpy as jnp
from jax import lax
import numpy as np

D_MODEL = 1024
BATCH = 4
SEQ = 4096
DEPTH = 1
DEC_BATCH = 32
DEC_SEQ = 8
PAST_LEN = 16384
PAGE_SIZE = 128

MIX_W = D_MODEL
ATT_W = MIX_W // 2
SGU_W = MIX_W - ATT_W
N_ATT_HEADS = 8
D_V = ATT_W // N_ATT_HEADS
D_QK = D_V // 2
N_SGU_GROUPS = 8
SGU_GC = SGU_W // N_SGU_GROUPS
CHUNK = 128
IN_W = 3 * ATT_W + 2 * SGU_W
N_EXPERT_GROUPS = 4
EXPERTS_PER_GROUP = 4
N_EXPERTS = N_EXPERT_GROUPS * EXPERTS_PER_GROUP
TOP_K_IN_GROUP = 2
D_EXPERT = D_MODEL // 2
Q_BLOCK = 128
KV_PAGES_PER_STEP = 8
EPS = 1e-6

kernel_name = "hymba_diffattn_sgu_hmoe_step"

F32 = jnp.float32


def rms_norm(x, g):
    xf = x.astype(F32)
    y = xf * lax.rsqrt(jnp.mean(xf * xf, axis=-1, keepdims=True) + EPS)
    return (y * g.astype(F32)).astype(x.dtype)


def layer_norm(x, g, b):
    xf = x.astype(F32)
    mu = jnp.mean(xf, axis=-1, keepdims=True)
    xc = xf - mu
    y = xc * lax.rsqrt(jnp.mean(xc * xc, axis=-1, keepdims=True) + EPS)
    return (y * g.astype(F32) + b.astype(F32)).astype(x.dtype)


def alibi_slopes(n_heads):
    return 2.0 ** (-8.0 * jnp.arange(1, n_heads + 1, dtype=F32) / n_heads)


def diff_lambda(lq1, lk1, lq2, lk2, lam_init):
    return (jnp.exp(jnp.sum(lq1.astype(F32) * lk1.astype(F32)))
            - jnp.exp(jnp.sum(lq2.astype(F32) * lk2.astype(F32))) + lam_init)


def diff_attn_prompt(q, k, v, lam, slopes):
    B, S, H = q.shape[:3]
    nb = S // Q_BLOCK
    qb = (q.astype(F32) * D_QK ** -0.5).reshape(B, nb, Q_BLOCK, H, 2, D_QK).transpose(1, 0, 2, 3, 4, 5)
    kf = k.astype(F32)
    vf = v.astype(F32)
    kpos = jnp.arange(S)

    def one_block(args):
        qi, i = args
        qpos = i * Q_BLOCK + jnp.arange(Q_BLOCK)
        dist = (qpos[:, None] - kpos[None, :]).astype(F32)
        s = jnp.einsum('bqhjd,bkhjd->bjhqk', qi, kf) - slopes[:, None, None] * dist
        s = jnp.where(dist >= 0, s, -jnp.inf)
        p = jax.nn.softmax(s, axis=-1)
        a = p[:, 0] - lam * p[:, 1]
        return jnp.einsum('bhqk,bkhd->bqhd', a, vf)

    o = lax.map(one_block, (qb, jnp.arange(nb)))
    return o.transpose(1, 0, 2, 3, 4).reshape(B, S, H, D_V).astype(v.dtype)


def diff_attn_sample(q, k, v, lam, cache_k, cache_v, page_table, slopes):
    Bd, T, H = q.shape[:3]
    n_pages = page_table.shape[1]
    pps = math.gcd(n_pages, KV_PAGES_PER_STEP)
    n_steps = n_pages // pps
    blk = pps * PAGE_SIZE
    past = n_pages * PAGE_SIZE
    qf = q.astype(F32) * D_QK ** -0.5
    qpos = past + jnp.arange(T)
    dist = (qpos[:, None] - qpos[None, :]).astype(F32)
    s = jnp.einsum('bqhjd,bkhjd->bjhqk', qf, k.astype(F32)) - slopes[:, None, None] * dist
    s = jnp.where(dist >= 0, s, -jnp.inf)
    m = jnp.max(s, axis=-1)
    p = jnp.exp(s - m[..., None])
    l = jnp.sum(p, axis=-1)
    acc = jnp.einsum('bjhqk,bkhd->bjhqd', p, v.astype(F32))
    pt = page_table.reshape(Bd, n_steps, pps).transpose(1, 0, 2)

    def step(carry, xs):
        m, l, acc = carry
        pages, j = xs
        kb = cache_k[pages].reshape(Bd, blk, H, 2, D_QK).astype(F32)
        vb = cache_v[pages].reshape(Bd, blk, H, D_V).astype(F32)
        kpos = j * blk + jnp.arange(blk)
        d = (qpos[:, None] - kpos[None, :]).astype(F32)
        sb = jnp.einsum('bqhjd,bkhjd->bjhqk', qf, kb) - slopes[:, None, None] * d
        m_new = jnp.maximum(m, jnp.max(sb, axis=-1))
        corr = jnp.exp(m - m_new)
        pb = jnp.exp(sb - m_new[..., None])
        l_new = l * corr + jnp.sum(pb, axis=-1)
        acc_new = acc * corr[..., None] + jnp.einsum('bjhqk,bkhd->bjhqd', pb, vb)
        return (m_new, l_new, acc_new), None

    (m, l, acc), _ = lax.scan(step, (m, l, acc), (pt, jnp.arange(n_steps)))
    o = acc / l[..., None]
    a = o[:, 0] - lam * o[:, 1]
    return a.transpose(0, 2, 1, 3).astype(v.dtype)


def spatial_gating(u, vs, ws, bs, chunk_len):
    B, S, _ = vs.shape
    n = S // chunk_len
    causal = jnp.tril(jnp.ones((chunk_len, chunk_len), dtype=ws.dtype))
    wm = ws[:, :chunk_len, :chunk_len] * causal
    vb = vs.reshape(B, n, chunk_len, N_SGU_GROUPS, SGU_GC)
    mixed = jnp.einsum('gts,bnsgc->bntgc', wm, vb) + bs[:, :chunk_len].T[None, None, :, :, None]
    return u * mixed.reshape(B, S, SGU_W)


def hier_moe(x, w_rg, b_rg, w_re, b_re, w_gate, w_up, w_down):
    N = x.shape[0]
    g_logits = (x @ w_rg).astype(F32) + b_rg.astype(F32)
    g_prob = jax.nn.softmax(g_logits, axis=-1)
    g_idx = jnp.argmax(g_logits, axis=-1)
    g_w = jnp.take_along_axis(g_prob, g_idx[:, None], axis=-1)
    e_logits = ((x @ w_re).astype(F32) + b_re.astype(F32)).reshape(N, N_EXPERT_GROUPS, EXPERTS_PER_GROUP)
    e_in = jnp.take_along_axis(e_logits, g_idx[:, None, None], axis=1)[:, 0]
    top_p, top_i = lax.top_k(jax.nn.softmax(e_in, axis=-1), TOP_K_IN_GROUP)
    wts = top_p / jnp.sum(top_p, axis=-1, keepdims=True) * g_w
    expert_id = g_idx[:, None] * EXPERTS_PER_GROUP + top_i
    gates = jnp.sum(jax.nn.one_hot(expert_id, N_EXPERTS, dtype=F32) * wts[..., None], axis=1).astype(x.dtype)
    y = jnp.zeros_like(x)
    for e in range(N_EXPERTS):
        hdn = jax.nn.silu(x @ w_gate[e]) * (x @ w_up[e])
        y = y + gates[:, e:e + 1] * (hdn @ w_down[e])
    return y


def run_layer(x, c, attend, chunk_len, lam_init,
              w_ada, b_ada, g_pre_mix, g_post_mix, g_pre_ffn, g_post_ffn,
              w_in, lam_q1, lam_k1, lam_q2, lam_k2, g_subln, sgu_ln_g, sgu_ln_b, sgu_ws, sgu_bs, w_out,
              w_rg, b_rg, w_re, b_re, w_gate, w_up, w_down):
    B, S, _ = x.shape
    mod = jax.nn.silu(c) @ w_ada + b_ada
    sh1, sc1, ga1, sh2, sc2, ga2 = [mm[:, None, :] for mm in jnp.split(mod, 6, axis=-1)]
    h = rms_norm(x, g_pre_mix) * (1 + sc1) + sh1
    z = h @ w_in
    zq, zk, zv, zu, zs = jnp.split(z, [ATT_W, 2 * ATT_W, 3 * ATT_W, 3 * ATT_W + SGU_W], axis=-1)
    q = zq.reshape(B, S, N_ATT_HEADS, 2, D_QK)
    k = zk.reshape(B, S, N_ATT_HEADS, 2, D_QK)
    v = zv.reshape(B, S, N_ATT_HEADS, D_V)
    lam = diff_lambda(lam_q1, lam_k1, lam_q2, lam_k2, lam_init)
    a = rms_norm(attend(q, k, v, lam), g_subln) * (1.0 - lam_init)
    u = jax.nn.gelu(zu)
    vs = layer_norm(jax.nn.gelu(zs), sgu_ln_g, sgu_ln_b)
    s_out = spatial_gating(u, vs, sgu_ws, sgu_bs, chunk_len)
    mix = jnp.concatenate([a.reshape(B, S, ATT_W), s_out], axis=-1) @ w_out
    x = x + ga1 * rms_norm(mix, g_post_mix)
    h2 = rms_norm(x, g_pre_ffn) * (1 + sc2) + sh2
    f = hier_moe(h2.reshape(B * S, D_MODEL), w_rg, b_rg, w_re, b_re, w_gate, w_up, w_down).reshape(B, S, D_MODEL)
    x = x + ga2 * rms_norm(f, g_post_ffn)
    return x, k.reshape(B, S, N_ATT_HEADS, 2 * D_QK), v, vs[:, S - chunk_len:]


def setup_inputs(seed: int = 0) -> dict:
    key = jax.random.key(seed)
    ks = iter(jax.random.split(key, 48))

    def nrm(shape, scale):
        return jax.random.normal(next(ks), shape, F32) * scale

    n_pages = PAST_LEN // PAGE_SIZE
    n_pool = (5 * DEC_BATCH * n_pages) // 4
    page_table = jax.random.permutation(next(ks), n_pool)[:DEC_BATCH * n_pages].reshape(DEC_BATCH, n_pages).astype(jnp.int32)
    L = DEPTH
    return {
        "x_prompt": nrm((BATCH, SEQ, D_MODEL), 1.0),
        "x_sample": nrm((DEC_BATCH, DEC_SEQ, D_MODEL), 1.0),
        "cache_k": nrm((L, n_pool, PAGE_SIZE, N_ATT_HEADS, 2 * D_QK), 1.0),
        "cache_v": nrm((L, n_pool, PAGE_SIZE, N_ATT_HEADS, D_V), 1.0),
        "page_table": page_table,
        "c_prompt": nrm((BATCH, D_MODEL), 1.0),
        "c_sample": nrm((DEC_BATCH, D_MODEL), 1.0),
        "w_ada": nrm((L, D_MODEL, 6 * D_MODEL), 0.5 * D_MODEL ** -0.5),
        "b_ada": nrm((L, 6 * D_MODEL), 0.02),
        "g_pre_mix": 1.0 + nrm((L, D_MODEL), 0.05),
        "g_post_mix": 1.0 + nrm((L, D_MODEL), 0.05),
        "g_pre_ffn": 1.0 + nrm((L, D_MODEL), 0.05),
        "g_post_ffn": 1.0 + nrm((L, D_MODEL), 0.05),
        "w_in": nrm((L, D_MODEL, IN_W), D_MODEL ** -0.5),
        "lam_q1": nrm((L, D_QK), 0.1),
        "lam_k1": nrm((L, D_QK), 0.1),
        "lam_q2": nrm((L, D_QK), 0.1),
        "lam_k2": nrm((L, D_QK), 0.1),
        "g_subln": 1.0 + nrm((L, D_V), 0.05),
        "sgu_ln_g": 1.0 + nrm((L, SGU_W), 0.05),
        "sgu_ln_b": nrm((L, SGU_W), 0.02),
        "sgu_ws": nrm((L, N_SGU_GROUPS, CHUNK, CHUNK), CHUNK ** -0.5),
        "sgu_bs": 1.0 + nrm((L, N_SGU_GROUPS, CHUNK), 0.02),
        "w_out": nrm((L, MIX_W, D_MODEL), MIX_W ** -0.5),
        "w_rg": nrm((L, D_MODEL, N_EXPERT_GROUPS), D_MODEL ** -0.5),
        "b_rg": nrm((L, N_EXPERT_GROUPS), 0.01),
        "w_re": nrm((L, D_MODEL, N_EXPERTS), D_MODEL ** -0.5),
        "b_re": nrm((L, N_EXPERTS), 0.01),
        "w_gate": nrm((L, N_EXPERTS, D_MODEL, D_EXPERT), D_MODEL ** -0.5),
        "w_up": nrm((L, N_EXPERTS, D_MODEL, D_EXPERT), D_MODEL ** -0.5),
        "w_down": nrm((L, N_EXPERTS, D_EXPERT, D_MODEL), D_EXPERT ** -0.5),
    }


def reference(x_prompt, x_sample, cache_k, cache_v, page_table, c_prompt, c_sample,
              w_ada, b_ada, g_pre_mix, g_post_mix, g_pre_ffn, g_post_ffn,
              w_in, lam_q1, lam_k1, lam_q2, lam_k2, g_subln, sgu_ln_g, sgu_ln_b, sgu_ws, sgu_bs, w_out,
              w_rg, b_rg, w_re, b_re, w_gate, w_up, w_down):
    slopes = alibi_slopes(N_ATT_HEADS)
    xp, xs = x_prompt, x_sample
    kp_all, vp_all, sp_all, ks_all, vs_all, ss_all = [], [], [], [], [], []
    for l in range(DEPTH):
        lam_init = 0.8 - 0.6 * math.exp(-0.3 * l)
        lp = (w_ada[l], b_ada[l], g_pre_mix[l], g_post_mix[l], g_pre_ffn[l], g_post_ffn[l],
              w_in[l], lam_q1[l], lam_k1[l], lam_q2[l], lam_k2[l], g_subln[l], sgu_ln_g[l], sgu_ln_b[l],
              sgu_ws[l], sgu_bs[l], w_out[l], w_rg[l], b_rg[l], w_re[l], b_re[l], w_gate[l], w_up[l], w_down[l])
        ck, cv = cache_k[l], cache_v[l]
        attend_p = lambda q, k, v, lam: diff_attn_prompt(q, k, v, lam, slopes)
        attend_s = lambda q, k, v, lam: diff_attn_sample(q, k, v, lam, ck, cv, page_table, slopes)
        xp, kp, vp, sp = run_layer(xp, c_prompt, attend_p, CHUNK, lam_init, *lp)
        xs, ksm, vsm, ssm = run_layer(xs, c_sample, attend_s, x_sample.shape[1], lam_init, *lp)
        kp_all.append(kp); vp_all.append(vp); sp_all.append(sp)
        ks_all.append(ksm); vs_all.append(vsm); ss_all.append(ssm)
    return (xp, xs, jnp.stack(kp_all), jnp.stack(vp_all), jnp.stack(sp_all),
            jnp.stack(ks_all), jnp.stack(vs_all), jnp.stack(ss_all))
```

```python
import functools
import math

import jax
import jax.numpy as jnp
from jax import lax
from jax.experimental import pallas as pl
from jax.experimental.pallas import tpu as pltpu

F32 = jnp.float32
BF16 = jnp.bfloat16

EPS = 1e-6
N_ATT_HEADS = 8
D_QK = 32
D_V = 64
N_SGU_GROUPS = 8
SGU_GC = 64
CHUNK = 128
N_EXPERT_GROUPS = 4
EXPERTS_PER_GROUP = 4
N_EXPERTS = 16
PAGE_SIZE = 128
LANES = 128
VMEM_LIMIT = 56 * 1024 * 1024
NEG_INF = float("-inf")


def _cparams(*sem):
    return pltpu.CompilerParams(dimension_semantics=sem, vmem_limit_bytes=VMEM_LIMIT)


def _rms(x, g):
    return x * lax.rsqrt(jnp.mean(x * x, axis=-1, keepdims=True) + EPS) * g


def _ada_kernel(c_ref, w_ref, b_ref, o_ref):
    c = c_ref[...]
    s = c * jax.nn.sigmoid(c)
    o_ref[...] = jnp.dot(s.astype(BF16), w_ref[...].astype(BF16), preferred_element_type=F32) + b_ref[...]


def _ada(c, w_ada, b_ada):
    n, d = c.shape
    nout = w_ada.shape[1]
    tn = d
    return pl.pallas_call(
        _ada_kernel,
        grid=(nout // tn,),
        in_specs=[pl.BlockSpec((n, d), lambda j: (0, 0)),
                  pl.BlockSpec((d, tn), lambda j: (0, j)),
                  pl.BlockSpec((1, tn), lambda j: (0, j))],
        out_specs=pl.BlockSpec((n, tn), lambda j: (0, j)),
        out_shape=jax.ShapeDtypeStruct((n, nout), F32),
        compiler_params=_cparams("arbitrary"),
        name="ada",
    )(c, w_ada, b_ada.reshape(1, nout))


def _inproj_kernel(x_ref, sc_ref, sh_ref, g_ref, w_ref, lng_ref, lnb_ref,
                   k_ref, v_ref, qb_ref, kb_ref, vb_ref, u_ref, vsb_ref, vsl_ref, *, last_rows, att_w, sgu_w):
    x = x_ref[...]
    h = (_rms(x, g_ref[...]) * (1.0 + sc_ref[0]) + sh_ref[0]).astype(BF16)

    def proj(lo, width):
        return jnp.dot(h, w_ref[:, lo:lo + width], preferred_element_type=F32)

    zq = proj(0, att_w)
    qb_ref[...] = (zq * (D_QK ** -0.5)).astype(BF16)
    zk = proj(att_w, att_w)
    k_ref[...] = zk
    kb_ref[...] = zk.astype(BF16)
    zv = proj(2 * att_w, att_w)
    v_ref[...] = zv
    vb_ref[...] = zv.astype(BF16)
    u_ref[...] = jax.nn.gelu(proj(3 * att_w, sgu_w)).astype(BF16)
    gs = jax.nn.gelu(proj(3 * att_w + sgu_w, sgu_w))
    mu = jnp.mean(gs, axis=-1, keepdims=True)
    xc = gs - mu
    vs = xc * lax.rsqrt(jnp.mean(xc * xc, axis=-1, keepdims=True) + EPS) * lng_ref[...] + lnb_ref[...]
    vsb_ref[...] = vs.astype(BF16)
    tm = vs.shape[0]
    vsl_ref[0] = vs[tm - last_rows:, :]


def _inproj(x, sc, sh, g, w_in_bf, ln_g, ln_b, *, tm, tiles_per_mod, tiles_per_group, last_rows):
    n, d = x.shape
    in_w = w_in_bf.shape[1]
    att_w = 512
    sgu_w = 512
    r = sc.shape[1]
    n_tiles = n // tm
    n_groups = n_tiles // tiles_per_group
    row = lambda i: (i, 0)
    modmap = lambda i: (i // tiles_per_mod, 0, 0)
    const = lambda i: (0, 0)
    outs = pl.pallas_call(
        functools.partial(_inproj_kernel, last_rows=last_rows, att_w=att_w, sgu_w=sgu_w),
        grid=(n_tiles,),
        in_specs=[pl.BlockSpec((tm, d), row),
                  pl.BlockSpec((1, r, d), modmap),
                  pl.BlockSpec((1, r, d), modmap),
                  pl.BlockSpec((1, d), const),
                  pl.BlockSpec((d, in_w), const),
                  pl.BlockSpec((1, sgu_w), const),
                  pl.BlockSpec((1, sgu_w), const)],
        out_specs=[pl.BlockSpec((tm, att_w), row),
                   pl.BlockSpec((tm, att_w), row),
                   pl.BlockSpec((tm, att_w), row),
                   pl.BlockSpec((tm, att_w), row),
                   pl.BlockSpec((tm, att_w), row),
                   pl.BlockSpec((tm, sgu_w), row),
                   pl.BlockSpec((tm, sgu_w), row),
                   pl.BlockSpec((1, last_rows, sgu_w), lambda i: (i // tiles_per_group, 0, 0))],
        out_shape=[jax.ShapeDtypeStruct((n, att_w), F32),
                   jax.ShapeDtypeStruct((n, att_w), F32),
                   jax.ShapeDtypeStruct((n, att_w), BF16),
                   jax.ShapeDtypeStruct((n, att_w), BF16),
                   jax.ShapeDtypeStruct((n, att_w), BF16),
                   jax.ShapeDtypeStruct((n, sgu_w), BF16),
                   jax.ShapeDtypeStruct((n, sgu_w), BF16),
                   jax.ShapeDtypeStruct((n_groups, last_rows, sgu_w), F32)],
        compiler_params=_cparams("arbitrary"),
        name="inproj",
    )(x, sc, sh, g.reshape(1, d), w_in_bf, ln_g.reshape(1, sgu_w), ln_b.reshape(1, sgu_w))
    return outs


def _diff_lambda(lq1, lk1, lq2, lk2, lam_init):
    return (jnp.exp(jnp.sum(lq1 * lk1, axis=-1, keepdims=True))
            - jnp.exp(jnp.sum(lq2 * lk2, axis=-1, keepdims=True)) + lam_init)


def _pair_slopes(hp):
    s0 = jnp.where(hp == 0, 2.0 ** -1, jnp.where(hp == 1, 2.0 ** -3, jnp.where(hp == 2, 2.0 ** -5, 2.0 ** -7)))
    return s0.astype(F32), (s0 * 0.5).astype(F32)


def _attn_prompt_kernel(q_ref, k_ref, v_ref, lq1_ref, lk1_ref, lq2_ref, lk2_ref, g_ref, o_ref,
                        qs_ref, bias_ref, m_ref, l_ref, acc_ref, *, tq, tk, lam_init):
    hp = pl.program_id(1)
    qi = pl.program_id(2)
    rows = 4 * tq
    q = q_ref[0]
    lane_q = lax.broadcasted_iota(jnp.int32, (tq, LANES), 1)
    for c in range(4):
        qs_ref[c * tq:(c + 1) * tq, :] = jnp.where((lane_q >= D_QK * c) & (lane_q < D_QK * (c + 1)), q,
                                                   jnp.zeros_like(q))
    s0, s1 = _pair_slopes(hp)
    row_id = lax.broadcasted_iota(jnp.int32, (rows, 1), 0)
    slope = jnp.where(row_id < 2 * tq, s0, s1)
    r_in = lax.broadcasted_iota(jnp.int32, (rows, tk), 0) % tq
    c_in = lax.broadcasted_iota(jnp.int32, (rows, tk), 1)
    rel = c_in - r_in
    bias_ref[...] = slope * rel.astype(F32)
    m_ref[...] = jnp.full((rows, 1), NEG_INF, F32)
    l_ref[...] = jnp.zeros((rows, 1), F32)
    acc_ref[...] = jnp.zeros((rows, LANES), F32)

    def chunk(kj, masked):
        start = pl.multiple_of(kj * tk, tk)
        kc = k_ref[0, pl.ds(start, tk), :]
        vc = v_ref[0, pl.ds(start, tk), :]
        s = lax.dot_general(qs_ref[...], kc, (((1,), (1,)), ((), ())), preferred_element_type=F32)
        off = kj * tk - qi * tq
        s = s + bias_ref[...]
        if masked:
            s = jnp.where(rel + off <= 0, s, NEG_INF)
        offc = slope * off.astype(F32)
        m_old = m_ref[...]
        m_new = jnp.maximum(m_old, jnp.max(s, axis=-1, keepdims=True) + offc)
        p = jnp.exp(s - (m_new - offc))
        corr = jnp.exp(m_old - m_new)
        l_ref[...] = corr * l_ref[...] + jnp.sum(p, axis=-1, keepdims=True)
        acc_ref[...] = corr * acc_ref[...] + jnp.dot(p.astype(BF16), vc, preferred_element_type=F32)
        m_ref[...] = m_new

    n_full = (qi * tq) // tk

    def body(kj, carry):
        chunk(kj, False)
        return carry

    lax.fori_loop(0, n_full, body, 0)
    chunk(n_full, True)

    lam = _diff_lambda(lq1_ref[...], lk1_ref[...], lq2_ref[...], lk2_ref[...], lam_init)
    o = acc_ref[...] / l_ref[...]
    a = jnp.where(lane_q < D_V, o[0:tq] - lam * o[tq:2 * tq], o[2 * tq:3 * tq] - lam * o[3 * tq:4 * tq])
    a2 = a * a
    ss0 = jnp.sum(jnp.where(lane_q < D_V, a2, 0.0), axis=-1, keepdims=True)
    ss1 = jnp.sum(jnp.where(lane_q >= D_V, a2, 0.0), axis=-1, keepdims=True)
    ms = jnp.where(lane_q < D_V, ss0, ss1) * (1.0 / D_V)
    o_ref[0] = (a * lax.rsqrt(ms + EPS) * g_ref[...] * (1.0 - lam_init)).astype(BF16)


def _attn_prompt(qb, kb, vb, lam_params, g_subln, lam_init, *, tq=256, tk=512):
    b, s, w = qb.shape
    n_pairs = w // LANES
    g2 = jnp.concatenate([g_subln, g_subln]).reshape(1, LANES)
    lam_specs = [pl.BlockSpec((1, D_QK), lambda bi, hp, qi: (0, 0))] * 4
    return pl.pallas_call(
        functools.partial(_attn_prompt_kernel, tq=tq, tk=tk, lam_init=lam_init),
        grid=(b, n_pairs, s // tq),
        in_specs=[pl.BlockSpec((1, tq, LANES), lambda bi, hp, qi: (bi, qi, hp)),
                  pl.BlockSpec((1, s, LANES), lambda bi, hp, qi: (bi, 0, hp)),
                  pl.BlockSpec((1, s, LANES), lambda bi, hp, qi: (bi, 0, hp)),
                  *lam_specs,
                  pl.BlockSpec((1, LANES), lambda bi, hp, qi: (0, 0))],
        out_specs=pl.BlockSpec((1, tq, LANES), lambda bi, hp, qi: (bi, qi, hp)),
        out_shape=jax.ShapeDtypeStruct((b, s, w), BF16),
        scratch_shapes=[pltpu.VMEM((4 * tq, LANES), BF16),
                        pltpu.VMEM((4 * tq, tk), F32),
                        pltpu.VMEM((4 * tq, 1), F32),
                        pltpu.VMEM((4 * tq, 1), F32),
                        pltpu.VMEM((4 * tq, LANES), F32)],
        compiler_params=_cparams("arbitrary", "arbitrary", "arbitrary"),
        name="attn_prompt",
    )(qb, kb, vb, *[p.reshape(1, D_QK) for p in lam_params], g2)


def _attn_sample_kernel(pt_ref, q_ref, kn_ref, vn_ref, lq1_ref, lk1_ref, lq2_ref, lk2_ref, g_ref, *rest,
                        pps, t_new, past, lam_init):
    k_refs = rest[:pps]
    v_refs = rest[pps:2 * pps]
    o_ref = rest[2 * pps]
    qbd_ref, m_ref, l_ref, acc_ref = rest[2 * pps + 1:]
    j = pl.program_id(1)
    n_steps = pl.num_programs(1)
    w = q_ref.shape[-1]
    n_hj = 2 * N_ATT_HEADS
    rows = n_hj * t_new
    blk = pps * PAGE_SIZE

    row_id = lax.broadcasted_iota(jnp.int32, (rows, 1), 0)
    head = row_id // (2 * t_new)
    slope = jnp.zeros((rows, 1), F32)
    for h in range(N_ATT_HEADS):
        slope = jnp.where(head == h, 2.0 ** -(h + 1), slope)
    qpos = (past + row_id % t_new).astype(F32)

    @pl.when(j == 0)
    def _():
        q = q_ref[0]
        lane = lax.broadcasted_iota(jnp.int32, (t_new, w), 1)
        for hj in range(n_hj):
            qbd_ref[hj * t_new:(hj + 1) * t_new, :] = jnp.where(
                (lane >= D_QK * hj) & (lane < D_QK * (hj + 1)), q, jnp.zeros_like(q))
        m_ref[...] = jnp.full((rows, 1), NEG_INF, F32)
        l_ref[...] = jnp.zeros((rows, 1), F32)
        acc_ref[...] = jnp.zeros((rows, w), F32)

    def update(s, vmat, v_contract_dim):
        m_old = m_ref[...]
        m_new = jnp.maximum(m_old, jnp.max(s, axis=-1, keepdims=True))
        p = jnp.exp(s - m_new)
        corr = jnp.exp(m_old - m_new)
        l_ref[...] = corr * l_ref[...] + jnp.sum(p, axis=-1, keepdims=True)
        pv = lax.dot_general(p.astype(vmat.dtype), vmat, (((1,), (v_contract_dim,)), ((), ())),
                             preferred_element_type=F32)
        acc_ref[...] = corr * acc_ref[...] + pv
        m_ref[...] = m_new

    kt_all = jnp.concatenate([kr[0].astype(BF16) for kr in k_refs], axis=1)
    vt_all = jnp.concatenate([vr[0].astype(BF16) for vr in v_refs], axis=1)
    s = jnp.dot(qbd_ref[...], kt_all, preferred_element_type=F32)
    kpos = (j * blk + lax.broadcasted_iota(jnp.int32, (1, blk), 1)).astype(F32)
    update(s - slope * (qpos - kpos), vt_all, 1)

    @pl.when(j == n_steps - 1)
    def _():
        kn = kn_ref[0]
        sn = lax.dot_general(qbd_ref[...].astype(F32), kn, (((1,), (1,)), ((), ())), preferred_element_type=F32)
        kposn = (past + lax.broadcasted_iota(jnp.int32, (1, t_new), 1)).astype(F32)
        dist = qpos - kposn
        update(jnp.where(dist >= 0, sn - slope * dist, NEG_INF), vn_ref[0], 0)

        lam = _diff_lambda(lq1_ref[...], lk1_ref[...], lq2_ref[...], lk2_ref[...], lam_init)
        o = acc_ref[...] / l_ref[...]
        lane = lax.broadcasted_iota(jnp.int32, (t_new, w), 1)
        a = jnp.zeros((t_new, w), F32)
        for h in range(N_ATT_HEADS):
            o1 = o[(2 * h) * t_new:(2 * h + 1) * t_new]
            o2 = o[(2 * h + 1) * t_new:(2 * h + 2) * t_new]
            a = jnp.where((lane >= D_V * h) & (lane < D_V * (h + 1)), o1 - lam * o2, a)
        a2 = a * a
        ms = jnp.zeros((t_new, w), F32)
        for h in range(N_ATT_HEADS):
            in_h = (lane >= D_V * h) & (lane < D_V * (h + 1))
            ssh = jnp.sum(jnp.where(in_h, a2, 0.0), axis=-1, keepdims=True)
            ms = jnp.where(in_h, ssh, ms)
        o_ref[0] = (a * lax.rsqrt(ms * (1.0 / D_V) + EPS) * g_ref[...] * (1.0 - lam_init)).astype(BF16)


def _attn_sample(page_table, qb, kb, vb, cache_k, cache_v, lam_params, g_subln, lam_init, *, pps=8):
    bd, t_new, w = qb.shape
    n_pages = page_table.shape[1]
    n_pool = cache_k.shape[0]
    past = n_pages * PAGE_SIZE
    ck = jnp.transpose(cache_k, (0, 2, 3, 1)).reshape(n_pool, w, PAGE_SIZE)
    cv = jnp.transpose(cache_v, (0, 2, 3, 1)).reshape(n_pool, w, PAGE_SIZE)
    g8 = jnp.tile(g_subln, N_ATT_HEADS).reshape(1, w)
    new_spec = pl.BlockSpec((1, t_new, w), lambda b, j, pt: (b, 0, 0))
    small = lambda shape: pl.BlockSpec(shape, lambda b, j, pt: (0, 0))

    def page_spec(i):
        return pl.BlockSpec((1, w, PAGE_SIZE), lambda b, j, pt: (pt[b, j * pps + i], 0, 0))

    grid_spec = pltpu.PrefetchScalarGridSpec(
        num_scalar_prefetch=1,
        grid=(bd, n_pages // pps),
        in_specs=[new_spec, new_spec, new_spec,
                  small((1, D_QK)), small((1, D_QK)), small((1, D_QK)), small((1, D_QK)), small((1, w)),
                  *[page_spec(i) for i in range(pps)],
                  *[page_spec(i) for i in range(pps)]],
        out_specs=pl.BlockSpec((1, t_new, w), lambda b, j, pt: (b, 0, 0)),
        scratch_shapes=[pltpu.VMEM((2 * N_ATT_HEADS * t_new, w), BF16),
                        pltpu.VMEM((2 * N_ATT_HEADS * t_new, 1), F32),
                        pltpu.VMEM((2 * N_ATT_HEADS * t_new, 1), F32),
                        pltpu.VMEM((2 * N_ATT_HEADS * t_new, w), F32)],
    )
    return pl.pallas_call(
        functools.partial(_attn_sample_kernel, pps=pps, t_new=t_new, past=past, lam_init=lam_init),
        grid_spec=grid_spec,
        out_shape=jax.ShapeDtypeStruct((bd, t_new, w), BF16),
        compiler_params=_cparams("arbitrary", "arbitrary"),
        name="attn_sample",
    )(page_table, qb, kb, vb, *[p.reshape(1, D_QK) for p in lam_params], g8,
      *([ck] * pps), *([cv] * pps))


def _outproj_kernel(a_ref, u_ref, vs_ref, x_ref, ga1_ref, sc2_ref, sh2_ref, wcat_ref, sbias_ref, wout_ref,
                    gpost_ref, gpre_ref, wr_hi_ref, wr_lo_ref, br_ref,
                    x1_ref, h2_ref, gates_ref, sg_ref):
    tm = x_ref.shape[0]
    sgu_w = u_ref.shape[1]
    lane = lax.broadcasted_iota(jnp.int32, (CHUNK, LANES), 1)
    for ci in range(tm // CHUNK):
        rs = slice(ci * CHUNK, (ci + 1) * CHUNK)
        for gp in range(sgu_w // LANES):
            cs = slice(gp * LANES, (gp + 1) * LANES)
            vp = vs_ref[rs, cs]
            zero = jnp.zeros_like(vp)
            rhs = jnp.concatenate([jnp.where(lane < SGU_GC, vp, zero), jnp.where(lane >= SGU_GC, vp, zero)], axis=0)
            mixed = jnp.dot(wcat_ref[gp], rhs, preferred_element_type=F32) + sbias_ref[:, cs]
            sg_ref[rs, cs] = (u_ref[rs, cs].astype(F32) * mixed).astype(BF16)
    att_w = a_ref.shape[1]
    mix = (jnp.dot(a_ref[...], wout_ref[0:att_w, :], preferred_element_type=F32)
           + jnp.dot(sg_ref[...], wout_ref[att_w:, :], preferred_element_type=F32))
    x1 = x_ref[...] + ga1_ref[0] * _rms(mix, gpost_ref[...])
    x1_ref[...] = x1
    h2 = _rms(x1, gpre_ref[...]) * (1.0 + sc2_ref[0]) + sh2_ref[0]
    h2_hi = h2.astype(BF16)
    h2_ref[...] = h2_hi
    h2_lo = (h2 - h2_hi.astype(F32)).astype(BF16)
    logits = (jnp.dot(h2_hi, wr_hi_ref[...], preferred_element_type=F32)
              + jnp.dot(h2_lo, wr_hi_ref[...], preferred_element_type=F32)
              + jnp.dot(h2_hi, wr_lo_ref[...], preferred_element_type=F32)) + br_ref[...]
    ln = lax.broadcasted_iota(jnp.int32, (tm, LANES), 1)
    big = jnp.int32(LANES)
    is_g = (ln >= N_EXPERTS) & (ln < N_EXPERTS + N_EXPERT_GROUPS)
    gl = jnp.where(is_g, logits, NEG_INF)
    gmax = jnp.max(gl, axis=-1, keepdims=True)
    g_idx = jnp.min(jnp.where(gl == gmax, ln, big), axis=-1, keepdims=True) - N_EXPERTS
    g_w = 1.0 / jnp.sum(jnp.exp(gl - gmax), axis=-1, keepdims=True)
    in_grp = (ln >= g_idx * EXPERTS_PER_GROUP) & (ln < (g_idx + 1) * EXPERTS_PER_GROUP)
    el = jnp.where(in_grp, logits, NEG_INF)
    emax = jnp.max(el, axis=-1, keepdims=True)
    ez = jnp.exp(el - emax)
    prob = ez / jnp.sum(ez, axis=-1, keepdims=True)
    p1 = jnp.max(prob, axis=-1, keepdims=True)
    i1 = jnp.min(jnp.where((prob == p1) & in_grp, ln, big), axis=-1, keepdims=True)
    rest = jnp.where(in_grp & (ln != i1), prob, -1.0)
    p2 = jnp.max(rest, axis=-1, keepdims=True)
    i2 = jnp.min(jnp.where(rest == p2, ln, big), axis=-1, keepdims=True)
    den = p1 + p2
    gates = jnp.where(ln == i1, p1 / den * g_w, 0.0) + jnp.where(ln == i2, p2 / den * g_w, 0.0)
    gates_ref[...] = gates[:, :N_EXPERTS]


def _outproj(a, u, vsb, x, ga1, sc2, sh2, wcat, sbias, w_out_bf, g_post, g_pre, wr_hi, wr_lo, br,
             *, tm, tiles_per_mod):
    n, d = x.shape
    att_w = a.shape[1]
    sgu_w = u.shape[1]
    r = ga1.shape[1]
    row = lambda i: (i, 0)
    modmap = lambda i: (i // tiles_per_mod, 0, 0)
    const = lambda i: (0, 0)
    return pl.pallas_call(
        _outproj_kernel,
        grid=(n // tm,),
        in_specs=[pl.BlockSpec((tm, att_w), row),
                  pl.BlockSpec((tm, sgu_w), row),
                  pl.BlockSpec((tm, sgu_w), row),
                  pl.BlockSpec((tm, d), row),
                  pl.BlockSpec((1, r, d), modmap),
                  pl.BlockSpec((1, r, d), modmap),
                  pl.BlockSpec((1, r, d), modmap),
                  pl.BlockSpec(wcat.shape, lambda i: (0, 0, 0)),
                  pl.BlockSpec(sbias.shape, const),
                  pl.BlockSpec(w_out_bf.shape, const),
                  pl.BlockSpec((1, d), const),
                  pl.BlockSpec((1, d), const),
                  pl.BlockSpec(wr_hi.shape, const),
                  pl.BlockSpec(wr_lo.shape, const),
                  pl.BlockSpec((1, LANES), const)],
        out_specs=[pl.BlockSpec((tm, d), row),
                   pl.BlockSpec((tm, d), row),
                   pl.BlockSpec((tm, N_EXPERTS), row)],
        out_shape=[jax.ShapeDtypeStruct((n, d), F32),
                   jax.ShapeDtypeStruct((n, d), BF16),
                   jax.ShapeDtypeStruct((n, N_EXPERTS), F32)],
        scratch_shapes=[pltpu.VMEM((tm, sgu_w), BF16)],
        compiler_params=_cparams("arbitrary"),
        name="outproj",
    )(a, u, vsb, x, ga1, sc2, sh2, wcat, sbias, w_out_bf, g_post.reshape(1, d), g_pre.reshape(1, d),
      wr_hi, wr_lo, br)


def _moe_kernel(h_ref, gates_ref, x1_ref, ga2_ref, gpost_ref, wg_ref, wu_ref, wd_ref, o_ref, acc_ref):
    e = pl.program_id(1)

    @pl.when(e == 0)
    def _():
        acc_ref[...] = jnp.zeros_like(acc_ref)

    h = h_ref[...]
    gate = jnp.dot(h, wg_ref[0], preferred_element_type=F32)
    up = jnp.dot(h, wu_ref[0], preferred_element_type=F32)
    hdn = (gate * jax.nn.sigmoid(gate) * up).astype(BF16)
    y = jnp.dot(hdn, wd_ref[0], preferred_element_type=F32)
    ln = lax.broadcasted_iota(jnp.int32, gates_ref.shape, 1)
    ge = jnp.sum(jnp.where(ln == e, gates_ref[...], 0.0), axis=-1, keepdims=True)
    acc_ref[...] += ge * y

    @pl.when(e == pl.num_programs(1) - 1)
    def _():
        o_ref[...] = x1_ref[...] + ga2_ref[0] * _rms(acc_ref[...], gpost_ref[...])


def _moe(h2, gates, x1, ga2, g_post, wg_bf, wu_bf, wd_bf, *, tm, tiles_per_mod):
    n, d = x1.shape
    n_e, _, de = wg_bf.shape
    r = ga2.shape[1]
    row = lambda i, e: (i, 0)
    return pl.pallas_call(
        _moe_kernel,
        grid=(n // tm, n_e),
        in_specs=[pl.BlockSpec((tm, d), row),
                  pl.BlockSpec((tm, n_e), row),
                  pl.BlockSpec((tm, d), row),
                  pl.BlockSpec((1, r, d), lambda i, e: (i // tiles_per_mod, 0, 0)),
                  pl.BlockSpec((1, d), lambda i, e: (0, 0)),
                  pl.BlockSpec((1, d, de), lambda i, e: (e, 0, 0)),
                  pl.BlockSpec((1, d, de), lambda i, e: (e, 0, 0)),
                  pl.BlockSpec((1, de, d), lambda i, e: (e, 0, 0))],
        out_specs=pl.BlockSpec((tm, d), row),
        out_shape=jax.ShapeDtypeStruct((n, d), F32),
        scratch_shapes=[pltpu.VMEM((tm, d), F32)],
        compiler_params=_cparams("arbitrary", "arbitrary"),
        name="moe",
    )(h2, gates, x1, ga2, g_post.reshape(1, d), wg_bf, wu_bf, wd_bf)


def _sgu_weights(ws, bs, chunk_len):
    causal = jnp.tril(jnp.ones((chunk_len, chunk_len), ws.dtype))
    wm = ws[:, :chunk_len, :chunk_len] * causal
    reps = CHUNK // chunk_len
    if reps > 1:
        eye = jnp.eye(reps, dtype=ws.dtype)
        wm = jnp.einsum("ab,gts->gatbs", eye, wm).reshape(N_SGU_GROUPS, CHUNK, CHUNK)
    wcat = jnp.concatenate([wm[0::2], wm[1::2]], axis=2).astype(BF16)
    bt = jnp.tile(bs[:, :chunk_len].T, (reps, 1))
    sbias = jnp.repeat(bt, SGU_GC, axis=1)
    return wcat, sbias


def _split_mod(mod):
    return [m[:, None, :] for m in jnp.split(mod, 6, axis=-1)]


def kernel(x_prompt, x_sample, cache_k, cache_v, page_table, c_prompt, c_sample, w_ada, b_ada, g_pre_mix, g_post_mix, g_pre_ffn, g_post_ffn, w_in, lam_q1, lam_k1, lam_q2, lam_k2, g_subln, sgu_ln_g, sgu_ln_b, sgu_ws, sgu_bs, w_out, w_rg, b_rg, w_re, b_re, w_gate, w_up, w_down):
    depth = w_in.shape[0]
    assert depth == 1
    l = 0
    lam_init = 0.8 - 0.6 * math.exp(-0.3 * l)
    bp, sp, d = x_prompt.shape
    bs_, ts, _ = x_sample.shape
    n_s = bs_ * ts

    mod = _ada(jnp.concatenate([c_prompt, c_sample], axis=0), w_ada[l], b_ada[l])
    sh1p, sc1p, ga1p, sh2p, sc2p, ga2p = _split_mod(mod[:bp])
    rep = lambda m: jnp.repeat(m, ts, axis=0).reshape(1, n_s, d)
    sh1s, sc1s, ga1s, sh2s, sc2s, ga2s = [rep(m) for m in jnp.split(mod[bp:], 6, axis=-1)]

    w_in_bf = w_in[l].astype(BF16)
    w_out_bf = w_out[l].astype(BF16)
    wg_bf = w_gate[l].astype(BF16)
    wu_bf = w_up[l].astype(BF16)
    wd_bf = w_down[l].astype(BF16)
    wr = jnp.concatenate([w_re[l], w_rg[l]], axis=1)
    wr = jnp.pad(wr, ((0, 0), (0, LANES - wr.shape[1])))
    wr_hi = wr.astype(BF16)
    wr_lo = (wr - wr_hi.astype(F32)).astype(BF16)
    br = jnp.pad(jnp.concatenate([b_re[l], b_rg[l]]), (0, LANES - N_EXPERTS - N_EXPERT_GROUPS)).reshape(1, LANES)
    lam_params = (lam_q1[l], lam_k1[l], lam_q2[l], lam_k2[l])

    tm_p = 512
    xp = x_prompt.reshape(bp * sp, d)
    kp, vp, qb, kb, vb, u, vsb, vs_last = _inproj(
        xp, sc1p, sh1p, g_pre_mix[l], w_in_bf, sgu_ln_g[l], sgu_ln_b[l],
        tm=tm_p, tiles_per_mod=sp // tm_p, tiles_per_group=sp // tm_p, last_rows=CHUNK)
    w3 = qb.shape[1]
    a = _attn_prompt(qb.reshape(bp, sp, w3), kb.reshape(bp, sp, w3), vb.reshape(bp, sp, w3),
                     lam_params, g_subln[l], lam_init)
    wcat_p, sbias_p = _sgu_weights(sgu_ws[l], sgu_bs[l], CHUNK)
    x1, h2, gates = _outproj(a.reshape(bp * sp, w3), u, vsb, xp, ga1p, sc2p, sh2p, wcat_p, sbias_p, w_out_bf,
                             g_post_mix[l], g_pre_ffn[l], wr_hi, wr_lo, br, tm=tm_p, tiles_per_mod=sp // tm_p)
    tm_moe = 1024
    yp = _moe(h2, gates, x1, ga2p, g_post_ffn[l], wg_bf, wu_bf, wd_bf, tm=tm_moe, tiles_per_mod=sp // tm_moe)

    xs = x_sample.reshape(n_s, d)
    ks, vs_, qbs, kbs, vbs, us, vsbs, vs_last_s = _inproj(
        xs, sc1s, sh1s, g_pre_mix[l], w_in_bf, sgu_ln_g[l], sgu_ln_b[l],
        tm=n_s, tiles_per_mod=1, tiles_per_group=1, last_rows=n_s)
    a_s = _attn_sample(page_table, qbs.reshape(bs_, ts, w3), ks.reshape(bs_, ts, w3), vs_.reshape(bs_, ts, w3),
                       cache_k[l], cache_v[l], lam_params, g_subln[l], lam_init)
    wcat_s, sbias_s = _sgu_weights(sgu_ws[l], sgu_bs[l], ts)
    x1s, h2s, gates_s = _outproj(a_s.reshape(n_s, w3), us, vsbs, xs, ga1s, sc2s, sh2s, wcat_s, sbias_s, w_out_bf,
                                 g_post_mix[l], g_pre_ffn[l], wr_hi, wr_lo, br, tm=n_s, tiles_per_mod=1)
    ys = _moe(h2s, gates_s, x1s, ga2s, g_post_ffn[l], wg_bf, wu_bf, wd_bf, tm=n_s, tiles_per_mod=1)

    return (yp.reshape(bp, sp, d), ys.reshape(bs_, ts, d),
            kp.reshape(1, bp, sp, N_ATT_HEADS, 2 * D_QK), vp.reshape(1, bp, sp, N_ATT_HEADS, D_V),
            vs_last.reshape(1, bp, CHUNK, -1),
            ks.reshape(1, bs_, ts, N_ATT_HEADS, 2 * D_QK), vs_.reshape(1, bs_, ts, N_ATT_HEADS, D_V),
            vs_last_s.reshape(1, bs_, ts, -1))
```

```python
import functools
import math
import struct

import jax
import jax.numpy as jnp
from jax import lax
from jax.experimental import pallas as pl
from jax.experimental.pallas import tpu as pltpu

F32 = jnp.float32
BF16 = jnp.bfloat16

EPS = 1e-6
N_ATT_HEADS = 8
D_QK = 32
D_V = 64
ATT_W = N_ATT_HEADS * D_V
N_SGU_GROUPS = 8
SGU_GC = 64
SGU_W = N_SGU_GROUPS * SGU_GC
CHUNK = 128
N_EXPERT_GROUPS = 4
EXPERTS_PER_GROUP = 4
N_EXPERTS = 16
PAGE_SIZE = 128
LANES = 128
VMEM_LIMIT = 56 * 1024 * 1024
NEG_INF = float("-inf")
LOG2E = math.log2(math.e)


def _cparams(*sem):
    return pltpu.CompilerParams(dimension_semantics=sem, vmem_limit_bytes=VMEM_LIMIT)


def _rms(x, g):
    return x * lax.rsqrt(jnp.mean(x * x, axis=-1, keepdims=True) + EPS) * g


def _ada_kernel(c_ref, w_ref, b_ref, o_ref):
    c = c_ref[...]
    s = c * jax.nn.sigmoid(c)
    o_ref[...] = jnp.dot(s.astype(BF16), w_ref[...].astype(BF16), preferred_element_type=F32) + b_ref[...]


def _ada(c, w_ada, b_ada):
    n, d = c.shape
    nout = w_ada.shape[1]
    tn = d
    return pl.pallas_call(
        _ada_kernel,
        grid=(nout // tn,),
        in_specs=[pl.BlockSpec((n, d), lambda j: (0, 0)),
                  pl.BlockSpec((d, tn), lambda j: (0, j)),
                  pl.BlockSpec((1, tn), lambda j: (0, j))],
        out_specs=pl.BlockSpec((n, tn), lambda j: (0, j)),
        out_shape=jax.ShapeDtypeStruct((n, nout), F32),
        compiler_params=_cparams("arbitrary"),
        name="ada",
    )(c, w_ada, b_ada.reshape(1, nout))


Q_SCALE = D_QK ** -0.5 * LOG2E


def _inproj_kernel(x_ref, sc_ref, sh_ref, g_ref, w_ref, wqv_t_ref, lng_ref, lnb_ref, *out_refs,
                   last_rows, transposed_qv):
    x = x_ref[...]
    h = (_rms(x, g_ref[...]) * (1.0 + sc_ref[0]) + sh_ref[0]).astype(BF16)

    def proj(lo):
        return jnp.dot(h, w_ref[:, lo:lo + ATT_W], preferred_element_type=F32)

    def proj_t(lo):
        return lax.dot_general(wqv_t_ref[lo:lo + ATT_W, :], h, (((1,), (1,)), ((), ())),
                               preferred_element_type=F32)

    zk = proj(ATT_W)
    if transposed_qv:
        k_ref, vt_ref, qtb_ref, kb_ref, vtb_ref, u_ref, vsb_ref, vsl_ref = out_refs
        k_ref[...] = zk
        kb_ref[...] = zk.astype(BF16)
        qtb_ref[0] = (proj_t(0) * Q_SCALE).astype(BF16)
        zvt = proj_t(ATT_W)
        vt_ref[0] = zvt
        vtb_ref[0, 0] = zvt.astype(BF16)
    else:
        k_ref, v_ref, qb_ref, u_ref, vsb_ref, vsl_ref = out_refs
        k_ref[...] = zk
        qb_ref[...] = (proj(0) * Q_SCALE).astype(BF16)
        v_ref[...] = proj(2 * ATT_W)
    u_ref[...] = jax.nn.gelu(proj(3 * ATT_W)).astype(BF16)
    gs = jax.nn.gelu(proj(3 * ATT_W + SGU_W))
    mu = jnp.mean(gs, axis=-1, keepdims=True)
    xc = gs - mu
    vs = xc * lax.rsqrt(jnp.mean(xc * xc, axis=-1, keepdims=True) + EPS) * lng_ref[...] + lnb_ref[...]
    vsb_ref[...] = vs.astype(BF16)
    tm = vs.shape[0]
    vsl_ref[0] = vs[tm - last_rows:, :]


def _inproj(x, sc, sh, g, w_in_bf, wqv_t_bf, ln_g, ln_b, *, tm, tiles_per_seq, last_rows, transposed_qv):
    n, d = x.shape
    in_w = w_in_bf.shape[1]
    r = sc.shape[1]
    n_tiles = n // tm
    n_seq = n_tiles // tiles_per_seq
    s_len = tiles_per_seq * tm
    row = lambda i: (i, 0)
    modmap = lambda i: (i // tiles_per_seq, 0, 0)
    const = lambda i: (0, 0)
    col_t = lambda i: (i // tiles_per_seq, 0, i % tiles_per_seq)
    row_spec = pl.BlockSpec((tm, ATT_W), row)
    tail = [(pl.BlockSpec((tm, SGU_W), row), jax.ShapeDtypeStruct((n, SGU_W), BF16)),
            (pl.BlockSpec((tm, SGU_W), row), jax.ShapeDtypeStruct((n, SGU_W), BF16)),
            (pl.BlockSpec((1, last_rows, SGU_W), lambda i: (i // tiles_per_seq, 0, 0)),
             jax.ShapeDtypeStruct((n_seq, last_rows, SGU_W), F32))]
    if transposed_qv:
        head = [(row_spec, jax.ShapeDtypeStruct((n, ATT_W), F32)),
                (pl.BlockSpec((1, ATT_W, tm), col_t), jax.ShapeDtypeStruct((n_seq, ATT_W, s_len), F32)),
                (pl.BlockSpec((1, ATT_W, tm), col_t), jax.ShapeDtypeStruct((n_seq, ATT_W, s_len), BF16)),
                (row_spec, jax.ShapeDtypeStruct((n, ATT_W), BF16)),
                (pl.BlockSpec((1, 1, ATT_W, tm), lambda i: (i // tiles_per_seq, i % tiles_per_seq, 0, 0)),
                 jax.ShapeDtypeStruct((n_seq, tiles_per_seq, ATT_W, tm), BF16))]
    else:
        head = [(row_spec, jax.ShapeDtypeStruct((n, ATT_W), F32)),
                (row_spec, jax.ShapeDtypeStruct((n, ATT_W), F32)),
                (row_spec, jax.ShapeDtypeStruct((n, ATT_W), BF16))]
    specs, shapes = zip(*(head + tail))
    return pl.pallas_call(
        functools.partial(_inproj_kernel, last_rows=last_rows, transposed_qv=transposed_qv),
        grid=(n_tiles,),
        in_specs=[pl.BlockSpec((tm, d), row),
                  pl.BlockSpec((1, r, d), modmap),
                  pl.BlockSpec((1, r, d), modmap),
                  pl.BlockSpec((1, d), const),
                  pl.BlockSpec((d, in_w), const),
                  pl.BlockSpec(wqv_t_bf.shape, const),
                  pl.BlockSpec((1, SGU_W), const),
                  pl.BlockSpec((1, SGU_W), const)],
        out_specs=list(specs),
        out_shape=list(shapes),
        compiler_params=_cparams("arbitrary"),
        name="inproj",
    )(x, sc, sh, g.reshape(1, d), w_in_bf, wqv_t_bf, ln_g.reshape(1, SGU_W), ln_b.reshape(1, SGU_W))


def _diff_lambda(lq1, lk1, lq2, lk2, lam_init):
    return (jnp.exp(jnp.sum(lq1 * lk1, axis=-1, keepdims=True))
            - jnp.exp(jnp.sum(lq2 * lk2, axis=-1, keepdims=True)) + lam_init)


def _pair_slopes(hp):
    s0 = jnp.where(hp == 0, 2.0 ** -1, jnp.where(hp == 1, 2.0 ** -3, jnp.where(hp == 2, 2.0 ** -5, 2.0 ** -7)))
    return s0.astype(F32), (s0 * 0.5).astype(F32)


def _bf16_round(x):
    bits = struct.unpack("<I", struct.pack("<f", x))[0]
    bits = (bits + 0x7FFF + ((bits >> 16) & 1)) & 0xFFFF0000
    return struct.unpack("<f", struct.pack("<I", bits))[0]


L2E_PARTS = []
_rest = LOG2E
for _ in range(3):
    L2E_PARTS.append(_bf16_round(_rest))
    _rest -= L2E_PARTS[-1]
POS_SPLIT = 64
N_POS_FEATS = 2 * len(L2E_PARTS)
EXTRA_ROWS = 16


def _attn_prompt_kernel(qt_ref, k_ref, vt_ref, pf_ref, rel_ref, lq1_ref, lk1_ref, lq2_ref, lk2_ref, g_ref, o_ref,
                        qst_ref, m_ref, acc_ref, *, tq, tk, lam_init):
    hp = pl.program_id(1)
    qi = pl.program_id(2)
    cols = 4 * tq
    qt = qt_ref[0]
    sub = lax.broadcasted_iota(jnp.int32, (LANES, tq), 0)
    for c in range(4):
        qst_ref[0:LANES, c * tq:(c + 1) * tq] = jnp.where((sub >= D_QK * c) & (sub < D_QK * (c + 1)), qt,
                                                          jnp.zeros_like(qt))
    s0, s1 = _pair_slopes(hp)
    col = lax.broadcasted_iota(jnp.int32, (EXTRA_ROWS, cols), 1)
    row = lax.broadcasted_iota(jnp.int32, (EXTRA_ROWS, cols), 0)
    feat = jnp.zeros((EXTRA_ROWS, cols), F32)
    for i, part in enumerate(L2E_PARTS):
        feat = jnp.where(row == i, POS_SPLIT * part, feat)
        feat = jnp.where(row == len(L2E_PARTS) + i, part, feat)
    qst_ref[LANES:LANES + EXTRA_ROWS, :] = (jnp.where(col < 2 * tq, s0, s1) * feat).astype(BF16)
    qst_ref[LANES + EXTRA_ROWS:, :] = jnp.zeros((LANES - EXTRA_ROWS, cols), BF16)
    m_ref[...] = jnp.full((1, cols), NEG_INF, F32)
    acc_ref[...] = jnp.zeros((LANES + EXTRA_ROWS, cols), F32)

    def chunk(kj, masked):
        start = pl.multiple_of(kj * tk, tk)
        kaug = jnp.concatenate([k_ref[0, pl.ds(start, tk), :], pf_ref[pl.ds(start, tk), :]], axis=1)
        st = jnp.dot(kaug, qst_ref[...], preferred_element_type=F32)
        if masked:
            off = (kj * tk - qi * tq).astype(F32)
            st = jnp.where(rel_ref[...] + off <= 0.0, st, NEG_INF)
        m_old = m_ref[...]
        m_new = jnp.maximum(m_old, jnp.max(st, axis=0, keepdims=True))
        pt = jnp.exp2(st - m_new).astype(BF16)
        corr = jnp.exp2(m_old - m_new)
        vaug = jnp.concatenate([vt_ref[0, kj], jnp.ones((EXTRA_ROWS, tk), BF16)], axis=0)
        acc_ref[...] = acc_ref[...] * corr + jnp.dot(vaug, pt, preferred_element_type=F32)
        m_ref[...] = m_new

    n_full = (qi * tq) // tk

    def body(kj, carry):
        chunk(kj, False)
        return carry

    lax.fori_loop(0, n_full, body, 0)
    chunk(n_full, True)

    lam = _diff_lambda(lq1_ref[...], lk1_ref[...], lq2_ref[...], lk2_ref[...], lam_init)
    ot = acc_ref[0:LANES, :] / acc_ref[LANES:LANES + 1, :]
    at = jnp.where(sub < D_V, ot[:, 0:tq] - lam * ot[:, tq:2 * tq], ot[:, 2 * tq:3 * tq] - lam * ot[:, 3 * tq:])
    a2 = at * at
    ss0 = jnp.sum(a2[0:D_V], axis=0, keepdims=True)
    ss1 = jnp.sum(a2[D_V:], axis=0, keepdims=True)
    ms = jnp.where(sub < D_V, ss0, ss1) * (1.0 / D_V)
    an = at * lax.rsqrt(ms + EPS) * g_ref[...] * (1.0 - lam_init)
    o_ref[0] = an.T.astype(BF16)


def _attn_prompt(qtb, kb, vtb, lam_params, g_subln, lam_init, *, tq=256):
    b, w, s = qtb.shape
    n_chunks, tk = vtb.shape[1], vtb.shape[3]
    n_pairs = w // LANES
    cols = 4 * tq
    g2 = jnp.concatenate([g_subln, g_subln]).reshape(LANES, 1)
    pos = jnp.arange(s, dtype=jnp.int32)
    hi, lo = (pos // POS_SPLIT).astype(BF16), (pos % POS_SPLIT).astype(BF16)
    n_parts = len(L2E_PARTS)
    pf = jnp.stack([hi] * n_parts + [lo] * n_parts + [jnp.zeros_like(hi)] * (LANES - N_POS_FEATS), axis=1)
    rel = (jnp.arange(tk, dtype=jnp.int32)[:, None] - (jnp.arange(cols, dtype=jnp.int32) % tq)[None, :]).astype(F32)
    const2 = lambda bi, hp, qi: (0, 0)
    lam_specs = [pl.BlockSpec((1, D_QK), const2)] * 4
    return pl.pallas_call(
        functools.partial(_attn_prompt_kernel, tq=tq, tk=tk, lam_init=lam_init),
        grid=(b, n_pairs, s // tq),
        in_specs=[pl.BlockSpec((1, LANES, tq), lambda bi, hp, qi: (bi, hp, qi)),
                  pl.BlockSpec((1, s, LANES), lambda bi, hp, qi: (bi, 0, hp)),
                  pl.BlockSpec((1, n_chunks, LANES, tk), lambda bi, hp, qi: (bi, 0, hp, 0)),
                  pl.BlockSpec((s, LANES), const2),
                  pl.BlockSpec((tk, cols), const2),
                  *lam_specs,
                  pl.BlockSpec((LANES, 1), const2)],
        out_specs=pl.BlockSpec((1, tq, LANES), lambda bi, hp, qi: (bi, qi, hp)),
        out_shape=jax.ShapeDtypeStruct((b, s, w), BF16),
        scratch_shapes=[pltpu.VMEM((2 * LANES, cols), BF16),
                        pltpu.VMEM((1, cols), F32),
                        pltpu.VMEM((LANES + EXTRA_ROWS, cols), F32)],
        compiler_params=_cparams("arbitrary", "arbitrary", "arbitrary"),
        name="attn_prompt",
    )(qtb, kb, vtb, pf, rel, *[p.reshape(1, D_QK) for p in lam_params], g2)


def _attn_sample_kernel(pt_ref, q_ref, kn_ref, vn_ref, lq1_ref, lk1_ref, lq2_ref, lk2_ref, g_ref, *rest,
                        pps, t_new, past, lam_init):
    k_refs = rest[:pps]
    v_refs = rest[pps:2 * pps]
    o_ref = rest[2 * pps]
    qbd_ref, m_ref, l_ref, acc_ref = rest[2 * pps + 1:]
    j = pl.program_id(1)
    n_steps = pl.num_programs(1)
    w = q_ref.shape[-1]
    n_hj = 2 * N_ATT_HEADS
    rows = n_hj * t_new
    blk = pps * PAGE_SIZE

    row_id = lax.broadcasted_iota(jnp.int32, (rows, 1), 0)
    head = row_id // (2 * t_new)
    slope = jnp.zeros((rows, 1), F32)
    for h in range(N_ATT_HEADS):
        slope = jnp.where(head == h, 2.0 ** -(h + 1), slope)
    slope = slope * LOG2E
    qpos = (past + row_id % t_new).astype(F32)

    @pl.when(j == 0)
    def _():
        q = q_ref[0]
        lane = lax.broadcasted_iota(jnp.int32, (t_new, w), 1)
        for hj in range(n_hj):
            qbd_ref[hj * t_new:(hj + 1) * t_new, :] = jnp.where(
                (lane >= D_QK * hj) & (lane < D_QK * (hj + 1)), q, jnp.zeros_like(q))
        m_ref[...] = jnp.full((rows, 1), NEG_INF, F32)
        l_ref[...] = jnp.zeros((rows, 1), F32)
        acc_ref[...] = jnp.zeros((rows, w), F32)

    def update(s, vmat, v_contract_dim):
        m_old = m_ref[...]
        m_new = jnp.maximum(m_old, jnp.max(s, axis=-1, keepdims=True))
        p = jnp.exp2(s - m_new)
        corr = jnp.exp2(m_old - m_new)
        l_ref[...] = corr * l_ref[...] + jnp.sum(p, axis=-1, keepdims=True)
        pv = lax.dot_general(p.astype(vmat.dtype), vmat, (((1,), (v_contract_dim,)), ((), ())),
                             preferred_element_type=F32)
        acc_ref[...] = corr * acc_ref[...] + pv
        m_ref[...] = m_new

    kt_all = jnp.concatenate([kr[0].astype(BF16) for kr in k_refs], axis=1)
    vt_all = jnp.concatenate([vr[0].astype(BF16) for vr in v_refs], axis=1)
    s = jnp.dot(qbd_ref[...], kt_all, preferred_element_type=F32)
    kpos = (j * blk + lax.broadcasted_iota(jnp.int32, (1, blk), 1)).astype(F32)
    update(s - slope * (qpos - kpos), vt_all, 1)

    @pl.when(j == n_steps - 1)
    def _():
        kn = kn_ref[0]
        sn = lax.dot_general(qbd_ref[...].astype(F32), kn, (((1,), (1,)), ((), ())), preferred_element_type=F32)
        kposn = (past + lax.broadcasted_iota(jnp.int32, (1, t_new), 1)).astype(F32)
        dist = qpos - kposn
        update(jnp.where(dist >= 0, sn - slope * dist, NEG_INF), vn_ref[0], 0)

        lam = _diff_lambda(lq1_ref[...], lk1_ref[...], lq2_ref[...], lk2_ref[...], lam_init)
        o = acc_ref[...] / l_ref[...]
        lane = lax.broadcasted_iota(jnp.int32, (t_new, w), 1)
        a = jnp.zeros((t_new, w), F32)
        for h in range(N_ATT_HEADS):
            o1 = o[(2 * h) * t_new:(2 * h + 1) * t_new]
            o2 = o[(2 * h + 1) * t_new:(2 * h + 2) * t_new]
            a = jnp.where((lane >= D_V * h) & (lane < D_V * (h + 1)), o1 - lam * o2, a)
        a2 = a * a
        ms = jnp.zeros((t_new, w), F32)
        for h in range(N_ATT_HEADS):
            in_h = (lane >= D_V * h) & (lane < D_V * (h + 1))
            ssh = jnp.sum(jnp.where(in_h, a2, 0.0), axis=-1, keepdims=True)
            ms = jnp.where(in_h, ssh, ms)
        o_ref[0] = (a * lax.rsqrt(ms * (1.0 / D_V) + EPS) * g_ref[...] * (1.0 - lam_init)).astype(BF16)


def _attn_sample(page_table, qb, kb, vb, cache_k, cache_v, lam_params, g_subln, lam_init, *, pps=8):
    bd, t_new, w = qb.shape
    n_pages = page_table.shape[1]
    n_pool = cache_k.shape[0]
    past = n_pages * PAGE_SIZE
    ck = jnp.transpose(cache_k, (0, 2, 3, 1)).reshape(n_pool, w, PAGE_SIZE)
    cv = jnp.transpose(cache_v, (0, 2, 3, 1)).reshape(n_pool, w, PAGE_SIZE)
    g8 = jnp.tile(g_subln, N_ATT_HEADS).reshape(1, w)
    new_spec = pl.BlockSpec((1, t_new, w), lambda b, j, pt: (b, 0, 0))
    small = lambda shape: pl.BlockSpec(shape, lambda b, j, pt: (0, 0))

    def page_spec(i):
        return pl.BlockSpec((1, w, PAGE_SIZE), lambda b, j, pt: (pt[b, j * pps + i], 0, 0))

    grid_spec = pltpu.PrefetchScalarGridSpec(
        num_scalar_prefetch=1,
        grid=(bd, n_pages // pps),
        in_specs=[new_spec, new_spec, new_spec,
                  small((1, D_QK)), small((1, D_QK)), small((1, D_QK)), small((1, D_QK)), small((1, w)),
                  *[page_spec(i) for i in range(pps)],
                  *[page_spec(i) for i in range(pps)]],
        out_specs=pl.BlockSpec((1, t_new, w), lambda b, j, pt: (b, 0, 0)),
        scratch_shapes=[pltpu.VMEM((2 * N_ATT_HEADS * t_new, w), BF16),
                        pltpu.VMEM((2 * N_ATT_HEADS * t_new, 1), F32),
                        pltpu.VMEM((2 * N_ATT_HEADS * t_new, 1), F32),
                        pltpu.VMEM((2 * N_ATT_HEADS * t_new, w), F32)],
    )
    return pl.pallas_call(
        functools.partial(_attn_sample_kernel, pps=pps, t_new=t_new, past=past, lam_init=lam_init),
        grid_spec=grid_spec,
        out_shape=jax.ShapeDtypeStruct((bd, t_new, w), BF16),
        compiler_params=_cparams("arbitrary", "arbitrary"),
        name="attn_sample",
    )(page_table, qb, kb, vb, *[p.reshape(1, D_QK) for p in lam_params], g8,
      *([ck] * pps), *([cv] * pps))


def _outproj_kernel(a_ref, u_ref, vs_ref, x_ref, ga1_ref, sc2_ref, sh2_ref, wcat_ref, sbias_ref, wout_ref,
                    gpost_ref, gpre_ref, wr_hi_ref, wr_lo_ref, br_ref,
                    x1_ref, h2_ref, gates_ref, sg_ref):
    tm = x_ref.shape[0]
    sgu_w = u_ref.shape[1]
    lane = lax.broadcasted_iota(jnp.int32, (CHUNK, LANES), 1)
    for ci in range(tm // CHUNK):
        rs = slice(ci * CHUNK, (ci + 1) * CHUNK)
        for gp in range(sgu_w // LANES):
            cs = slice(gp * LANES, (gp + 1) * LANES)
            vp = vs_ref[rs, cs]
            zero = jnp.zeros_like(vp)
            rhs = jnp.concatenate([jnp.where(lane < SGU_GC, vp, zero), jnp.where(lane >= SGU_GC, vp, zero)], axis=0)
            mixed = jnp.dot(wcat_ref[gp], rhs, preferred_element_type=F32) + sbias_ref[:, cs]
            sg_ref[rs, cs] = (u_ref[rs, cs].astype(F32) * mixed).astype(BF16)
    att_w = a_ref.shape[1]
    mix = (jnp.dot(a_ref[...], wout_ref[0:att_w, :], preferred_element_type=F32)
           + jnp.dot(sg_ref[...], wout_ref[att_w:, :], preferred_element_type=F32))
    x1 = x_ref[...] + ga1_ref[0] * _rms(mix, gpost_ref[...])
    x1_ref[...] = x1
    h2 = _rms(x1, gpre_ref[...]) * (1.0 + sc2_ref[0]) + sh2_ref[0]
    h2_hi = h2.astype(BF16)
    h2_ref[...] = h2_hi
    h2_lo = (h2 - h2_hi.astype(F32)).astype(BF16)
    logits = (jnp.dot(h2_hi, wr_hi_ref[...], preferred_element_type=F32)
              + jnp.dot(h2_lo, wr_hi_ref[...], preferred_element_type=F32)
              + jnp.dot(h2_hi, wr_lo_ref[...], preferred_element_type=F32)) + br_ref[...]
    ln = lax.broadcasted_iota(jnp.int32, (tm, LANES), 1)
    big = jnp.int32(LANES)
    is_g = (ln >= N_EXPERTS) & (ln < N_EXPERTS + N_EXPERT_GROUPS)
    gl = jnp.where(is_g, logits, NEG_INF)
    gmax = jnp.max(gl, axis=-1, keepdims=True)
    g_idx = jnp.min(jnp.where(gl == gmax, ln, big), axis=-1, keepdims=True) - N_EXPERTS
    g_w = 1.0 / jnp.sum(jnp.exp(gl - gmax), axis=-1, keepdims=True)
    in_grp = (ln >= g_idx * EXPERTS_PER_GROUP) & (ln < (g_idx + 1) * EXPERTS_PER_GROUP)
    el = jnp.where(in_grp, logits, NEG_INF)
    emax = jnp.max(el, axis=-1, keepdims=True)
    ez = jnp.exp(el - emax)
    prob = ez / jnp.sum(ez, axis=-1, keepdims=True)
    p1 = jnp.max(prob, axis=-1, keepdims=True)
    i1 = jnp.min(jnp.where((prob == p1) & in_grp, ln, big), axis=-1, keepdims=True)
    rest = jnp.where(in_grp & (ln != i1), prob, -1.0)
    p2 = jnp.max(rest, axis=-1, keepdims=True)
    i2 = jnp.min(jnp.where(rest == p2, ln, big), axis=-1, keepdims=True)
    den = p1 + p2
    gates = jnp.where(ln == i1, p1 / den * g_w, 0.0) + jnp.where(ln == i2, p2 / den * g_w, 0.0)
    gates_ref[...] = gates[:, :N_EXPERTS]


def _outproj(a, u, vsb, x, ga1, sc2, sh2, wcat, sbias, w_out_bf, g_post, g_pre, wr_hi, wr_lo, br,
             *, tm, tiles_per_mod):
    n, d = x.shape
    att_w = a.shape[1]
    sgu_w = u.shape[1]
    r = ga1.shape[1]
    row = lambda i: (i, 0)
    modmap = lambda i: (i // tiles_per_mod, 0, 0)
    const = lambda i: (0, 0)
    return pl.pallas_call(
        _outproj_kernel,
        grid=(n // tm,),
        in_specs=[pl.BlockSpec((tm, att_w), row),
                  pl.BlockSpec((tm, sgu_w), row),
                  pl.BlockSpec((tm, sgu_w), row),
                  pl.BlockSpec((tm, d), row),
                  pl.BlockSpec((1, r, d), modmap),
                  pl.BlockSpec((1, r, d), modmap),
                  pl.BlockSpec((1, r, d), modmap),
                  pl.BlockSpec(wcat.shape, lambda i: (0, 0, 0)),
                  pl.BlockSpec(sbias.shape, const),
                  pl.BlockSpec(w_out_bf.shape, const),
                  pl.BlockSpec((1, d), const),
                  pl.BlockSpec((1, d), const),
                  pl.BlockSpec(wr_hi.shape, const),
                  pl.BlockSpec(wr_lo.shape, const),
                  pl.BlockSpec((1, LANES), const)],
        out_specs=[pl.BlockSpec((tm, d), row),
                   pl.BlockSpec((tm, d), row),
                   pl.BlockSpec((tm, N_EXPERTS), row)],
        out_shape=[jax.ShapeDtypeStruct((n, d), F32),
                   jax.ShapeDtypeStruct((n, d), BF16),
                   jax.ShapeDtypeStruct((n, N_EXPERTS), F32)],
        scratch_shapes=[pltpu.VMEM((tm, sgu_w), BF16)],
        compiler_params=_cparams("arbitrary"),
        name="outproj",
    )(a, u, vsb, x, ga1, sc2, sh2, wcat, sbias, w_out_bf, g_post.reshape(1, d), g_pre.reshape(1, d),
      wr_hi, wr_lo, br)


def _moe_kernel(h_ref, gates_ref, x1_ref, ga2_ref, gpost_ref, wg_ref, wu_ref, wd_ref, o_ref, acc_ref):
    e = pl.program_id(1)

    @pl.when(e == 0)
    def _():
        acc_ref[...] = jnp.zeros_like(acc_ref)

    h = h_ref[...]
    gate = jnp.dot(h, wg_ref[0], preferred_element_type=F32)
    up = jnp.dot(h, wu_ref[0], preferred_element_type=F32)
    hdn = (gate * jax.nn.sigmoid(gate) * up).astype(BF16)
    y = jnp.dot(hdn, wd_ref[0], preferred_element_type=F32)
    ln = lax.broadcasted_iota(jnp.int32, gates_ref.shape, 1)
    ge = jnp.sum(jnp.where(ln == e, gates_ref[...], 0.0), axis=-1, keepdims=True)
    acc_ref[...] += ge * y

    @pl.when(e == pl.num_programs(1) - 1)
    def _():
        o_ref[...] = x1_ref[...] + ga2_ref[0] * _rms(acc_ref[...], gpost_ref[...])


def _moe(h2, gates, x1, ga2, g_post, wg_bf, wu_bf, wd_bf, *, tm, tiles_per_mod):
    n, d = x1.shape
    n_e, _, de = wg_bf.shape
    r = ga2.shape[1]
    row = lambda i, e: (i, 0)
    return pl.pallas_call(
        _moe_kernel,
        grid=(n // tm, n_e),
        in_specs=[pl.BlockSpec((tm, d), row),
                  pl.BlockSpec((tm, n_e), row),
                  pl.BlockSpec((tm, d), row),
                  pl.BlockSpec((1, r, d), lambda i, e: (i // tiles_per_mod, 0, 0)),
                  pl.BlockSpec((1, d), lambda i, e: (0, 0)),
                  pl.BlockSpec((1, d, de), lambda i, e: (e, 0, 0)),
                  pl.BlockSpec((1, d, de), lambda i, e: (e, 0, 0)),
                  pl.BlockSpec((1, de, d), lambda i, e: (e, 0, 0))],
        out_specs=pl.BlockSpec((tm, d), row),
        out_shape=jax.ShapeDtypeStruct((n, d), F32),
        scratch_shapes=[pltpu.VMEM((tm, d), F32)],
        compiler_params=_cparams("arbitrary", "arbitrary"),
        name="moe",
    )(h2, gates, x1, ga2, g_post.reshape(1, d), wg_bf, wu_bf, wd_bf)


def _sgu_weights(ws, bs, chunk_len):
    causal = jnp.tril(jnp.ones((chunk_len, chunk_len), ws.dtype))
    wm = ws[:, :chunk_len, :chunk_len] * causal
    reps = CHUNK // chunk_len
    if reps > 1:
        eye = jnp.eye(reps, dtype=ws.dtype)
        wm = jnp.einsum("ab,gts->gatbs", eye, wm).reshape(N_SGU_GROUPS, CHUNK, CHUNK)
    wcat = jnp.concatenate([wm[0::2], wm[1::2]], axis=2).astype(BF16)
    bt = jnp.tile(bs[:, :chunk_len].T, (reps, 1))
    sbias = jnp.repeat(bt, SGU_GC, axis=1)
    return wcat, sbias


def _split_mod(mod):
    return [m[:, None, :] for m in jnp.split(mod, 6, axis=-1)]


def kernel(x_prompt, x_sample, cache_k, cache_v, page_table, c_prompt, c_sample, w_ada, b_ada, g_pre_mix, g_post_mix, g_pre_ffn, g_post_ffn, w_in, lam_q1, lam_k1, lam_q2, lam_k2, g_subln, sgu_ln_g, sgu_ln_b, sgu_ws, sgu_bs, w_out, w_rg, b_rg, w_re, b_re, w_gate, w_up, w_down):
    depth = w_in.shape[0]
    assert depth == 1
    l = 0
    lam_init = 0.8 - 0.6 * math.exp(-0.3 * l)
    bp, sp, d = x_prompt.shape
    bs_, ts, _ = x_sample.shape
    n_s = bs_ * ts

    mod = _ada(jnp.concatenate([c_prompt, c_sample], axis=0), w_ada[l], b_ada[l])
    sh1p, sc1p, ga1p, sh2p, sc2p, ga2p = _split_mod(mod[:bp])
    rep = lambda m: jnp.repeat(m, ts, axis=0).reshape(1, n_s, d)
    sh1s, sc1s, ga1s, sh2s, sc2s, ga2s = [rep(m) for m in jnp.split(mod[bp:], 6, axis=-1)]

    w_in_bf = w_in[l].astype(BF16)
    wqv_t_bf = jnp.concatenate([w_in_bf[:, 0:ATT_W], w_in_bf[:, 2 * ATT_W:3 * ATT_W]], axis=1).T
    w_out_bf = w_out[l].astype(BF16)
    wg_bf = w_gate[l].astype(BF16)
    wu_bf = w_up[l].astype(BF16)
    wd_bf = w_down[l].astype(BF16)
    wr = jnp.concatenate([w_re[l], w_rg[l]], axis=1)
    wr = jnp.pad(wr, ((0, 0), (0, LANES - wr.shape[1])))
    wr_hi = wr.astype(BF16)
    wr_lo = (wr - wr_hi.astype(F32)).astype(BF16)
    br = jnp.pad(jnp.concatenate([b_re[l], b_rg[l]]), (0, LANES - N_EXPERTS - N_EXPERT_GROUPS)).reshape(1, LANES)
    lam_params = (lam_q1[l], lam_k1[l], lam_q2[l], lam_k2[l])

    tm_p = 512
    xp = x_prompt.reshape(bp * sp, d)
    kp, vtp, qtb, kb, vtb, u, vsb, vs_last = _inproj(
        xp, sc1p, sh1p, g_pre_mix[l], w_in_bf, wqv_t_bf, sgu_ln_g[l], sgu_ln_b[l],
        tm=tm_p, tiles_per_seq=sp // tm_p, last_rows=CHUNK, transposed_qv=True)
    w3 = ATT_W
    a = _attn_prompt(qtb, kb.reshape(bp, sp, w3), vtb, lam_params, g_subln[l], lam_init)
    vp = jnp.transpose(vtp.reshape(bp, N_ATT_HEADS, D_V, sp), (0, 3, 1, 2))
    wcat_p, sbias_p = _sgu_weights(sgu_ws[l], sgu_bs[l], CHUNK)
    x1, h2, gates = _outproj(a.reshape(bp * sp, w3), u, vsb, xp, ga1p, sc2p, sh2p, wcat_p, sbias_p, w_out_bf,
                             g_post_mix[l], g_pre_ffn[l], wr_hi, wr_lo, br, tm=tm_p, tiles_per_mod=sp // tm_p)
    tm_moe = 1024
    yp = _moe(h2, gates, x1, ga2p, g_post_ffn[l], wg_bf, wu_bf, wd_bf, tm=tm_moe, tiles_per_mod=sp // tm_moe)

    xs = x_sample.reshape(n_s, d)
    ks, vs_, qbs, us, vsbs, vs_last_s = _inproj(
        xs, sc1s, sh1s, g_pre_mix[l], w_in_bf, wqv_t_bf, sgu_ln_g[l], sgu_ln_b[l],
        tm=n_s, tiles_per_seq=1, last_rows=n_s, transposed_qv=False)
    a_s = _attn_sample(page_table, qbs.reshape(bs_, ts, w3), ks.reshape(bs_, ts, w3), vs_.reshape(bs_, ts, w3),
                       cache_k[l], cache_v[l], lam_params, g_subln[l], lam_init)
    wcat_s, sbias_s = _sgu_weights(sgu_ws[l], sgu_bs[l], ts)
    x1s, h2s, gates_s = _outproj(a_s.reshape(n_s, w3), us, vsbs, xs, ga1s, sc2s, sh2s, wcat_s, sbias_s, w_out_bf,
                                 g_post_mix[l], g_pre_ffn[l], wr_hi, wr_lo, br, tm=n_s, tiles_per_mod=1)
    ys = _moe(h2s, gates_s, x1s, ga2s, g_post_ffn[l], wg_bf, wu_bf, wd_bf, tm=n_s, tiles_per_mod=1)

    return (yp.reshape(bp, sp, d), ys.reshape(bs_, ts, d),
            kp.reshape(1, bp, sp, N_ATT_HEADS, 2 * D_QK), vp[None],
            vs_last.reshape(1, bp, CHUNK, -1),
            ks.reshape(1, bs_, ts, N_ATT_HEADS, 2 * D_QK), vs_.reshape(1, bs_, ts, N_ATT_HEADS, D_V),
            vs_last_s.reshape(1, bs_, ts, -1))
```

```python
import functools
import math
import struct

import jax
import jax.numpy as jnp
from jax import lax
from jax.experimental import pallas as pl
from jax.experimental.pallas import tpu as pltpu
from jax.experimental.pallas import tpu_sc as plsc

F32 = jnp.float32
BF16 = jnp.bfloat16

EPS = 1e-6
N_ATT_HEADS = 8
D_QK = 32
D_V = 64
ATT_W = N_ATT_HEADS * D_V
N_SGU_GROUPS = 8
SGU_GC = 64
SGU_W = N_SGU_GROUPS * SGU_GC
CHUNK = 128
N_EXPERT_GROUPS = 4
EXPERTS_PER_GROUP = 4
N_EXPERTS = 16
PAGE_SIZE = 128
LANES = 128
VMEM_LIMIT = 56 * 1024 * 1024
NEG_INF = float("-inf")
LOG2E = math.log2(math.e)
PAYLOAD_H2_SEGS = 4
PAYLOAD_SEGS = PAYLOAD_H2_SEGS + 1
INFO_W = 8
SC_WINDOW = 128


def _cparams(*sem):
    return pltpu.CompilerParams(dimension_semantics=sem, vmem_limit_bytes=VMEM_LIMIT)


def _rms(x, g):
    return x * lax.rsqrt(jnp.mean(x * x, axis=-1, keepdims=True) + EPS) * g


def _ada_kernel(c_ref, w_ref, b_ref, o_ref):
    c = c_ref[...]
    s = c * jax.nn.sigmoid(c)
    o_ref[...] = jnp.dot(s.astype(BF16), w_ref[...].astype(BF16), preferred_element_type=F32) + b_ref[...]


def _ada(c, w_ada, b_ada):
    n, d = c.shape
    nout = w_ada.shape[1]
    tn = d
    return pl.pallas_call(
        _ada_kernel,
        grid=(nout // tn,),
        in_specs=[pl.BlockSpec((n, d), lambda j: (0, 0)),
                  pl.BlockSpec((d, tn), lambda j: (0, j)),
                  pl.BlockSpec((1, tn), lambda j: (0, j))],
        out_specs=pl.BlockSpec((n, tn), lambda j: (0, j)),
        out_shape=jax.ShapeDtypeStruct((n, nout), F32),
        compiler_params=_cparams("arbitrary"),
        name="ada",
    )(c, w_ada, b_ada.reshape(1, nout))


Q_SCALE = D_QK ** -0.5 * LOG2E


def _inproj_kernel(x_ref, sc_ref, sh_ref, g_ref, w_ref, wqv_t_ref, lng_ref, lnb_ref, *out_refs,
                   last_rows, transposed_qv):
    x = x_ref[...]
    h = (_rms(x, g_ref[...]) * (1.0 + sc_ref[0]) + sh_ref[0]).astype(BF16)

    def proj(lo):
        return jnp.dot(h, w_ref[:, lo:lo + ATT_W], preferred_element_type=F32)

    def proj_t(lo):
        return lax.dot_general(wqv_t_ref[lo:lo + ATT_W, :], h, (((1,), (1,)), ((), ())),
                               preferred_element_type=F32)

    zk = proj(ATT_W)
    if transposed_qv:
        k_ref, vt_ref, qtb_ref, kb_ref, vtb_ref, u_ref, vsb_ref, vsl_ref = out_refs
        k_ref[...] = zk
        kb_ref[...] = zk.astype(BF16)
        qtb_ref[0] = (proj_t(0) * Q_SCALE).astype(BF16)
        zvt = proj_t(ATT_W)
        vt_ref[0] = zvt
        vtb_ref[0, 0] = zvt.astype(BF16)
    else:
        k_ref, v_ref, qb_ref, u_ref, vsb_ref, vsl_ref = out_refs
        k_ref[...] = zk
        qb_ref[...] = (proj(0) * Q_SCALE).astype(BF16)
        v_ref[...] = proj(2 * ATT_W)
    u_ref[...] = jax.nn.gelu(proj(3 * ATT_W)).astype(BF16)
    gs = jax.nn.gelu(proj(3 * ATT_W + SGU_W))
    mu = jnp.mean(gs, axis=-1, keepdims=True)
    xc = gs - mu
    vs = xc * lax.rsqrt(jnp.mean(xc * xc, axis=-1, keepdims=True) + EPS) * lng_ref[...] + lnb_ref[...]
    vsb_ref[...] = vs.astype(BF16)
    tm = vs.shape[0]
    vsl_ref[0] = vs[tm - last_rows:, :]


def _inproj(x, sc, sh, g, w_in_bf, wqv_t_bf, ln_g, ln_b, *, tm, tiles_per_seq, last_rows, transposed_qv):
    n, d = x.shape
    in_w = w_in_bf.shape[1]
    r = sc.shape[1]
    n_tiles = n // tm
    n_seq = n_tiles // tiles_per_seq
    s_len = tiles_per_seq * tm
    row = lambda i: (i, 0)
    modmap = lambda i: (i // tiles_per_seq, 0, 0)
    const = lambda i: (0, 0)
    col_t = lambda i: (i // tiles_per_seq, 0, i % tiles_per_seq)
    row_spec = pl.BlockSpec((tm, ATT_W), row)
    tail = [(pl.BlockSpec((tm, SGU_W), row), jax.ShapeDtypeStruct((n, SGU_W), BF16)),
            (pl.BlockSpec((tm, SGU_W), row), jax.ShapeDtypeStruct((n, SGU_W), BF16)),
            (pl.BlockSpec((1, last_rows, SGU_W), lambda i: (i // tiles_per_seq, 0, 0)),
             jax.ShapeDtypeStruct((n_seq, last_rows, SGU_W), F32))]
    if transposed_qv:
        head = [(row_spec, jax.ShapeDtypeStruct((n, ATT_W), F32)),
                (pl.BlockSpec((1, ATT_W, tm), col_t), jax.ShapeDtypeStruct((n_seq, ATT_W, s_len), F32)),
                (pl.BlockSpec((1, ATT_W, tm), col_t), jax.ShapeDtypeStruct((n_seq, ATT_W, s_len), BF16)),
                (row_spec, jax.ShapeDtypeStruct((n, ATT_W), BF16)),
                (pl.BlockSpec((1, 1, ATT_W, tm), lambda i: (i // tiles_per_seq, i % tiles_per_seq, 0, 0)),
                 jax.ShapeDtypeStruct((n_seq, tiles_per_seq, ATT_W, tm), BF16))]
    else:
        head = [(row_spec, jax.ShapeDtypeStruct((n, ATT_W), F32)),
                (row_spec, jax.ShapeDtypeStruct((n, ATT_W), F32)),
                (row_spec, jax.ShapeDtypeStruct((n, ATT_W), BF16))]
    specs, shapes = zip(*(head + tail))
    return pl.pallas_call(
        functools.partial(_inproj_kernel, last_rows=last_rows, transposed_qv=transposed_qv),
        grid=(n_tiles,),
        in_specs=[pl.BlockSpec((tm, d), row),
                  pl.BlockSpec((1, r, d), modmap),
                  pl.BlockSpec((1, r, d), modmap),
                  pl.BlockSpec((1, d), const),
                  pl.BlockSpec((d, in_w), const),
                  pl.BlockSpec(wqv_t_bf.shape, const),
                  pl.BlockSpec((1, SGU_W), const),
                  pl.BlockSpec((1, SGU_W), const)],
        out_specs=list(specs),
        out_shape=list(shapes),
        compiler_params=_cparams("arbitrary"),
        name="inproj",
    )(x, sc, sh, g.reshape(1, d), w_in_bf, wqv_t_bf, ln_g.reshape(1, SGU_W), ln_b.reshape(1, SGU_W))


def _diff_lambda(lq1, lk1, lq2, lk2, lam_init):
    return (jnp.exp(jnp.sum(lq1 * lk1, axis=-1, keepdims=True))
            - jnp.exp(jnp.sum(lq2 * lk2, axis=-1, keepdims=True)) + lam_init)


def _pair_slopes(hp):
    s0 = jnp.where(hp == 0, 2.0 ** -1, jnp.where(hp == 1, 2.0 ** -3, jnp.where(hp == 2, 2.0 ** -5, 2.0 ** -7)))
    return s0.astype(F32), (s0 * 0.5).astype(F32)


def _bf16_round(x):
    bits = struct.unpack("<I", struct.pack("<f", x))[0]
    bits = (bits + 0x7FFF + ((bits >> 16) & 1)) & 0xFFFF0000
    return struct.unpack("<f", struct.pack("<I", bits))[0]


L2E_PARTS = []
_rest = LOG2E
for _ in range(3):
    L2E_PARTS.append(_bf16_round(_rest))
    _rest -= L2E_PARTS[-1]
POS_SPLIT = 64
N_POS_FEATS = 2 * len(L2E_PARTS)
EXTRA_ROWS = 16


def _attn_prompt_kernel(qt_ref, k_ref, vt_ref, pf_ref, rel_ref, lq1_ref, lk1_ref, lq2_ref, lk2_ref, g_ref, o_ref,
                        qst_ref, m_ref, acc_ref, *, tq, tk, lam_init):
    hp = pl.program_id(1)
    qi = pl.program_id(2)
    cols = 4 * tq
    qt = qt_ref[0]
    sub = lax.broadcasted_iota(jnp.int32, (LANES, tq), 0)
    for c in range(4):
        qst_ref[0:LANES, c * tq:(c + 1) * tq] = jnp.where((sub >= D_QK * c) & (sub < D_QK * (c + 1)), qt,
                                                          jnp.zeros_like(qt))
    s0, s1 = _pair_slopes(hp)
    col = lax.broadcasted_iota(jnp.int32, (EXTRA_ROWS, cols), 1)
    row = lax.broadcasted_iota(jnp.int32, (EXTRA_ROWS, cols), 0)
    feat = jnp.zeros((EXTRA_ROWS, cols), F32)
    for i, part in enumerate(L2E_PARTS):
        feat = jnp.where(row == i, POS_SPLIT * part, feat)
        feat = jnp.where(row == len(L2E_PARTS) + i, part, feat)
    qst_ref[LANES:LANES + EXTRA_ROWS, :] = (jnp.where(col < 2 * tq, s0, s1) * feat).astype(BF16)
    qst_ref[LANES + EXTRA_ROWS:, :] = jnp.zeros((LANES - EXTRA_ROWS, cols), BF16)
    m_ref[...] = jnp.full((1, cols), NEG_INF, F32)
    acc_ref[...] = jnp.zeros((LANES + EXTRA_ROWS, cols), F32)

    def chunk(kj, masked):
        start = pl.multiple_of(kj * tk, tk)
        kaug = jnp.concatenate([k_ref[0, pl.ds(start, tk), :], pf_ref[pl.ds(start, tk), :]], axis=1)
        st = jnp.dot(kaug, qst_ref[...], preferred_element_type=F32)
        if masked:
            off = (kj * tk - qi * tq).astype(F32)
            st = jnp.where(rel_ref[...] + off <= 0.0, st, NEG_INF)
        m_old = m_ref[...]
        m_new = jnp.maximum(m_old, jnp.max(st, axis=0, keepdims=True))
        pt = jnp.exp2(st - m_new).astype(BF16)
        corr = jnp.exp2(m_old - m_new)
        vaug = jnp.concatenate([vt_ref[0, kj], jnp.ones((EXTRA_ROWS, tk), BF16)], axis=0)
        acc_ref[...] = acc_ref[...] * corr + jnp.dot(vaug, pt, preferred_element_type=F32)
        m_ref[...] = m_new

    n_full = (qi * tq) // tk

    def body(kj, carry):
        chunk(kj, False)
        return carry

    lax.fori_loop(0, n_full, body, 0)
    chunk(n_full, True)

    lam = _diff_lambda(lq1_ref[...], lk1_ref[...], lq2_ref[...], lk2_ref[...], lam_init)
    ot = acc_ref[0:LANES, :] / acc_ref[LANES:LANES + 1, :]
    at = jnp.where(sub < D_V, ot[:, 0:tq] - lam * ot[:, tq:2 * tq], ot[:, 2 * tq:3 * tq] - lam * ot[:, 3 * tq:])
    a2 = at * at
    ss0 = jnp.sum(a2[0:D_V], axis=0, keepdims=True)
    ss1 = jnp.sum(a2[D_V:], axis=0, keepdims=True)
    ms = jnp.where(sub < D_V, ss0, ss1) * (1.0 / D_V)
    an = at * lax.rsqrt(ms + EPS) * g_ref[...] * (1.0 - lam_init)
    o_ref[0] = an.T.astype(BF16)


def _attn_prompt(qtb, kb, vtb, lam_params, g_subln, lam_init, *, tq=256):
    b, w, s = qtb.shape
    n_chunks, tk = vtb.shape[1], vtb.shape[3]
    n_pairs = w // LANES
    cols = 4 * tq
    g2 = jnp.concatenate([g_subln, g_subln]).reshape(LANES, 1)
    pos = jnp.arange(s, dtype=jnp.int32)
    hi, lo = (pos // POS_SPLIT).astype(BF16), (pos % POS_SPLIT).astype(BF16)
    n_parts = len(L2E_PARTS)
    pf = jnp.stack([hi] * n_parts + [lo] * n_parts + [jnp.zeros_like(hi)] * (LANES - N_POS_FEATS), axis=1)
    rel = (jnp.arange(tk, dtype=jnp.int32)[:, None] - (jnp.arange(cols, dtype=jnp.int32) % tq)[None, :]).astype(F32)
    const2 = lambda bi, hp, qi: (0, 0)
    lam_specs = [pl.BlockSpec((1, D_QK), const2)] * 4
    return pl.pallas_call(
        functools.partial(_attn_prompt_kernel, tq=tq, tk=tk, lam_init=lam_init),
        grid=(b, n_pairs, s // tq),
        in_specs=[pl.BlockSpec((1, LANES, tq), lambda bi, hp, qi: (bi, hp, qi)),
                  pl.BlockSpec((1, s, LANES), lambda bi, hp, qi: (bi, 0, hp)),
                  pl.BlockSpec((1, n_chunks, LANES, tk), lambda bi, hp, qi: (bi, 0, hp, 0)),
                  pl.BlockSpec((s, LANES), const2),
                  pl.BlockSpec((tk, cols), const2),
                  *lam_specs,
                  pl.BlockSpec((LANES, 1), const2)],
        out_specs=pl.BlockSpec((1, tq, LANES), lambda bi, hp, qi: (bi, qi, hp)),
        out_shape=jax.ShapeDtypeStruct((b, s, w), BF16),
        scratch_shapes=[pltpu.VMEM((2 * LANES, cols), BF16),
                        pltpu.VMEM((1, cols), F32),
                        pltpu.VMEM((LANES + EXTRA_ROWS, cols), F32)],
        compiler_params=_cparams("arbitrary", "arbitrary", "arbitrary"),
        name="attn_prompt",
    )(qtb, kb, vtb, pf, rel, *[p.reshape(1, D_QK) for p in lam_params], g2)


def _attn_sample_kernel(pt_ref, q_ref, kn_ref, vn_ref, lq1_ref, lk1_ref, lq2_ref, lk2_ref, g_ref, *rest,
                        pps, t_new, past, lam_init):
    k_refs = rest[:pps]
    v_refs = rest[pps:2 * pps]
    o_ref = rest[2 * pps]
    qbd_ref, m_ref, l_ref, acc_ref = rest[2 * pps + 1:]
    j = pl.program_id(1)
    n_steps = pl.num_programs(1)
    w = q_ref.shape[-1]
    n_hj = 2 * N_ATT_HEADS
    rows = n_hj * t_new
    blk = pps * PAGE_SIZE

    row_id = lax.broadcasted_iota(jnp.int32, (rows, 1), 0)
    head = row_id // (2 * t_new)
    slope = jnp.zeros((rows, 1), F32)
    for h in range(N_ATT_HEADS):
        slope = jnp.where(head == h, 2.0 ** -(h + 1), slope)
    slope = slope * LOG2E
    qpos = (past + row_id % t_new).astype(F32)

    @pl.when(j == 0)
    def _():
        q = q_ref[0]
        lane = lax.broadcasted_iota(jnp.int32, (t_new, w), 1)
        for hj in range(n_hj):
            qbd_ref[hj * t_new:(hj + 1) * t_new, :] = jnp.where(
                (lane >= D_QK * hj) & (lane < D_QK * (hj + 1)), q, jnp.zeros_like(q))
        m_ref[...] = jnp.full((rows, 1), NEG_INF, F32)
        l_ref[...] = jnp.zeros((rows, 1), F32)
        acc_ref[...] = jnp.zeros((rows, w), F32)

    def update(s, vmat, v_contract_dim):
        m_old = m_ref[...]
        m_new = jnp.maximum(m_old, jnp.max(s, axis=-1, keepdims=True))
        p = jnp.exp2(s - m_new)
        corr = jnp.exp2(m_old - m_new)
        l_ref[...] = corr * l_ref[...] + jnp.sum(p, axis=-1, keepdims=True)
        pv = lax.dot_general(p.astype(vmat.dtype), vmat, (((1,), (v_contract_dim,)), ((), ())),
                             preferred_element_type=F32)
        acc_ref[...] = corr * acc_ref[...] + pv
        m_ref[...] = m_new

    kt_all = jnp.concatenate([kr[0].astype(BF16) for kr in k_refs], axis=1)
    vt_all = jnp.concatenate([vr[0].astype(BF16) for vr in v_refs], axis=1)
    s = jnp.dot(qbd_ref[...], kt_all, preferred_element_type=F32)
    kpos = (j * blk + lax.broadcasted_iota(jnp.int32, (1, blk), 1)).astype(F32)
    update(s - slope * (qpos - kpos), vt_all, 1)

    @pl.when(j == n_steps - 1)
    def _():
        kn = kn_ref[0]
        sn = lax.dot_general(qbd_ref[...].astype(F32), kn, (((1,), (1,)), ((), ())), preferred_element_type=F32)
        kposn = (past + lax.broadcasted_iota(jnp.int32, (1, t_new), 1)).astype(F32)
        dist = qpos - kposn
        update(jnp.where(dist >= 0, sn - slope * dist, NEG_INF), vn_ref[0], 0)

        lam = _diff_lambda(lq1_ref[...], lk1_ref[...], lq2_ref[...], lk2_ref[...], lam_init)
        o = acc_ref[...] / l_ref[...]
        lane = lax.broadcasted_iota(jnp.int32, (t_new, w), 1)
        a = jnp.zeros((t_new, w), F32)
        for h in range(N_ATT_HEADS):
            o1 = o[(2 * h) * t_new:(2 * h + 1) * t_new]
            o2 = o[(2 * h + 1) * t_new:(2 * h + 2) * t_new]
            a = jnp.where((lane >= D_V * h) & (lane < D_V * (h + 1)), o1 - lam * o2, a)
        a2 = a * a
        ms = jnp.zeros((t_new, w), F32)
        for h in range(N_ATT_HEADS):
            in_h = (lane >= D_V * h) & (lane < D_V * (h + 1))
            ssh = jnp.sum(jnp.where(in_h, a2, 0.0), axis=-1, keepdims=True)
            ms = jnp.where(in_h, ssh, ms)
        o_ref[0] = (a * lax.rsqrt(ms * (1.0 / D_V) + EPS) * g_ref[...] * (1.0 - lam_init)).astype(BF16)


def _attn_sample(page_table, qb, kb, vb, cache_k, cache_v, lam_params, g_subln, lam_init, *, pps=8):
    bd, t_new, w = qb.shape
    n_pages = page_table.shape[1]
    n_pool = cache_k.shape[0]
    past = n_pages * PAGE_SIZE
    ck = jnp.transpose(cache_k, (0, 2, 3, 1)).reshape(n_pool, w, PAGE_SIZE)
    cv = jnp.transpose(cache_v, (0, 2, 3, 1)).reshape(n_pool, w, PAGE_SIZE)
    g8 = jnp.tile(g_subln, N_ATT_HEADS).reshape(1, w)
    new_spec = pl.BlockSpec((1, t_new, w), lambda b, j, pt: (b, 0, 0))
    small = lambda shape: pl.BlockSpec(shape, lambda b, j, pt: (0, 0))

    def page_spec(i):
        return pl.BlockSpec((1, w, PAGE_SIZE), lambda b, j, pt: (pt[b, j * pps + i], 0, 0))

    grid_spec = pltpu.PrefetchScalarGridSpec(
        num_scalar_prefetch=1,
        grid=(bd, n_pages // pps),
        in_specs=[new_spec, new_spec, new_spec,
                  small((1, D_QK)), small((1, D_QK)), small((1, D_QK)), small((1, D_QK)), small((1, w)),
                  *[page_spec(i) for i in range(pps)],
                  *[page_spec(i) for i in range(pps)]],
        out_specs=pl.BlockSpec((1, t_new, w), lambda b, j, pt: (b, 0, 0)),
        scratch_shapes=[pltpu.VMEM((2 * N_ATT_HEADS * t_new, w), BF16),
                        pltpu.VMEM((2 * N_ATT_HEADS * t_new, 1), F32),
                        pltpu.VMEM((2 * N_ATT_HEADS * t_new, 1), F32),
                        pltpu.VMEM((2 * N_ATT_HEADS * t_new, w), F32)],
    )
    return pl.pallas_call(
        functools.partial(_attn_sample_kernel, pps=pps, t_new=t_new, past=past, lam_init=lam_init),
        grid_spec=grid_spec,
        out_shape=jax.ShapeDtypeStruct((bd, t_new, w), BF16),
        compiler_params=_cparams("arbitrary", "arbitrary"),
        name="attn_sample",
    )(page_table, qb, kb, vb, *[p.reshape(1, D_QK) for p in lam_params], g8,
      *([ck] * pps), *([cv] * pps))


def _outproj_kernel(a_ref, u_ref, vs_ref, x_ref, ga1_ref, sc2_ref, sh2_ref, wcat_ref, sbias_ref, wout_ref,
                    gpost_ref, gpre_ref, wr_hi_ref, wr_lo_ref, br_ref, tril_ref, *rest, group_dispatch):
    if group_dispatch:
        x1_ref, pay_ref, info_ref, counts_ref, sg_ref, run_ref = rest
    else:
        x1_ref, h2_ref, gates_ref, sg_ref = rest
    tm = x_ref.shape[0]
    sgu_w = u_ref.shape[1]
    lane = lax.broadcasted_iota(jnp.int32, (CHUNK, LANES), 1)
    for ci in range(tm // CHUNK):
        rs = slice(ci * CHUNK, (ci + 1) * CHUNK)
        for gp in range(sgu_w // LANES):
            cs = slice(gp * LANES, (gp + 1) * LANES)
            vp = vs_ref[rs, cs]
            zero = jnp.zeros_like(vp)
            rhs = jnp.concatenate([jnp.where(lane < SGU_GC, vp, zero), jnp.where(lane >= SGU_GC, vp, zero)], axis=0)
            mixed = jnp.dot(wcat_ref[gp], rhs, preferred_element_type=F32) + sbias_ref[:, cs]
            sg_ref[rs, cs] = (u_ref[rs, cs].astype(F32) * mixed).astype(BF16)
    att_w = a_ref.shape[1]
    mix = (jnp.dot(a_ref[...], wout_ref[0:att_w, :], preferred_element_type=F32)
           + jnp.dot(sg_ref[...], wout_ref[att_w:, :], preferred_element_type=F32))
    x1 = x_ref[...] + ga1_ref[0] * _rms(mix, gpost_ref[...])
    x1_ref[...] = x1
    h2 = _rms(x1, gpre_ref[...]) * (1.0 + sc2_ref[0]) + sh2_ref[0]
    h2_hi = h2.astype(BF16)
    h2_lo = (h2 - h2_hi.astype(F32)).astype(BF16)
    logits = (jnp.dot(h2_hi, wr_hi_ref[...], preferred_element_type=F32)
              + jnp.dot(h2_lo, wr_hi_ref[...], preferred_element_type=F32)
              + jnp.dot(h2_hi, wr_lo_ref[...], preferred_element_type=F32)) + br_ref[...]
    ln = lax.broadcasted_iota(jnp.int32, (tm, LANES), 1)
    big = jnp.int32(LANES)
    is_g = (ln >= N_EXPERTS) & (ln < N_EXPERTS + N_EXPERT_GROUPS)
    gl = jnp.where(is_g, logits, NEG_INF)
    gmax = jnp.max(gl, axis=-1, keepdims=True)
    g_idx = jnp.min(jnp.where(gl == gmax, ln, big), axis=-1, keepdims=True) - N_EXPERTS
    g_w = 1.0 / jnp.sum(jnp.exp(gl - gmax), axis=-1, keepdims=True)
    in_grp = (ln >= g_idx * EXPERTS_PER_GROUP) & (ln < (g_idx + 1) * EXPERTS_PER_GROUP)
    el = jnp.where(in_grp, logits, NEG_INF)
    emax = jnp.max(el, axis=-1, keepdims=True)
    ez = jnp.exp(el - emax)
    prob = ez / jnp.sum(ez, axis=-1, keepdims=True)
    p1 = jnp.max(prob, axis=-1, keepdims=True)
    i1 = jnp.min(jnp.where((prob == p1) & in_grp, ln, big), axis=-1, keepdims=True)
    rest = jnp.where(in_grp & (ln != i1), prob, -1.0)
    p2 = jnp.max(rest, axis=-1, keepdims=True)
    i2 = jnp.min(jnp.where(rest == p2, ln, big), axis=-1, keepdims=True)
    den = p1 + p2
    gates = jnp.where(ln == i1, p1 / den * g_w, 0.0) + jnp.where(ln == i2, p2 / den * g_w, 0.0)
    if not group_dispatch:
        h2_ref[...] = h2_hi
        gates_ref[...] = gates[:, :N_EXPERTS]
        return

    bits = pltpu.bitcast(h2_hi.astype(F32), jnp.uint32)
    half = PAYLOAD_H2_SEGS * LANES
    for s in range(PAYLOAD_H2_SEGS):
        lo = bits[:, s * LANES:(s + 1) * LANES]
        hi = bits[:, half + s * LANES:half + (s + 1) * LANES]
        pay_ref[s] = hi | lax.shift_right_logical(lo, jnp.uint32(16))
    pay_ref[PAYLOAD_H2_SEGS] = pltpu.bitcast(gates, jnp.uint32)

    @pl.when(pl.program_id(0) == 0)
    def _():
        run_ref[...] = jnp.zeros_like(run_ref)

    onehot = jnp.where(ln == g_idx, 1.0, 0.0).astype(BF16)
    csum = jnp.dot(tril_ref[...], onehot, preferred_element_type=F32) + run_ref[...]
    rank = jnp.sum(jnp.where(ln == g_idx, csum, 0.0), axis=-1, keepdims=True) - 1.0
    run_ref[...] = csum[tm - 1:tm, :]
    counts_ref[...] = csum[tm - 1:tm, :]
    info = jnp.where(ln == 0, g_idx, jnp.where(ln == 1, rank.astype(jnp.int32), 0))
    info_ref[...] = info[:, :INFO_W]


def _outproj(a, u, vsb, x, ga1, sc2, sh2, wcat, sbias, w_out_bf, g_post, g_pre, wr_hi, wr_lo, br,
             *, tm, tiles_per_mod, group_dispatch):
    n, d = x.shape
    att_w = a.shape[1]
    sgu_w = u.shape[1]
    r = ga1.shape[1]
    row = lambda i: (i, 0)
    modmap = lambda i: (i // tiles_per_mod, 0, 0)
    const = lambda i: (0, 0)
    tril = jnp.tril(jnp.ones((tm, tm), BF16))
    if group_dispatch:
        out_specs = [pl.BlockSpec((tm, d), row),
                     pl.BlockSpec((PAYLOAD_SEGS, tm, LANES), lambda i: (0, i, 0)),
                     pl.BlockSpec((tm, INFO_W), row),
                     pl.BlockSpec((1, LANES), const)]
        out_shape = [jax.ShapeDtypeStruct((n, d), F32),
                     jax.ShapeDtypeStruct((PAYLOAD_SEGS, n, LANES), jnp.uint32),
                     jax.ShapeDtypeStruct((n, INFO_W), jnp.int32),
                     jax.ShapeDtypeStruct((1, LANES), F32)]
        scratch = [pltpu.VMEM((tm, sgu_w), BF16), pltpu.VMEM((1, LANES), F32)]
    else:
        out_specs = [pl.BlockSpec((tm, d), row),
                     pl.BlockSpec((tm, d), row),
                     pl.BlockSpec((tm, N_EXPERTS), row)]
        out_shape = [jax.ShapeDtypeStruct((n, d), F32),
                     jax.ShapeDtypeStruct((n, d), BF16),
                     jax.ShapeDtypeStruct((n, N_EXPERTS), F32)]
        scratch = [pltpu.VMEM((tm, sgu_w), BF16)]
    return pl.pallas_call(
        functools.partial(_outproj_kernel, group_dispatch=group_dispatch),
        grid=(n // tm,),
        in_specs=[pl.BlockSpec((tm, att_w), row),
                  pl.BlockSpec((tm, sgu_w), row),
                  pl.BlockSpec((tm, sgu_w), row),
                  pl.BlockSpec((tm, d), row),
                  pl.BlockSpec((1, r, d), modmap),
                  pl.BlockSpec((1, r, d), modmap),
                  pl.BlockSpec((1, r, d), modmap),
                  pl.BlockSpec(wcat.shape, lambda i: (0, 0, 0)),
                  pl.BlockSpec(sbias.shape, const),
                  pl.BlockSpec(w_out_bf.shape, const),
                  pl.BlockSpec((1, d), const),
                  pl.BlockSpec((1, d), const),
                  pl.BlockSpec(wr_hi.shape, const),
                  pl.BlockSpec(wr_lo.shape, const),
                  pl.BlockSpec((1, LANES), const),
                  pl.BlockSpec((tm, tm), const)],
        out_specs=out_specs,
        out_shape=out_shape,
        scratch_shapes=scratch,
        compiler_params=_cparams("arbitrary"),
        name="outproj",
    )(a, u, vsb, x, ga1, sc2, sh2, wcat, sbias, w_out_bf, g_post.reshape(1, d), g_pre.reshape(1, d),
      wr_hi, wr_lo, br, tril)


def _moe_kernel(h_ref, gates_ref, x1_ref, ga2_ref, gpost_ref, wg_ref, wu_ref, wd_ref, o_ref, acc_ref):
    e = pl.program_id(1)

    @pl.when(e == 0)
    def _():
        acc_ref[...] = jnp.zeros_like(acc_ref)

    h = h_ref[...]
    gate = jnp.dot(h, wg_ref[0], preferred_element_type=F32)
    up = jnp.dot(h, wu_ref[0], preferred_element_type=F32)
    hdn = (gate * jax.nn.sigmoid(gate) * up).astype(BF16)
    y = jnp.dot(hdn, wd_ref[0], preferred_element_type=F32)
    ln = lax.broadcasted_iota(jnp.int32, gates_ref.shape, 1)
    ge = jnp.sum(jnp.where(ln == e, gates_ref[...], 0.0), axis=-1, keepdims=True)
    acc_ref[...] += ge * y

    @pl.when(e == pl.num_programs(1) - 1)
    def _():
        o_ref[...] = x1_ref[...] + ga2_ref[0] * _rms(acc_ref[...], gpost_ref[...])


def _moe(h2, gates, x1, ga2, g_post, wg_bf, wu_bf, wd_bf, *, tm, tiles_per_mod):
    n, d = x1.shape
    n_e, _, de = wg_bf.shape
    r = ga2.shape[1]
    row = lambda i, e: (i, 0)
    return pl.pallas_call(
        _moe_kernel,
        grid=(n // tm, n_e),
        in_specs=[pl.BlockSpec((tm, d), row),
                  pl.BlockSpec((tm, n_e), row),
                  pl.BlockSpec((tm, d), row),
                  pl.BlockSpec((1, r, d), lambda i, e: (i // tiles_per_mod, 0, 0)),
                  pl.BlockSpec((1, d), lambda i, e: (0, 0)),
                  pl.BlockSpec((1, d, de), lambda i, e: (e, 0, 0)),
                  pl.BlockSpec((1, d, de), lambda i, e: (e, 0, 0)),
                  pl.BlockSpec((1, de, d), lambda i, e: (e, 0, 0))],
        out_specs=pl.BlockSpec((tm, d), row),
        out_shape=jax.ShapeDtypeStruct((n, d), F32),
        scratch_shapes=[pltpu.VMEM((tm, d), F32)],
        compiler_params=_cparams("arbitrary", "arbitrary"),
        name="moe",
    )(h2, gates, x1, ga2, g_post.reshape(1, d), wg_bf, wu_bf, wd_bf)


def _sc_mesh():
    return plsc.VectorSubcoreMesh(core_axis_name="core", subcore_axis_name="subcore")


def _sc_scatter_rows(x, idx, n_out_rows):
    n_rows, width = x.shape

    @pl.kernel(out_type=jax.ShapeDtypeStruct((n_out_rows, width), x.dtype), mesh=_sc_mesh(), scratch_types=[])
    def scatter(x_hbm, i_hbm, o_hbm):
        def body(x_vmem, i_vmem):
            pltpu.sync_copy(x_vmem, o_hbm.at[i_vmem.at[0]])

        pltpu.emit_pipeline(
            body,
            grid=(n_rows // SC_WINDOW,),
            in_specs=[pl.BlockSpec((SC_WINDOW, width), lambda i: (i, 0)),
                      pl.BlockSpec((1, SC_WINDOW), lambda i: (0, i))],
            out_specs=[],
            core_axis_name=("core", "subcore"),
            dimension_semantics=(pltpu.PARALLEL,),
        )(x_hbm, i_hbm)

    return scatter(x, idx.reshape(1, n_rows))


def _sc_gather_rows(x, idx):
    n_rows = idx.shape[0]
    width = x.shape[1]

    @pl.kernel(out_type=jax.ShapeDtypeStruct((n_rows, width), x.dtype), mesh=_sc_mesh(), scratch_types=[])
    def gather(x_hbm, i_hbm, o_hbm):
        def body(i_vmem, o_vmem):
            pltpu.sync_copy(x_hbm.at[i_vmem.at[0]], o_vmem)

        pltpu.emit_pipeline(
            body,
            grid=(n_rows // SC_WINDOW,),
            in_specs=[pl.BlockSpec((1, SC_WINDOW), lambda i: (0, i))],
            out_specs=[pl.BlockSpec((SC_WINDOW, width), lambda i: (i, 0))],
            core_axis_name=("core", "subcore"),
            dimension_semantics=(pltpu.PARALLEL,),
        )(i_hbm, o_hbm)

    return gather(x, idx.reshape(1, n_rows))


def _group_moe_kernel(tg_ref, nused_ref, xs_ref, gpost_ref, wg_ref, wu_ref, wd_ref, rs_ref):
    t = pl.program_id(0)
    tm = xs_ref.shape[1]

    @pl.when(t >= nused_ref[0])
    def _():
        rs_ref[...] = jnp.zeros_like(rs_ref)

    @pl.when(t < nused_ref[0])
    def _():
        words = [xs_ref[s] for s in range(PAYLOAD_H2_SEGS)]
        lo = [pltpu.bitcast(lax.shift_left(w, jnp.uint32(16)), F32) for w in words]
        hi = [pltpu.bitcast(w & jnp.uint32(0xFFFF0000), F32) for w in words]
        x = jnp.concatenate(lo + hi, axis=1).astype(BF16)
        gates = pltpu.bitcast(xs_ref[PAYLOAD_H2_SEGS], F32)
        ln = lax.broadcasted_iota(jnp.int32, (tm, LANES), 1)
        first = tg_ref[t] * EXPERTS_PER_GROUP
        y = jnp.zeros((tm, x.shape[1]), F32)
        for el in range(EXPERTS_PER_GROUP):
            ge = jnp.sum(jnp.where(ln == first + el, gates, 0.0), axis=-1, keepdims=True)
            gate = jnp.dot(x, wg_ref[el], preferred_element_type=F32)
            up = jnp.dot(x, wu_ref[el], preferred_element_type=F32)
            hdn = (gate * jax.nn.sigmoid(gate) * up).astype(BF16)
            y = y + ge * jnp.dot(hdn, wd_ref[el], preferred_element_type=F32)
        r = _rms(y, gpost_ref[...])
        for s in range(rs_ref.shape[0]):
            rs_ref[s] = r[:, s * LANES:(s + 1) * LANES]


def _group_moe(xs, tile_group, n_used, g_post, wg_bf, wu_bf, wd_bf, *, tm):
    _, p_rows, _ = xs.shape
    n_e, d, de = wg_bf.shape
    g = EXPERTS_PER_GROUP
    grid_spec = pltpu.PrefetchScalarGridSpec(
        num_scalar_prefetch=2,
        grid=(p_rows // tm,),
        in_specs=[pl.BlockSpec((PAYLOAD_SEGS, tm, LANES), lambda t, tg, nu: (0, t, 0)),
                  pl.BlockSpec((1, d), lambda t, tg, nu: (0, 0)),
                  pl.BlockSpec((g, d, de), lambda t, tg, nu: (tg[t], 0, 0)),
                  pl.BlockSpec((g, d, de), lambda t, tg, nu: (tg[t], 0, 0)),
                  pl.BlockSpec((g, de, d), lambda t, tg, nu: (tg[t], 0, 0))],
        out_specs=pl.BlockSpec((d // LANES, tm, LANES), lambda t, tg, nu: (0, t, 0)),
    )
    return pl.pallas_call(
        _group_moe_kernel,
        grid_spec=grid_spec,
        out_shape=jax.ShapeDtypeStruct((d // LANES, p_rows, LANES), F32),
        compiler_params=_cparams("arbitrary"),
        name="group_moe",
    )(tile_group, n_used, xs, g_post.reshape(1, d), wg_bf, wu_bf, wd_bf)


def _residual_kernel(x1_ref, ga2_ref, r_ref, o_ref):
    r = jnp.concatenate([r_ref[s] for s in range(r_ref.shape[0])], axis=1)
    o_ref[...] = x1_ref[...] + ga2_ref[0] * r


def _residual(x1, ga2, r_segs, *, tm, tiles_per_mod):
    n, d = x1.shape
    return pl.pallas_call(
        _residual_kernel,
        grid=(n // tm,),
        in_specs=[pl.BlockSpec((tm, d), lambda i: (i, 0)),
                  pl.BlockSpec((1, ga2.shape[1], d), lambda i: (i // tiles_per_mod, 0, 0)),
                  pl.BlockSpec((d // LANES, tm, LANES), lambda i: (0, i, 0))],
        out_specs=pl.BlockSpec((tm, d), lambda i: (i, 0)),
        out_shape=jax.ShapeDtypeStruct((n, d), F32),
        compiler_params=_cparams("arbitrary"),
        name="residual",
    )(x1, ga2, r_segs)


def _group_plan(info, counts, n_tokens, tm):
    n_tiles = n_tokens // tm + N_EXPERT_GROUPS
    cnt = counts[0, :N_EXPERT_GROUPS].astype(jnp.int32)
    padded = (cnt + tm - 1) // tm * tm
    ends = jnp.cumsum(padded)
    starts = ends - padded
    gid, rank = info[:, 0], info[:, 1]
    onehot = (gid[:, None] == jnp.arange(N_EXPERT_GROUPS, dtype=jnp.int32)[None, :]).astype(jnp.int32)
    pos = jnp.sum(onehot * starts[None, :], axis=1) + rank
    tile_ends = ends // tm
    t = jnp.arange(n_tiles, dtype=jnp.int32)
    tile_group = jnp.minimum(jnp.sum((t[:, None] >= tile_ends[None, :]).astype(jnp.int32), axis=1),
                             N_EXPERT_GROUPS - 1)
    return pos, tile_group, tile_ends[-1:], n_tiles * tm


def _sgu_weights(ws, bs, chunk_len):
    causal = jnp.tril(jnp.ones((chunk_len, chunk_len), ws.dtype))
    wm = ws[:, :chunk_len, :chunk_len] * causal
    reps = CHUNK // chunk_len
    if reps > 1:
        eye = jnp.eye(reps, dtype=ws.dtype)
        wm = jnp.einsum("ab,gts->gatbs", eye, wm).reshape(N_SGU_GROUPS, CHUNK, CHUNK)
    wcat = jnp.concatenate([wm[0::2], wm[1::2]], axis=2).astype(BF16)
    bt = jnp.tile(bs[:, :chunk_len].T, (reps, 1))
    sbias = jnp.repeat(bt, SGU_GC, axis=1)
    return wcat, sbias


def _split_mod(mod):
    return [m[:, None, :] for m in jnp.split(mod, 6, axis=-1)]


def kernel(x_prompt, x_sample, cache_k, cache_v, page_table, c_prompt, c_sample, w_ada, b_ada, g_pre_mix, g_post_mix, g_pre_ffn, g_post_ffn, w_in, lam_q1, lam_k1, lam_q2, lam_k2, g_subln, sgu_ln_g, sgu_ln_b, sgu_ws, sgu_bs, w_out, w_rg, b_rg, w_re, b_re, w_gate, w_up, w_down):
    depth = w_in.shape[0]
    assert depth == 1
    l = 0
    lam_init = 0.8 - 0.6 * math.exp(-0.3 * l)
    bp, sp, d = x_prompt.shape
    bs_, ts, _ = x_sample.shape
    n_s = bs_ * ts

    mod = _ada(jnp.concatenate([c_prompt, c_sample], axis=0), w_ada[l], b_ada[l])
    sh1p, sc1p, ga1p, sh2p, sc2p, ga2p = _split_mod(mod[:bp])
    rep = lambda m: jnp.repeat(m, ts, axis=0).reshape(1, n_s, d)
    sh1s, sc1s, ga1s, sh2s, sc2s, ga2s = [rep(m) for m in jnp.split(mod[bp:], 6, axis=-1)]

    w_in_bf = w_in[l].astype(BF16)
    wqv_t_bf = jnp.concatenate([w_in_bf[:, 0:ATT_W], w_in_bf[:, 2 * ATT_W:3 * ATT_W]], axis=1).T
    w_out_bf = w_out[l].astype(BF16)
    wg_bf = w_gate[l].astype(BF16)
    wu_bf = w_up[l].astype(BF16)
    wd_bf = w_down[l].astype(BF16)
    wr = jnp.concatenate([w_re[l], w_rg[l]], axis=1)
    wr = jnp.pad(wr, ((0, 0), (0, LANES - wr.shape[1])))
    wr_hi = wr.astype(BF16)
    wr_lo = (wr - wr_hi.astype(F32)).astype(BF16)
    br = jnp.pad(jnp.concatenate([b_re[l], b_rg[l]]), (0, LANES - N_EXPERTS - N_EXPERT_GROUPS)).reshape(1, LANES)
    lam_params = (lam_q1[l], lam_k1[l], lam_q2[l], lam_k2[l])

    tm_p = 512
    xp = x_prompt.reshape(bp * sp, d)
    kp, vtp, qtb, kb, vtb, u, vsb, vs_last = _inproj(
        xp, sc1p, sh1p, g_pre_mix[l], w_in_bf, wqv_t_bf, sgu_ln_g[l], sgu_ln_b[l],
        tm=tm_p, tiles_per_seq=sp // tm_p, last_rows=CHUNK, transposed_qv=True)
    w3 = ATT_W
    a = _attn_prompt(qtb, kb.reshape(bp, sp, w3), vtb, lam_params, g_subln[l], lam_init)
    vp = jnp.transpose(vtp.reshape(bp, N_ATT_HEADS, D_V, sp), (0, 3, 1, 2))
    wcat_p, sbias_p = _sgu_weights(sgu_ws[l], sgu_bs[l], CHUNK)
    n_p = bp * sp
    x1, payload, info, counts = _outproj(
        a.reshape(n_p, w3), u, vsb, xp, ga1p, sc2p, sh2p, wcat_p, sbias_p, w_out_bf,
        g_post_mix[l], g_pre_ffn[l], wr_hi, wr_lo, br, tm=tm_p, tiles_per_mod=sp // tm_p, group_dispatch=True)
    tm_moe = 512
    pos, tile_group, n_used, p_rows = _group_plan(info, counts, n_p, tm_moe)
    seg_p = lambda k: (jnp.arange(k, dtype=jnp.int32)[:, None] * p_rows + pos[None, :]).reshape(-1)
    xs = _sc_scatter_rows(payload.reshape(PAYLOAD_SEGS * n_p, LANES), seg_p(PAYLOAD_SEGS), PAYLOAD_SEGS * p_rows)
    rs = _group_moe(xs.reshape(PAYLOAD_SEGS, p_rows, LANES), tile_group, n_used, g_post_ffn[l],
                    wg_bf, wu_bf, wd_bf, tm=tm_moe)
    n_seg = d // LANES
    r_tok = _sc_gather_rows(rs.reshape(n_seg * p_rows, LANES), seg_p(n_seg))
    yp = _residual(x1, ga2p, r_tok.reshape(n_seg, n_p, LANES), tm=tm_p, tiles_per_mod=sp // tm_p)

    xs = x_sample.reshape(n_s, d)
    ks, vs_, qbs, us, vsbs, vs_last_s = _inproj(
        xs, sc1s, sh1s, g_pre_mix[l], w_in_bf, wqv_t_bf, sgu_ln_g[l], sgu_ln_b[l],
        tm=n_s, tiles_per_seq=1, last_rows=n_s, transposed_qv=False)
    a_s = _attn_sample(page_table, qbs.reshape(bs_, ts, w3), ks.reshape(bs_, ts, w3), vs_.reshape(bs_, ts, w3),
                       cache_k[l], cache_v[l], lam_params, g_subln[l], lam_init)
    wcat_s, sbias_s = _sgu_weights(sgu_ws[l], sgu_bs[l], ts)
    x1s, h2s, gates_s = _outproj(a_s.reshape(n_s, w3), us, vsbs, xs, ga1s, sc2s, sh2s, wcat_s, sbias_s, w_out_bf,
                                 g_post_mix[l], g_pre_ffn[l], wr_hi, wr_lo, br, tm=n_s, tiles_per_mod=1,
                                 group_dispatch=False)
    ys = _moe(h2s, gates_s, x1s, ga2s, g_post_ffn[l], wg_bf, wu_bf, wd_bf, tm=n_s, tiles_per_mod=1)

    return (yp.reshape(bp, sp, d), ys.reshape(bs_, ts, d),
            kp.reshape(1, bp, sp, N_ATT_HEADS, 2 * D_QK), vp[None],
            vs_last.reshape(1, bp, CHUNK, -1),
            ks.reshape(1, bs_, ts, N_ATT_HEADS, 2 * D_QK), vs_.reshape(1, bs_, ts, N_ATT_HEADS, D_V),
            vs_last_s.reshape(1, bs_, ts, -1))
```

```python
import functools
import math
import struct

import jax
import jax.numpy as jnp
from jax import lax
from jax.experimental import pallas as pl
from jax.experimental.pallas import tpu as pltpu
from jax.experimental.pallas import tpu_sc as plsc

F32 = jnp.float32
BF16 = jnp.bfloat16

EPS = 1e-6
N_ATT_HEADS = 8
D_QK = 32
D_V = 64
ATT_W = N_ATT_HEADS * D_V
N_SGU_GROUPS = 8
SGU_GC = 64
SGU_W = N_SGU_GROUPS * SGU_GC
CHUNK = 128
N_EXPERT_GROUPS = 4
EXPERTS_PER_GROUP = 4
N_EXPERTS = 16
PAGE_SIZE = 128
LANES = 128
VMEM_LIMIT = 56 * 1024 * 1024
NEG_INF = float("-inf")
LOG2E = math.log2(math.e)
PAYLOAD_H2_SEGS = 4
PAYLOAD_SEGS = PAYLOAD_H2_SEGS + 1
INFO_W = 8
SC_WINDOW = 128


def _cparams(*sem):
    return pltpu.CompilerParams(dimension_semantics=sem, vmem_limit_bytes=VMEM_LIMIT)


def _rms(x, g):
    return x * lax.rsqrt(jnp.mean(x * x, axis=-1, keepdims=True) + EPS) * g


def _ada_kernel(c_ref, w_ref, b_ref, o_ref):
    c = c_ref[...]
    s = c * jax.nn.sigmoid(c)
    o_ref[...] = jnp.dot(s.astype(BF16), w_ref[...].astype(BF16), preferred_element_type=F32) + b_ref[...]


def _ada(c, w_ada, b_ada):
    n, d = c.shape
    nout = w_ada.shape[1]
    tn = d
    return pl.pallas_call(
        _ada_kernel,
        grid=(nout // tn,),
        in_specs=[pl.BlockSpec((n, d), lambda j: (0, 0)),
                  pl.BlockSpec((d, tn), lambda j: (0, j)),
                  pl.BlockSpec((1, tn), lambda j: (0, j))],
        out_specs=pl.BlockSpec((n, tn), lambda j: (0, j)),
        out_shape=jax.ShapeDtypeStruct((n, nout), F32),
        compiler_params=_cparams("arbitrary"),
        name="ada",
    )(c, w_ada, b_ada.reshape(1, nout))


Q_SCALE = D_QK ** -0.5 * LOG2E


def _inproj_kernel(x_ref, sc_ref, sh_ref, g_ref, w_ref, wqv_t_ref, lng_ref, lnb_ref, *out_refs,
                   last_rows, transposed_qv):
    x = x_ref[...]
    h = (_rms(x, g_ref[...]) * (1.0 + sc_ref[0]) + sh_ref[0]).astype(BF16)

    def proj(lo):
        return jnp.dot(h, w_ref[:, lo:lo + ATT_W], preferred_element_type=F32)

    def proj_t(lo):
        return lax.dot_general(wqv_t_ref[lo:lo + ATT_W, :], h, (((1,), (1,)), ((), ())),
                               preferred_element_type=F32)

    zk = proj(ATT_W)
    if transposed_qv:
        kt_ref, vt_ref, qtb_ref, kb_ref, vtb_ref, u_ref, vsb_ref, vsl_ref = out_refs
        kt_ref[0] = zk.T
        kb_ref[...] = zk.astype(BF16)
        qtb_ref[0] = (proj_t(0) * Q_SCALE).astype(BF16)
        zvt = proj_t(ATT_W)
        vt_ref[0] = zvt
        vtb_ref[0, 0] = zvt.astype(BF16)
    else:
        k_ref, v_ref, qb_ref, u_ref, vsb_ref, vsl_ref = out_refs
        k_ref[...] = zk
        qb_ref[...] = (proj(0) * Q_SCALE).astype(BF16)
        v_ref[...] = proj(2 * ATT_W)
    u_ref[...] = jax.nn.gelu(proj(3 * ATT_W)).astype(BF16)
    gs = jax.nn.gelu(proj(3 * ATT_W + SGU_W))
    mu = jnp.mean(gs, axis=-1, keepdims=True)
    xc = gs - mu
    vs = xc * lax.rsqrt(jnp.mean(xc * xc, axis=-1, keepdims=True) + EPS) * lng_ref[...] + lnb_ref[...]
    vsb_ref[...] = vs.astype(BF16)
    tm = vs.shape[0]
    vsl_ref[0] = vs[tm - last_rows:, :]


def _inproj(x, sc, sh, g, w_in_bf, wqv_t_bf, ln_g, ln_b, *, tm, tiles_per_seq, last_rows, transposed_qv):
    n, d = x.shape
    in_w = w_in_bf.shape[1]
    r = sc.shape[1]
    n_tiles = n // tm
    n_seq = n_tiles // tiles_per_seq
    s_len = tiles_per_seq * tm
    row = lambda i: (i, 0)
    modmap = lambda i: (i // tiles_per_seq, 0, 0)
    const = lambda i: (0, 0)
    col_t = lambda i: (i // tiles_per_seq, 0, i % tiles_per_seq)
    row_spec = pl.BlockSpec((tm, ATT_W), row)
    tail = [(pl.BlockSpec((tm, SGU_W), row), jax.ShapeDtypeStruct((n, SGU_W), BF16)),
            (pl.BlockSpec((tm, SGU_W), row), jax.ShapeDtypeStruct((n, SGU_W), BF16)),
            (pl.BlockSpec((1, last_rows, SGU_W), lambda i: (i // tiles_per_seq, 0, 0)),
             jax.ShapeDtypeStruct((n_seq, last_rows, SGU_W), F32))]
    if transposed_qv:
        head = [(pl.BlockSpec((1, ATT_W, tm), col_t), jax.ShapeDtypeStruct((n_seq, ATT_W, s_len), F32)),
                (pl.BlockSpec((1, ATT_W, tm), col_t), jax.ShapeDtypeStruct((n_seq, ATT_W, s_len), F32)),
                (pl.BlockSpec((1, ATT_W, tm), col_t), jax.ShapeDtypeStruct((n_seq, ATT_W, s_len), BF16)),
                (row_spec, jax.ShapeDtypeStruct((n, ATT_W), BF16)),
                (pl.BlockSpec((1, 1, ATT_W, tm), lambda i: (i // tiles_per_seq, i % tiles_per_seq, 0, 0)),
                 jax.ShapeDtypeStruct((n_seq, tiles_per_seq, ATT_W, tm), BF16))]
    else:
        head = [(row_spec, jax.ShapeDtypeStruct((n, ATT_W), F32)),
                (row_spec, jax.ShapeDtypeStruct((n, ATT_W), F32)),
                (row_spec, jax.ShapeDtypeStruct((n, ATT_W), BF16))]
    specs, shapes = zip(*(head + tail))
    return pl.pallas_call(
        functools.partial(_inproj_kernel, last_rows=last_rows, transposed_qv=transposed_qv),
        grid=(n_tiles,),
        in_specs=[pl.BlockSpec((tm, d), row),
                  pl.BlockSpec((1, r, d), modmap),
                  pl.BlockSpec((1, r, d), modmap),
                  pl.BlockSpec((1, d), const),
                  pl.BlockSpec((d, in_w), const),
                  pl.BlockSpec(wqv_t_bf.shape, const),
                  pl.BlockSpec((1, SGU_W), const),
                  pl.BlockSpec((1, SGU_W), const)],
        out_specs=list(specs),
        out_shape=list(shapes),
        compiler_params=_cparams("arbitrary"),
        name="inproj",
    )(x, sc, sh, g.reshape(1, d), w_in_bf, wqv_t_bf, ln_g.reshape(1, SGU_W), ln_b.reshape(1, SGU_W))


def _diff_lambda(lq1, lk1, lq2, lk2, lam_init):
    return (jnp.exp(jnp.sum(lq1 * lk1, axis=-1, keepdims=True))
            - jnp.exp(jnp.sum(lq2 * lk2, axis=-1, keepdims=True)) + lam_init)


def _pair_slopes(hp):
    s0 = jnp.where(hp == 0, 2.0 ** -1, jnp.where(hp == 1, 2.0 ** -3, jnp.where(hp == 2, 2.0 ** -5, 2.0 ** -7)))
    return s0.astype(F32), (s0 * 0.5).astype(F32)


def _bf16_round(x):
    bits = struct.unpack("<I", struct.pack("<f", x))[0]
    bits = (bits + 0x7FFF + ((bits >> 16) & 1)) & 0xFFFF0000
    return struct.unpack("<f", struct.pack("<I", bits))[0]


L2E_PARTS = []
_rest = LOG2E
for _ in range(3):
    L2E_PARTS.append(_bf16_round(_rest))
    _rest -= L2E_PARTS[-1]
POS_SPLIT = 64
N_POS_FEATS = 2 * len(L2E_PARTS)
EXTRA_ROWS = 16


def _attn_prompt_kernel(qt_ref, k_ref, vt_ref, pf_ref, rel_ref, lq1_ref, lk1_ref, lq2_ref, lk2_ref, g_ref, o_ref,
                        qst_ref, m_ref, acc_ref, *, tq, tk, lam_init):
    hp = pl.program_id(1)
    qi = pl.program_id(2)
    cols = 4 * tq
    qt = qt_ref[0]
    sub = lax.broadcasted_iota(jnp.int32, (LANES, tq), 0)
    for c in range(4):
        qst_ref[0:LANES, c * tq:(c + 1) * tq] = jnp.where((sub >= D_QK * c) & (sub < D_QK * (c + 1)), qt,
                                                          jnp.zeros_like(qt))
    s0, s1 = _pair_slopes(hp)
    col = lax.broadcasted_iota(jnp.int32, (EXTRA_ROWS, cols), 1)
    row = lax.broadcasted_iota(jnp.int32, (EXTRA_ROWS, cols), 0)
    feat = jnp.zeros((EXTRA_ROWS, cols), F32)
    for i, part in enumerate(L2E_PARTS):
        feat = jnp.where(row == i, POS_SPLIT * part, feat)
        feat = jnp.where(row == len(L2E_PARTS) + i, part, feat)
    qst_ref[LANES:LANES + EXTRA_ROWS, :] = (jnp.where(col < 2 * tq, s0, s1) * feat).astype(BF16)
    qst_ref[LANES + EXTRA_ROWS:, :] = jnp.zeros((LANES - EXTRA_ROWS, cols), BF16)
    m_ref[...] = jnp.full((1, cols), NEG_INF, F32)
    acc_ref[...] = jnp.zeros((LANES + EXTRA_ROWS, cols), F32)

    def chunks(kj0, n_sub, masked):
        sts = []
        for i in range(n_sub):
            start = pl.multiple_of((kj0 + i) * tk, tk)
            kaug = jnp.concatenate([k_ref[0, pl.ds(start, tk), :], pf_ref[pl.ds(start, tk), :]], axis=1)
            st = jnp.dot(kaug, qst_ref[...], preferred_element_type=F32)
            if masked:
                off = ((kj0 + i) * tk - qi * tq).astype(F32)
                st = jnp.where(rel_ref[...] + off <= 0.0, st, NEG_INF)
            sts.append(st)
        m_old = m_ref[...]
        m_new = m_old
        for st in sts:
            m_new = jnp.maximum(m_new, jnp.max(st, axis=0, keepdims=True))
        acc = acc_ref[...] * jnp.exp2(m_old - m_new)
        for i, st in enumerate(sts):
            pt = jnp.exp2(st - m_new).astype(BF16)
            vaug = jnp.concatenate([vt_ref[0, kj0 + i], jnp.ones((EXTRA_ROWS, tk), BF16)], axis=0)
            acc = acc + jnp.dot(vaug, pt, preferred_element_type=F32)
        acc_ref[...] = acc
        m_ref[...] = m_new

    n_full = (qi * tq) // tk

    def body(i, carry):
        chunks(2 * i, 2, False)
        return carry

    lax.fori_loop(0, n_full // 2, body, 0)

    @pl.when(n_full % 2 == 1)
    def _():
        chunks(n_full - 1, 1, False)

    chunks(n_full, 1, True)

    lam = _diff_lambda(lq1_ref[...], lk1_ref[...], lq2_ref[...], lk2_ref[...], lam_init)
    ot = acc_ref[0:LANES, :] / acc_ref[LANES:LANES + 1, :]
    at = jnp.where(sub < D_V, ot[:, 0:tq] - lam * ot[:, tq:2 * tq], ot[:, 2 * tq:3 * tq] - lam * ot[:, 3 * tq:])
    a2 = at * at
    ss0 = jnp.sum(a2[0:D_V], axis=0, keepdims=True)
    ss1 = jnp.sum(a2[D_V:], axis=0, keepdims=True)
    ms = jnp.where(sub < D_V, ss0, ss1) * (1.0 / D_V)
    an = at * lax.rsqrt(ms + EPS) * g_ref[...] * (1.0 - lam_init)
    o_ref[0] = an.T.astype(BF16)


def _attn_prompt(qtb, kb, vtb, lam_params, g_subln, lam_init, *, tq=256):
    b, w, s = qtb.shape
    n_chunks, tk = vtb.shape[1], vtb.shape[3]
    n_pairs = w // LANES
    cols = 4 * tq
    g2 = jnp.concatenate([g_subln, g_subln]).reshape(LANES, 1)
    pos = jnp.arange(s, dtype=jnp.int32)
    hi, lo = (pos // POS_SPLIT).astype(BF16), (pos % POS_SPLIT).astype(BF16)
    n_parts = len(L2E_PARTS)
    pf = jnp.stack([hi] * n_parts + [lo] * n_parts + [jnp.zeros_like(hi)] * (LANES - N_POS_FEATS), axis=1)
    rel = (jnp.arange(tk, dtype=jnp.int32)[:, None] - (jnp.arange(cols, dtype=jnp.int32) % tq)[None, :]).astype(F32)
    const2 = lambda bi, hp, qi: (0, 0)
    lam_specs = [pl.BlockSpec((1, D_QK), const2)] * 4
    return pl.pallas_call(
        functools.partial(_attn_prompt_kernel, tq=tq, tk=tk, lam_init=lam_init),
        grid=(b, n_pairs, s // tq),
        in_specs=[pl.BlockSpec((1, LANES, tq), lambda bi, hp, qi: (bi, hp, qi)),
                  pl.BlockSpec((1, s, LANES), lambda bi, hp, qi: (bi, 0, hp)),
                  pl.BlockSpec((1, n_chunks, LANES, tk), lambda bi, hp, qi: (bi, 0, hp, 0)),
                  pl.BlockSpec((s, LANES), const2),
                  pl.BlockSpec((tk, cols), const2),
                  *lam_specs,
                  pl.BlockSpec((LANES, 1), const2)],
        out_specs=pl.BlockSpec((1, tq, LANES), lambda bi, hp, qi: (bi, qi, hp)),
        out_shape=jax.ShapeDtypeStruct((b, s, w), BF16),
        scratch_shapes=[pltpu.VMEM((2 * LANES, cols), BF16),
                        pltpu.VMEM((1, cols), F32),
                        pltpu.VMEM((LANES + EXTRA_ROWS, cols), F32)],
        compiler_params=_cparams("arbitrary", "arbitrary", "arbitrary"),
        name="attn_prompt",
    )(qtb, kb, vtb, pf, rel, *[p.reshape(1, D_QK) for p in lam_params], g2)


def _attn_sample_kernel(pt_ref, q_ref, kn_ref, vn_ref, lq1_ref, lk1_ref, lq2_ref, lk2_ref, g_ref, *rest,
                        pps, t_new, past, lam_init):
    k_refs = rest[:pps]
    v_refs = rest[pps:2 * pps]
    o_ref = rest[2 * pps]
    qbd_ref, m_ref, l_ref, acc_ref = rest[2 * pps + 1:]
    j = pl.program_id(1)
    n_steps = pl.num_programs(1)
    w = q_ref.shape[-1]
    n_hj = 2 * N_ATT_HEADS
    rows = n_hj * t_new
    blk = pps * PAGE_SIZE

    row_id = lax.broadcasted_iota(jnp.int32, (rows, 1), 0)
    head = row_id // (2 * t_new)
    slope = jnp.zeros((rows, 1), F32)
    for h in range(N_ATT_HEADS):
        slope = jnp.where(head == h, 2.0 ** -(h + 1), slope)
    slope = slope * LOG2E
    qpos = (past + row_id % t_new).astype(F32)

    @pl.when(j == 0)
    def _():
        q = q_ref[0]
        lane = lax.broadcasted_iota(jnp.int32, (t_new, w), 1)
        for hj in range(n_hj):
            qbd_ref[hj * t_new:(hj + 1) * t_new, :] = jnp.where(
                (lane >= D_QK * hj) & (lane < D_QK * (hj + 1)), q, jnp.zeros_like(q))
        m_ref[...] = jnp.full((rows, 1), NEG_INF, F32)
        l_ref[...] = jnp.zeros((rows, 1), F32)
        acc_ref[...] = jnp.zeros((rows, w), F32)

    def update(s, vmat, v_contract_dim):
        m_old = m_ref[...]
        m_new = jnp.maximum(m_old, jnp.max(s, axis=-1, keepdims=True))
        p = jnp.exp2(s - m_new)
        corr = jnp.exp2(m_old - m_new)
        l_ref[...] = corr * l_ref[...] + jnp.sum(p, axis=-1, keepdims=True)
        pv = lax.dot_general(p.astype(vmat.dtype), vmat, (((1,), (v_contract_dim,)), ((), ())),
                             preferred_element_type=F32)
        acc_ref[...] = corr * acc_ref[...] + pv
        m_ref[...] = m_new

    kt_all = jnp.concatenate([kr[0].astype(BF16) for kr in k_refs], axis=1)
    vt_all = jnp.concatenate([vr[0].astype(BF16) for vr in v_refs], axis=1)
    s = jnp.dot(qbd_ref[...], kt_all, preferred_element_type=F32)
    kpos = (j * blk + lax.broadcasted_iota(jnp.int32, (1, blk), 1)).astype(F32)
    update(s - slope * (qpos - kpos), vt_all, 1)

    @pl.when(j == n_steps - 1)
    def _():
        kn = kn_ref[0]
        sn = lax.dot_general(qbd_ref[...].astype(F32), kn, (((1,), (1,)), ((), ())), preferred_element_type=F32)
        kposn = (past + lax.broadcasted_iota(jnp.int32, (1, t_new), 1)).astype(F32)
        dist = qpos - kposn
        update(jnp.where(dist >= 0, sn - slope * dist, NEG_INF), vn_ref[0], 0)

        lam = _diff_lambda(lq1_ref[...], lk1_ref[...], lq2_ref[...], lk2_ref[...], lam_init)
        o = acc_ref[...] / l_ref[...]
        lane = lax.broadcasted_iota(jnp.int32, (t_new, w), 1)
        a = jnp.zeros((t_new, w), F32)
        for h in range(N_ATT_HEADS):
            o1 = o[(2 * h) * t_new:(2 * h + 1) * t_new]
            o2 = o[(2 * h + 1) * t_new:(2 * h + 2) * t_new]
            a = jnp.where((lane >= D_V * h) & (lane < D_V * (h + 1)), o1 - lam * o2, a)
        a2 = a * a
        ms = jnp.zeros((t_new, w), F32)
        for h in range(N_ATT_HEADS):
            in_h = (lane >= D_V * h) & (lane < D_V * (h + 1))
            ssh = jnp.sum(jnp.where(in_h, a2, 0.0), axis=-1, keepdims=True)
            ms = jnp.where(in_h, ssh, ms)
        o_ref[0] = (a * lax.rsqrt(ms * (1.0 / D_V) + EPS) * g_ref[...] * (1.0 - lam_init)).astype(BF16)


def _attn_sample(page_table, qb, kb, vb, cache_k, cache_v, lam_params, g_subln, lam_init, *, pps=16):
    bd, t_new, w = qb.shape
    n_pages = page_table.shape[1]
    n_pool = cache_k.shape[0]
    past = n_pages * PAGE_SIZE
    ck = jnp.transpose(cache_k, (0, 2, 3, 1)).reshape(n_pool, w, PAGE_SIZE)
    cv = jnp.transpose(cache_v, (0, 2, 3, 1)).reshape(n_pool, w, PAGE_SIZE)
    g8 = jnp.tile(g_subln, N_ATT_HEADS).reshape(1, w)
    new_spec = pl.BlockSpec((1, t_new, w), lambda b, j, pt: (b, 0, 0))
    small = lambda shape: pl.BlockSpec(shape, lambda b, j, pt: (0, 0))

    def page_spec(i):
        return pl.BlockSpec((1, w, PAGE_SIZE), lambda b, j, pt: (pt[b, j * pps + i], 0, 0))

    grid_spec = pltpu.PrefetchScalarGridSpec(
        num_scalar_prefetch=1,
        grid=(bd, n_pages // pps),
        in_specs=[new_spec, new_spec, new_spec,
                  small((1, D_QK)), small((1, D_QK)), small((1, D_QK)), small((1, D_QK)), small((1, w)),
                  *[page_spec(i) for i in range(pps)],
                  *[page_spec(i) for i in range(pps)]],
        out_specs=pl.BlockSpec((1, t_new, w), lambda b, j, pt: (b, 0, 0)),
        scratch_shapes=[pltpu.VMEM((2 * N_ATT_HEADS * t_new, w), BF16),
                        pltpu.VMEM((2 * N_ATT_HEADS * t_new, 1), F32),
                        pltpu.VMEM((2 * N_ATT_HEADS * t_new, 1), F32),
                        pltpu.VMEM((2 * N_ATT_HEADS * t_new, w), F32)],
    )
    return pl.pallas_call(
        functools.partial(_attn_sample_kernel, pps=pps, t_new=t_new, past=past, lam_init=lam_init),
        grid_spec=grid_spec,
        out_shape=jax.ShapeDtypeStruct((bd, t_new, w), BF16),
        compiler_params=_cparams("arbitrary", "arbitrary"),
        name="attn_sample",
    )(page_table, qb, kb, vb, *[p.reshape(1, D_QK) for p in lam_params], g8,
      *([ck] * pps), *([cv] * pps))


def _outproj_kernel(a_ref, u_ref, vs_ref, x_ref, ga1_ref, sc2_ref, sh2_ref, wcat_ref, sbias_ref, wout_ref,
                    gpost_ref, gpre_ref, wr_cat_ref, br_ref, tril_ref, *rest, group_dispatch):
    if group_dispatch:
        x1_ref, pay_ref, info_ref, counts_ref, sg_ref, run_ref = rest
    else:
        x1_ref, h2_ref, gates_ref, sg_ref = rest
    tm = x_ref.shape[0]
    sgu_w = u_ref.shape[1]
    lane = lax.broadcasted_iota(jnp.int32, (CHUNK, LANES), 1)
    for ci in range(tm // CHUNK):
        rs = slice(ci * CHUNK, (ci + 1) * CHUNK)
        for gp in range(sgu_w // LANES):
            cs = slice(gp * LANES, (gp + 1) * LANES)
            vp = vs_ref[rs, cs]
            zero = jnp.zeros_like(vp)
            rhs = jnp.concatenate([jnp.where(lane < SGU_GC, vp, zero), jnp.where(lane >= SGU_GC, vp, zero)], axis=0)
            mixed = jnp.dot(wcat_ref[gp], rhs, preferred_element_type=F32) + sbias_ref[:, cs]
            sg_ref[rs, cs] = (u_ref[rs, cs].astype(F32) * mixed).astype(BF16)
    att_w = a_ref.shape[1]
    mix = (jnp.dot(a_ref[...], wout_ref[0:att_w, :], preferred_element_type=F32)
           + jnp.dot(sg_ref[...], wout_ref[att_w:, :], preferred_element_type=F32))
    x1 = x_ref[...] + ga1_ref[0] * _rms(mix, gpost_ref[...])
    x1_ref[...] = x1
    h2 = _rms(x1, gpre_ref[...]) * (1.0 + sc2_ref[0]) + sh2_ref[0]
    h2_hi = h2.astype(BF16)
    h2_lo = (h2 - h2_hi.astype(F32)).astype(BF16)
    hh = jnp.dot(h2_hi, wr_cat_ref[...], preferred_element_type=F32)
    logits = (hh[:, :LANES] + hh[:, LANES:]
              + jnp.dot(h2_lo, wr_cat_ref[:, :LANES], preferred_element_type=F32)) + br_ref[...]
    ln = lax.broadcasted_iota(jnp.int32, (tm, LANES), 1)
    big = jnp.int32(LANES)
    is_g = (ln >= N_EXPERTS) & (ln < N_EXPERTS + N_EXPERT_GROUPS)
    gl = jnp.where(is_g, logits, NEG_INF)
    gmax = jnp.max(gl, axis=-1, keepdims=True)
    g_idx = jnp.min(jnp.where(gl == gmax, ln, big), axis=-1, keepdims=True) - N_EXPERTS
    g_w = 1.0 / jnp.sum(jnp.exp(gl - gmax), axis=-1, keepdims=True)
    in_grp = (ln >= g_idx * EXPERTS_PER_GROUP) & (ln < (g_idx + 1) * EXPERTS_PER_GROUP)
    el = jnp.where(in_grp, logits, NEG_INF)
    emax = jnp.max(el, axis=-1, keepdims=True)
    ez = jnp.exp(el - emax)
    prob = ez / jnp.sum(ez, axis=-1, keepdims=True)
    p1 = jnp.max(prob, axis=-1, keepdims=True)
    i1 = jnp.min(jnp.where((prob == p1) & in_grp, ln, big), axis=-1, keepdims=True)
    rest = jnp.where(in_grp & (ln != i1), prob, -1.0)
    p2 = jnp.max(rest, axis=-1, keepdims=True)
    i2 = jnp.min(jnp.where(rest == p2, ln, big), axis=-1, keepdims=True)
    den = p1 + p2
    gates = jnp.where(ln == i1, p1 / den * g_w, 0.0) + jnp.where(ln == i2, p2 / den * g_w, 0.0)
    if not group_dispatch:
        h2_ref[...] = h2_hi
        gates_ref[...] = gates[:, :N_EXPERTS]
        return

    bits = pltpu.bitcast(h2_hi.astype(F32), jnp.uint32)
    half = PAYLOAD_H2_SEGS * LANES
    for s in range(PAYLOAD_H2_SEGS):
        lo = bits[:, s * LANES:(s + 1) * LANES]
        hi = bits[:, half + s * LANES:half + (s + 1) * LANES]
        pay_ref[s] = hi | lax.shift_right_logical(lo, jnp.uint32(16))
    pay_ref[PAYLOAD_H2_SEGS] = pltpu.bitcast(gates, jnp.uint32)

    @pl.when(pl.program_id(0) == 0)
    def _():
        run_ref[...] = jnp.zeros_like(run_ref)

    onehot = jnp.where(ln == g_idx, 1.0, 0.0).astype(BF16)
    csum = jnp.dot(tril_ref[...], onehot, preferred_element_type=F32) + run_ref[...]
    rank = jnp.sum(jnp.where(ln == g_idx, csum, 0.0), axis=-1, keepdims=True) - 1.0
    run_ref[...] = csum[tm - 1:tm, :]
    counts_ref[...] = csum[tm - 1:tm, :]
    info = jnp.where(ln == 0, g_idx, jnp.where(ln == 1, rank.astype(jnp.int32), 0))
    info_ref[...] = info[:, :INFO_W]


def _outproj(a, u, vsb, x, ga1, sc2, sh2, wcat, sbias, w_out_bf, g_post, g_pre, wr_cat, br,
             *, tm, tiles_per_mod, group_dispatch):
    n, d = x.shape
    att_w = a.shape[1]
    sgu_w = u.shape[1]
    r = ga1.shape[1]
    row = lambda i: (i, 0)
    modmap = lambda i: (i // tiles_per_mod, 0, 0)
    const = lambda i: (0, 0)
    tril = jnp.tril(jnp.ones((tm, tm), BF16))
    if group_dispatch:
        out_specs = [pl.BlockSpec((tm, d), row),
                     pl.BlockSpec((PAYLOAD_SEGS, tm, LANES), lambda i: (0, i, 0)),
                     pl.BlockSpec((tm, INFO_W), row),
                     pl.BlockSpec((1, LANES), const)]
        out_shape = [jax.ShapeDtypeStruct((n, d), F32),
                     jax.ShapeDtypeStruct((PAYLOAD_SEGS, n, LANES), jnp.uint32),
                     jax.ShapeDtypeStruct((n, INFO_W), jnp.int32),
                     jax.ShapeDtypeStruct((1, LANES), F32)]
        scratch = [pltpu.VMEM((tm, sgu_w), BF16), pltpu.VMEM((1, LANES), F32)]
    else:
        out_specs = [pl.BlockSpec((tm, d), row),
                     pl.BlockSpec((tm, d), row),
                     pl.BlockSpec((tm, N_EXPERTS), row)]
        out_shape = [jax.ShapeDtypeStruct((n, d), F32),
                     jax.ShapeDtypeStruct((n, d), BF16),
                     jax.ShapeDtypeStruct((n, N_EXPERTS), F32)]
        scratch = [pltpu.VMEM((tm, sgu_w), BF16)]
    return pl.pallas_call(
        functools.partial(_outproj_kernel, group_dispatch=group_dispatch),
        grid=(n // tm,),
        in_specs=[pl.BlockSpec((tm, att_w), row),
                  pl.BlockSpec((tm, sgu_w), row),
                  pl.BlockSpec((tm, sgu_w), row),
                  pl.BlockSpec((tm, d), row),
                  pl.BlockSpec((1, r, d), modmap),
                  pl.BlockSpec((1, r, d), modmap),
                  pl.BlockSpec((1, r, d), modmap),
                  pl.BlockSpec(wcat.shape, lambda i: (0, 0, 0)),
                  pl.BlockSpec(sbias.shape, const),
                  pl.BlockSpec(w_out_bf.shape, const),
                  pl.BlockSpec((1, d), const),
                  pl.BlockSpec((1, d), const),
                  pl.BlockSpec(wr_cat.shape, const),
                  pl.BlockSpec((1, LANES), const),
                  pl.BlockSpec((tm, tm), const)],
        out_specs=out_specs,
        out_shape=out_shape,
        scratch_shapes=scratch,
        compiler_params=_cparams("arbitrary"),
        name="outproj",
    )(a, u, vsb, x, ga1, sc2, sh2, wcat, sbias, w_out_bf, g_post.reshape(1, d), g_pre.reshape(1, d),
      wr_cat, br, tril)


def _moe_kernel(h_ref, gates_ref, x1_ref, ga2_ref, gpost_ref, wg_ref, wu_ref, wd_ref, o_ref, acc_ref):
    e = pl.program_id(1)

    @pl.when(e == 0)
    def _():
        acc_ref[...] = jnp.zeros_like(acc_ref)

    h = h_ref[...]
    gate = jnp.dot(h, wg_ref[0], preferred_element_type=F32)
    up = jnp.dot(h, wu_ref[0], preferred_element_type=F32)
    hdn = (gate * jax.nn.sigmoid(gate) * up).astype(BF16)
    y = jnp.dot(hdn, wd_ref[0], preferred_element_type=F32)
    ln = lax.broadcasted_iota(jnp.int32, gates_ref.shape, 1)
    ge = jnp.sum(jnp.where(ln == e, gates_ref[...], 0.0), axis=-1, keepdims=True)
    acc_ref[...] += ge * y

    @pl.when(e == pl.num_programs(1) - 1)
    def _():
        o_ref[...] = x1_ref[...] + ga2_ref[0] * _rms(acc_ref[...], gpost_ref[...])


def _moe(h2, gates, x1, ga2, g_post, wg_bf, wu_bf, wd_bf, *, tm, tiles_per_mod):
    n, d = x1.shape
    n_e, _, de = wg_bf.shape
    r = ga2.shape[1]
    row = lambda i, e: (i, 0)
    return pl.pallas_call(
        _moe_kernel,
        grid=(n // tm, n_e),
        in_specs=[pl.BlockSpec((tm, d), row),
                  pl.BlockSpec((tm, n_e), row),
                  pl.BlockSpec((tm, d), row),
                  pl.BlockSpec((1, r, d), lambda i, e: (i // tiles_per_mod, 0, 0)),
                  pl.BlockSpec((1, d), lambda i, e: (0, 0)),
                  pl.BlockSpec((1, d, de), lambda i, e: (e, 0, 0)),
                  pl.BlockSpec((1, d, de), lambda i, e: (e, 0, 0)),
                  pl.BlockSpec((1, de, d), lambda i, e: (e, 0, 0))],
        out_specs=pl.BlockSpec((tm, d), row),
        out_shape=jax.ShapeDtypeStruct((n, d), F32),
        scratch_shapes=[pltpu.VMEM((tm, d), F32)],
        compiler_params=_cparams("arbitrary", "arbitrary"),
        name="moe",
    )(h2, gates, x1, ga2, g_post.reshape(1, d), wg_bf, wu_bf, wd_bf)


def _sc_mesh():
    return plsc.VectorSubcoreMesh(core_axis_name="core", subcore_axis_name="subcore")


def _sc_scatter_rows(x, idx, n_out_rows):
    n_rows, width = x.shape

    @pl.kernel(out_type=jax.ShapeDtypeStruct((n_out_rows, width), x.dtype), mesh=_sc_mesh(), scratch_types=[])
    def scatter(x_hbm, i_hbm, o_hbm):
        def body(x_vmem, i_vmem):
            pltpu.sync_copy(x_vmem, o_hbm.at[i_vmem.at[0]])

        pltpu.emit_pipeline(
            body,
            grid=(n_rows // SC_WINDOW,),
            in_specs=[pl.BlockSpec((SC_WINDOW, width), lambda i: (i, 0)),
                      pl.BlockSpec((1, SC_WINDOW), lambda i: (0, i))],
            out_specs=[],
            core_axis_name=("core", "subcore"),
            dimension_semantics=(pltpu.PARALLEL,),
        )(x_hbm, i_hbm)

    return scatter(x, idx.reshape(1, n_rows))


def _sc_gather_rows(x, idx):
    n_rows = idx.shape[0]
    width = x.shape[1]

    @pl.kernel(out_type=jax.ShapeDtypeStruct((n_rows, width), x.dtype), mesh=_sc_mesh(), scratch_types=[])
    def gather(x_hbm, i_hbm, o_hbm):
        def body(i_vmem, o_vmem):
            pltpu.sync_copy(x_hbm.at[i_vmem.at[0]], o_vmem)

        pltpu.emit_pipeline(
            body,
            grid=(n_rows // SC_WINDOW,),
            in_specs=[pl.BlockSpec((1, SC_WINDOW), lambda i: (0, i))],
            out_specs=[pl.BlockSpec((SC_WINDOW, width), lambda i: (i, 0))],
            core_axis_name=("core", "subcore"),
            dimension_semantics=(pltpu.PARALLEL,),
        )(i_hbm, o_hbm)

    return gather(x, idx.reshape(1, n_rows))


def _group_moe_kernel(tg_ref, nused_ref, xs_ref, gpost_ref, wg_ref, wu_ref, wd_ref, rs_ref):
    t = pl.program_id(0)
    tm = xs_ref.shape[1]

    @pl.when(t >= nused_ref[0])
    def _():
        rs_ref[...] = jnp.zeros_like(rs_ref)

    @pl.when(t < nused_ref[0])
    def _():
        words = [xs_ref[s] for s in range(PAYLOAD_H2_SEGS)]
        lo = [pltpu.bitcast(lax.shift_left(w, jnp.uint32(16)), F32) for w in words]
        hi = [pltpu.bitcast(w & jnp.uint32(0xFFFF0000), F32) for w in words]
        x = jnp.concatenate(lo + hi, axis=1).astype(BF16)
        gates = pltpu.bitcast(xs_ref[PAYLOAD_H2_SEGS], F32)
        ln = lax.broadcasted_iota(jnp.int32, (tm, LANES), 1)
        first = tg_ref[t] * EXPERTS_PER_GROUP
        y = jnp.zeros((tm, x.shape[1]), F32)
        for el in range(EXPERTS_PER_GROUP):
            ge = jnp.sum(jnp.where(ln == first + el, gates, 0.0), axis=-1, keepdims=True)
            gate = jnp.dot(x, wg_ref[el], preferred_element_type=F32)
            up = jnp.dot(x, wu_ref[el], preferred_element_type=F32)
            hdn = (gate * jax.nn.sigmoid(gate) * up).astype(BF16)
            y = y + ge * jnp.dot(hdn, wd_ref[el], preferred_element_type=F32)
        r = _rms(y, gpost_ref[...])
        for s in range(rs_ref.shape[0]):
            rs_ref[s] = r[:, s * LANES:(s + 1) * LANES]


def _group_moe(xs, tile_group, n_used, g_post, wg_bf, wu_bf, wd_bf, *, tm):
    _, p_rows, _ = xs.shape
    n_e, d, de = wg_bf.shape
    g = EXPERTS_PER_GROUP
    grid_spec = pltpu.PrefetchScalarGridSpec(
        num_scalar_prefetch=2,
        grid=(p_rows // tm,),
        in_specs=[pl.BlockSpec((PAYLOAD_SEGS, tm, LANES), lambda t, tg, nu: (0, t, 0)),
                  pl.BlockSpec((1, d), lambda t, tg, nu: (0, 0)),
                  pl.BlockSpec((g, d, de), lambda t, tg, nu: (tg[t], 0, 0)),
                  pl.BlockSpec((g, d, de), lambda t, tg, nu: (tg[t], 0, 0)),
                  pl.BlockSpec((g, de, d), lambda t, tg, nu: (tg[t], 0, 0))],
        out_specs=pl.BlockSpec((d // LANES, tm, LANES), lambda t, tg, nu: (0, t, 0)),
    )
    return pl.pallas_call(
        _group_moe_kernel,
        grid_spec=grid_spec,
        out_shape=jax.ShapeDtypeStruct((d // LANES, p_rows, LANES), F32),
        compiler_params=_cparams("arbitrary"),
        name="group_moe",
    )(tile_group, n_used, xs, g_post.reshape(1, d), wg_bf, wu_bf, wd_bf)


def _residual_kernel(x1_ref, ga2_ref, r_ref, o_ref):
    r = jnp.concatenate([r_ref[s] for s in range(r_ref.shape[0])], axis=1)
    o_ref[...] = x1_ref[...] + ga2_ref[0] * r


def _residual(x1, ga2, r_segs, *, tm, tiles_per_mod):
    n, d = x1.shape
    return pl.pallas_call(
        _residual_kernel,
        grid=(n // tm,),
        in_specs=[pl.BlockSpec((tm, d), lambda i: (i, 0)),
                  pl.BlockSpec((1, ga2.shape[1], d), lambda i: (i // tiles_per_mod, 0, 0)),
                  pl.BlockSpec((d // LANES, tm, LANES), lambda i: (0, i, 0))],
        out_specs=pl.BlockSpec((tm, d), lambda i: (i, 0)),
        out_shape=jax.ShapeDtypeStruct((n, d), F32),
        compiler_params=_cparams("arbitrary"),
        name="residual",
    )(x1, ga2, r_segs)


def _group_plan(info, counts, n_tokens, tm):
    n_tiles = n_tokens // tm + N_EXPERT_GROUPS
    cnt = counts[0, :N_EXPERT_GROUPS].astype(jnp.int32)
    padded = (cnt + tm - 1) // tm * tm
    ends = jnp.cumsum(padded)
    starts = ends - padded
    gid, rank = info[:, 0], info[:, 1]
    onehot = (gid[:, None] == jnp.arange(N_EXPERT_GROUPS, dtype=jnp.int32)[None, :]).astype(jnp.int32)
    pos = jnp.sum(onehot * starts[None, :], axis=1) + rank
    tile_ends = ends // tm
    t = jnp.arange(n_tiles, dtype=jnp.int32)
    tile_group = jnp.minimum(jnp.sum((t[:, None] >= tile_ends[None, :]).astype(jnp.int32), axis=1),
                             N_EXPERT_GROUPS - 1)
    return pos, tile_group, tile_ends[-1:], n_tiles * tm


def _sgu_weights(ws, bs, chunk_len):
    causal = jnp.tril(jnp.ones((chunk_len, chunk_len), ws.dtype))
    wm = ws[:, :chunk_len, :chunk_len] * causal
    reps = CHUNK // chunk_len
    if reps > 1:
        eye = jnp.eye(reps, dtype=ws.dtype)
        wm = jnp.einsum("ab,gts->gatbs", eye, wm).reshape(N_SGU_GROUPS, CHUNK, CHUNK)
    wcat = jnp.concatenate([wm[0::2], wm[1::2]], axis=2).astype(BF16)
    bt = jnp.tile(bs[:, :chunk_len].T, (reps, 1))
    sbias = jnp.repeat(bt, SGU_GC, axis=1)
    return wcat, sbias


def _split_mod(mod):
    return [m[:, None, :] for m in jnp.split(mod, 6, axis=-1)]


def kernel(x_prompt, x_sample, cache_k, cache_v, page_table, c_prompt, c_sample, w_ada, b_ada, g_pre_mix, g_post_mix, g_pre_ffn, g_post_ffn, w_in, lam_q1, lam_k1, lam_q2, lam_k2, g_subln, sgu_ln_g, sgu_ln_b, sgu_ws, sgu_bs, w_out, w_rg, b_rg, w_re, b_re, w_gate, w_up, w_down):
    depth = w_in.shape[0]
    assert depth == 1
    l = 0
    lam_init = 0.8 - 0.6 * math.exp(-0.3 * l)
    bp, sp, d = x_prompt.shape
    bs_, ts, _ = x_sample.shape
    n_s = bs_ * ts

    mod = _ada(jnp.concatenate([c_prompt, c_sample], axis=0), w_ada[l], b_ada[l])
    sh1p, sc1p, ga1p, sh2p, sc2p, ga2p = _split_mod(mod[:bp])
    rep = lambda m: jnp.repeat(m, ts, axis=0).reshape(1, n_s, d)
    sh1s, sc1s, ga1s, sh2s, sc2s, ga2s = [rep(m) for m in jnp.split(mod[bp:], 6, axis=-1)]

    w_in_bf = w_in[l].astype(BF16)
    wqv_t_bf = jnp.concatenate([w_in_bf[:, 0:ATT_W], w_in_bf[:, 2 * ATT_W:3 * ATT_W]], axis=1).T
    w_out_bf = w_out[l].astype(BF16)
    wg_bf = w_gate[l].astype(BF16)
    wu_bf = w_up[l].astype(BF16)
    wd_bf = w_down[l].astype(BF16)
    wr = jnp.concatenate([w_re[l], w_rg[l]], axis=1)
    wr = jnp.pad(wr, ((0, 0), (0, LANES - wr.shape[1])))
    wr_hi = wr.astype(BF16)
    wr_cat = jnp.concatenate([wr_hi, (wr - wr_hi.astype(F32)).astype(BF16)], axis=1)
    br = jnp.pad(jnp.concatenate([b_re[l], b_rg[l]]), (0, LANES - N_EXPERTS - N_EXPERT_GROUPS)).reshape(1, LANES)
    lam_params = (lam_q1[l], lam_k1[l], lam_q2[l], lam_k2[l])

    tm_p = 512
    xp = x_prompt.reshape(bp * sp, d)
    ktp, vtp, qtb, kb, vtb, u, vsb, vs_last = _inproj(
        xp, sc1p, sh1p, g_pre_mix[l], w_in_bf, wqv_t_bf, sgu_ln_g[l], sgu_ln_b[l],
        tm=tm_p, tiles_per_seq=sp // tm_p, last_rows=CHUNK, transposed_qv=True)
    w3 = ATT_W
    a = _attn_prompt(qtb, kb.reshape(bp, sp, w3), vtb, lam_params, g_subln[l], lam_init)
    kp = jnp.transpose(ktp.reshape(bp, N_ATT_HEADS, 2 * D_QK, sp), (0, 3, 1, 2))
    vp = jnp.transpose(vtp.reshape(bp, N_ATT_HEADS, D_V, sp), (0, 3, 1, 2))
    wcat_p, sbias_p = _sgu_weights(sgu_ws[l], sgu_bs[l], CHUNK)
    n_p = bp * sp
    x1, payload, info, counts = _outproj(
        a.reshape(n_p, w3), u, vsb, xp, ga1p, sc2p, sh2p, wcat_p, sbias_p, w_out_bf,
        g_post_mix[l], g_pre_ffn[l], wr_cat, br, tm=tm_p, tiles_per_mod=sp // tm_p, group_dispatch=True)
    tm_moe = 512
    pos, tile_group, n_used, p_rows = _group_plan(info, counts, n_p, tm_moe)
    seg_p = lambda k: (jnp.arange(k, dtype=jnp.int32)[:, None] * p_rows + pos[None, :]).reshape(-1)
    xs = _sc_scatter_rows(payload.reshape(PAYLOAD_SEGS * n_p, LANES), seg_p(PAYLOAD_SEGS), PAYLOAD_SEGS * p_rows)
    rs = _group_moe(xs.reshape(PAYLOAD_SEGS, p_rows, LANES), tile_group, n_used, g_post_ffn[l],
                    wg_bf, wu_bf, wd_bf, tm=tm_moe)
    n_seg = d // LANES
    r_tok = _sc_gather_rows(rs.reshape(n_seg * p_rows, LANES), seg_p(n_seg))
    yp = _residual(x1, ga2p, r_tok.reshape(n_seg, n_p, LANES), tm=tm_p, tiles_per_mod=sp // tm_p)

    xs = x_sample.reshape(n_s, d)
    ks, vs_, qbs, us, vsbs, vs_last_s = _inproj(
        xs, sc1s, sh1s, g_pre_mix[l], w_in_bf, wqv_t_bf, sgu_ln_g[l], sgu_ln_b[l],
        tm=n_s, tiles_per_seq=1, last_rows=n_s, transposed_qv=False)
    a_s = _attn_sample(page_table, qbs.reshape(bs_, ts, w3), ks.reshape(bs_, ts, w3), vs_.reshape(bs_, ts, w3),
                       cache_k[l], cache_v[l], lam_params, g_subln[l], lam_init)
    wcat_s, sbias_s = _sgu_weights(sgu_ws[l], sgu_bs[l], ts)
    x1s, h2s, gates_s = _outproj(a_s.reshape(n_s, w3), us, vsbs, xs, ga1s, sc2s, sh2s, wcat_s, sbias_s, w_out_bf,
                                 g_post_mix[l], g_pre_ffn[l], wr_cat, br, tm=n_s, tiles_per_mod=1,
                                 group_dispatch=False)
    ys = _moe(h2s, gates_s, x1s, ga2s, g_post_ffn[l], wg_bf, wu_bf, wd_bf, tm=n_s, tiles_per_mod=1)

    return (yp.reshape(bp, sp, d), ys.reshape(bs_, ts, d),
            kp[None], vp[None],
            vs_last.reshape(1, bp, CHUNK, -1),
            ks.reshape(1, bs_, ts, N_ATT_HEADS, 2 * D_QK), vs_.reshape(1, bs_, ts, N_ATT_HEADS, D_V),
            vs_last_s.reshape(1, bs_, ts, -1))
```

```python
import functools
import math
import struct

import jax
import jax.numpy as jnp
from jax import lax
from jax.experimental import pallas as pl
from jax.experimental.pallas import tpu as pltpu
from jax.experimental.pallas import tpu_sc as plsc

F32 = jnp.float32
BF16 = jnp.bfloat16

EPS = 1e-6
N_ATT_HEADS = 8
D_QK = 32
D_V = 64
ATT_W = N_ATT_HEADS * D_V
N_SGU_GROUPS = 8
SGU_GC = 64
SGU_W = N_SGU_GROUPS * SGU_GC
CHUNK = 128
N_EXPERT_GROUPS = 4
EXPERTS_PER_GROUP = 4
N_EXPERTS = 16
PAGE_SIZE = 128
LANES = 128
VMEM_LIMIT = 56 * 1024 * 1024
NEG_INF = float("-inf")
LOG2E = math.log2(math.e)
PAYLOAD_H2_SEGS = 4
PAYLOAD_SEGS = PAYLOAD_H2_SEGS + 1
INFO_W = 8
SC_WINDOW = 128


def _cparams(*sem):
    return pltpu.CompilerParams(dimension_semantics=sem, vmem_limit_bytes=VMEM_LIMIT)


def _rms(x, g):
    return x * lax.rsqrt(jnp.mean(x * x, axis=-1, keepdims=True) + EPS) * g


def _ada_kernel(c_ref, w_ref, b_ref, o_ref):
    c = c_ref[...]
    s = c * jax.nn.sigmoid(c)
    o_ref[...] = jnp.dot(s.astype(BF16), w_ref[...].astype(BF16), preferred_element_type=F32) + b_ref[...]


def _ada(c, w_ada, b_ada):
    n, d = c.shape
    nout = w_ada.shape[1]
    tn = d
    return pl.pallas_call(
        _ada_kernel,
        grid=(nout // tn,),
        in_specs=[pl.BlockSpec((n, d), lambda j: (0, 0)),
                  pl.BlockSpec((d, tn), lambda j: (0, j)),
                  pl.BlockSpec((1, tn), lambda j: (0, j))],
        out_specs=pl.BlockSpec((n, tn), lambda j: (0, j)),
        out_shape=jax.ShapeDtypeStruct((n, nout), F32),
        compiler_params=_cparams("arbitrary"),
        name="ada",
    )(c, w_ada, b_ada.reshape(1, nout))


Q_SCALE = D_QK ** -0.5 * LOG2E


def _inproj_kernel(x_ref, sc_ref, sh_ref, g_ref, w_ref, wqv_t_ref, lng_ref, lnb_ref, *out_refs,
                   last_rows, transposed_qv):
    x = x_ref[...]
    h = (_rms(x, g_ref[...]) * (1.0 + sc_ref[0]) + sh_ref[0]).astype(BF16)

    def proj(lo):
        return jnp.dot(h, w_ref[:, lo:lo + ATT_W], preferred_element_type=F32)

    def proj_t(lo):
        return lax.dot_general(wqv_t_ref[lo:lo + ATT_W, :], h, (((1,), (1,)), ((), ())),
                               preferred_element_type=F32)

    zk = proj(ATT_W)
    if transposed_qv:
        kt_ref, vt_ref, qtb_ref, kb_ref, vtb_ref, u_ref, vsb_ref, vsl_ref = out_refs
        kt_ref[0] = zk.T
        kb_ref[...] = zk.astype(BF16)
        qtb_ref[0] = (proj_t(0) * Q_SCALE).astype(BF16)
        zvt = proj_t(ATT_W)
        vt_ref[0] = zvt
        vtb_ref[0, 0] = zvt.astype(BF16)
    else:
        k_ref, v_ref, qb_ref, u_ref, vsb_ref, vsl_ref = out_refs
        k_ref[...] = zk
        qb_ref[...] = (proj(0) * Q_SCALE).astype(BF16)
        v_ref[...] = proj(2 * ATT_W)
    u_ref[...] = jax.nn.gelu(proj(3 * ATT_W)).astype(BF16)
    gs = jax.nn.gelu(proj(3 * ATT_W + SGU_W))
    mu = jnp.mean(gs, axis=-1, keepdims=True)
    xc = gs - mu
    vs = xc * lax.rsqrt(jnp.mean(xc * xc, axis=-1, keepdims=True) + EPS) * lng_ref[...] + lnb_ref[...]
    vsb_ref[...] = vs.astype(BF16)
    tm = vs.shape[0]
    vsl_ref[0] = vs[tm - last_rows:, :]


def _inproj(x, sc, sh, g, w_in_bf, wqv_t_bf, ln_g, ln_b, *, tm, tiles_per_seq, last_rows, transposed_qv):
    n, d = x.shape
    in_w = w_in_bf.shape[1]
    r = sc.shape[1]
    n_tiles = n // tm
    n_seq = n_tiles // tiles_per_seq
    s_len = tiles_per_seq * tm
    row = lambda i: (i, 0)
    modmap = lambda i: (i // tiles_per_seq, 0, 0)
    const = lambda i: (0, 0)
    col_t = lambda i: (i // tiles_per_seq, 0, i % tiles_per_seq)
    row_spec = pl.BlockSpec((tm, ATT_W), row)
    tail = [(pl.BlockSpec((tm, SGU_W), row), jax.ShapeDtypeStruct((n, SGU_W), BF16)),
            (pl.BlockSpec((tm, SGU_W), row), jax.ShapeDtypeStruct((n, SGU_W), BF16)),
            (pl.BlockSpec((1, last_rows, SGU_W), lambda i: (i // tiles_per_seq, 0, 0)),
             jax.ShapeDtypeStruct((n_seq, last_rows, SGU_W), F32))]
    if transposed_qv:
        head = [(pl.BlockSpec((1, ATT_W, tm), col_t), jax.ShapeDtypeStruct((n_seq, ATT_W, s_len), F32)),
                (pl.BlockSpec((1, ATT_W, tm), col_t), jax.ShapeDtypeStruct((n_seq, ATT_W, s_len), F32)),
                (pl.BlockSpec((1, ATT_W, tm), col_t), jax.ShapeDtypeStruct((n_seq, ATT_W, s_len), BF16)),
                (row_spec, jax.ShapeDtypeStruct((n, ATT_W), BF16)),
                (pl.BlockSpec((1, 1, ATT_W, tm), lambda i: (i // tiles_per_seq, i % tiles_per_seq, 0, 0)),
                 jax.ShapeDtypeStruct((n_seq, tiles_per_seq, ATT_W, tm), BF16))]
    else:
        head = [(row_spec, jax.ShapeDtypeStruct((n, ATT_W), F32)),
                (row_spec, jax.ShapeDtypeStruct((n, ATT_W), F32)),
                (row_spec, jax.ShapeDtypeStruct((n, ATT_W), BF16))]
    specs, shapes = zip(*(head + tail))
    return pl.pallas_call(
        functools.partial(_inproj_kernel, last_rows=last_rows, transposed_qv=transposed_qv),
        grid=(n_tiles,),
        in_specs=[pl.BlockSpec((tm, d), row),
                  pl.BlockSpec((1, r, d), modmap),
                  pl.BlockSpec((1, r, d), modmap),
                  pl.BlockSpec((1, d), const),
                  pl.BlockSpec((d, in_w), const),
                  pl.BlockSpec(wqv_t_bf.shape, const),
                  pl.BlockSpec((1, SGU_W), const),
                  pl.BlockSpec((1, SGU_W), const)],
        out_specs=list(specs),
        out_shape=list(shapes),
        compiler_params=_cparams("arbitrary"),
        name="inproj",
    )(x, sc, sh, g.reshape(1, d), w_in_bf, wqv_t_bf, ln_g.reshape(1, SGU_W), ln_b.reshape(1, SGU_W))


def _diff_lambda(lq1, lk1, lq2, lk2, lam_init):
    return (jnp.exp(jnp.sum(lq1 * lk1, axis=-1, keepdims=True))
            - jnp.exp(jnp.sum(lq2 * lk2, axis=-1, keepdims=True)) + lam_init)


def _pair_slopes(hp):
    s0 = jnp.where(hp == 0, 2.0 ** -1, jnp.where(hp == 1, 2.0 ** -3, jnp.where(hp == 2, 2.0 ** -5, 2.0 ** -7)))
    return s0.astype(F32), (s0 * 0.5).astype(F32)


def _bf16_round(x):
    bits = struct.unpack("<I", struct.pack("<f", x))[0]
    bits = (bits + 0x7FFF + ((bits >> 16) & 1)) & 0xFFFF0000
    return struct.unpack("<f", struct.pack("<I", bits))[0]


L2E_PARTS = []
_rest = LOG2E
for _ in range(3):
    L2E_PARTS.append(_bf16_round(_rest))
    _rest -= L2E_PARTS[-1]
POS_SPLIT = 64
N_POS_FEATS = 2 * len(L2E_PARTS)
EXTRA_ROWS = 16


def _attn_prompt_kernel(qt_ref, k_ref, vt_ref, pf_ref, rel_ref, lq1_ref, lk1_ref, lq2_ref, lk2_ref, g_ref, o_ref,
                        qst_ref, m_ref, acc_ref, *, tq, tk, n_hp, kv_unroll, lam_init):
    qi = pl.program_id(2)
    cols = 4 * tq
    sub = lax.broadcasted_iota(jnp.int32, (LANES, tq), 0)
    col = lax.broadcasted_iota(jnp.int32, (EXTRA_ROWS, cols), 1)
    row = lax.broadcasted_iota(jnp.int32, (EXTRA_ROWS, cols), 0)
    feat = jnp.zeros((EXTRA_ROWS, cols), F32)
    for i, part in enumerate(L2E_PARTS):
        feat = jnp.where(row == i, POS_SPLIT * part, feat)
        feat = jnp.where(row == len(L2E_PARTS) + i, part, feat)
    for h in range(n_hp):
        qt = qt_ref[0, h * LANES:(h + 1) * LANES, :]
        for c in range(4):
            qst_ref[h, 0:LANES, c * tq:(c + 1) * tq] = jnp.where((sub >= D_QK * c) & (sub < D_QK * (c + 1)), qt,
                                                                 jnp.zeros_like(qt))
        s0, s1 = _pair_slopes(pl.program_id(1) * n_hp + h)
        qst_ref[h, LANES:LANES + EXTRA_ROWS, :] = (jnp.where(col < 2 * tq, s0, s1) * feat).astype(BF16)
        qst_ref[h, LANES + EXTRA_ROWS:, :] = jnp.zeros((LANES - EXTRA_ROWS, cols), BF16)
    m_ref[...] = jnp.full(m_ref.shape, NEG_INF, F32)
    acc_ref[...] = jnp.zeros(acc_ref.shape, F32)

    def chunks(kj0, n_sub, masked, rows=tk):
        all_sts = []
        for h in range(n_hp):
            hl = slice(h * LANES, (h + 1) * LANES)
            sts = []
            for i in range(n_sub):
                start = pl.multiple_of((kj0 + i) * tk, tk)
                kaug = jnp.concatenate([k_ref[0, pl.ds(start, rows), hl], pf_ref[pl.ds(start, rows), :]], axis=1)
                st = jnp.dot(kaug, qst_ref[h], preferred_element_type=F32)
                if masked:
                    off = ((kj0 + i) * tk - qi * tq).astype(F32)
                    st = jnp.where(rel_ref[0:rows, :] + off <= 0.0, st, NEG_INF)
                sts.append(st)
            all_sts.append(sts)
        for h in range(n_hp):
            hl = slice(h * LANES, (h + 1) * LANES)
            sts = all_sts[h]
            m_old = m_ref[h]
            m_new = m_old
            for st in sts:
                m_new = jnp.maximum(m_new, jnp.max(st, axis=0, keepdims=True))
            acc = acc_ref[h] * jnp.exp2(m_old - m_new)
            for i, st in enumerate(sts):
                pt = jnp.exp2(st - m_new).astype(BF16)
                vaug = jnp.concatenate([vt_ref[0, kj0 + i, hl, 0:rows], jnp.ones((EXTRA_ROWS, rows), BF16)], axis=0)
                acc = acc + jnp.dot(vaug, pt, preferred_element_type=F32)
            acc_ref[h] = acc
            m_ref[h] = m_new

    n_full = (qi * tq) // tk

    def body(i, carry):
        chunks(kv_unroll * i, kv_unroll, False)
        return carry

    lax.fori_loop(0, n_full // kv_unroll, body, 0)
    if kv_unroll == 2:
        @pl.when(n_full % 2 == 1)
        def _():
            chunks(n_full - 1, 1, False)
    else:
        assert kv_unroll == 1

    assert tk == 2 * tq
    tile_at_chunk_start = (qi * tq) % tk == 0

    @pl.when(tile_at_chunk_start)
    def _():
        chunks(n_full, 1, True, rows=tq)

    @pl.when(jnp.logical_not(tile_at_chunk_start))
    def _():
        chunks(n_full, 1, True)

    lam = _diff_lambda(lq1_ref[...], lk1_ref[...], lq2_ref[...], lk2_ref[...], lam_init)
    for h in range(n_hp):
        ot = acc_ref[h, 0:LANES, :] / acc_ref[h, LANES:LANES + 1, :]
        at = jnp.where(sub < D_V, ot[:, 0:tq] - lam * ot[:, tq:2 * tq], ot[:, 2 * tq:3 * tq] - lam * ot[:, 3 * tq:])
        a2 = at * at
        ss0 = jnp.sum(a2[0:D_V], axis=0, keepdims=True)
        ss1 = jnp.sum(a2[D_V:], axis=0, keepdims=True)
        ms = jnp.where(sub < D_V, ss0, ss1) * (1.0 / D_V)
        an = at * lax.rsqrt(ms + EPS) * g_ref[...] * (1.0 - lam_init)
        o_ref[0, :, h * LANES:(h + 1) * LANES] = an.T.astype(BF16)


def _attn_prompt(qtb, kb, vtb, lam_params, g_subln, lam_init, *, tq=256, n_hp=4, kv_unroll=1):
    b, w, s = qtb.shape
    n_chunks, tk = vtb.shape[1], vtb.shape[3]
    hw = n_hp * LANES
    cols = 4 * tq
    g2 = jnp.concatenate([g_subln, g_subln]).reshape(LANES, 1)
    pos = jnp.arange(s, dtype=jnp.int32)
    hi, lo = (pos // POS_SPLIT).astype(BF16), (pos % POS_SPLIT).astype(BF16)
    n_parts = len(L2E_PARTS)
    pf = jnp.stack([hi] * n_parts + [lo] * n_parts + [jnp.zeros_like(hi)] * (LANES - N_POS_FEATS), axis=1)
    rel = (jnp.arange(tk, dtype=jnp.int32)[:, None] - (jnp.arange(cols, dtype=jnp.int32) % tq)[None, :]).astype(F32)
    const2 = lambda bi, hp, qi: (0, 0)
    lam_specs = [pl.BlockSpec((1, D_QK), const2)] * 4
    return pl.pallas_call(
        functools.partial(_attn_prompt_kernel, tq=tq, tk=tk, n_hp=n_hp, kv_unroll=kv_unroll, lam_init=lam_init),
        grid=(b, w // hw, s // tq),
        in_specs=[pl.BlockSpec((1, hw, tq), lambda bi, hp, qi: (bi, hp, qi)),
                  pl.BlockSpec((1, s, hw), lambda bi, hp, qi: (bi, 0, hp)),
                  pl.BlockSpec((1, n_chunks, hw, tk), lambda bi, hp, qi: (bi, 0, hp, 0)),
                  pl.BlockSpec((s, LANES), const2),
                  pl.BlockSpec((tk, cols), const2),
                  *lam_specs,
                  pl.BlockSpec((LANES, 1), const2)],
        out_specs=pl.BlockSpec((1, tq, hw), lambda bi, hp, qi: (bi, qi, hp)),
        out_shape=jax.ShapeDtypeStruct((b, s, w), BF16),
        scratch_shapes=[pltpu.VMEM((n_hp, 2 * LANES, cols), BF16),
                        pltpu.VMEM((n_hp, 1, cols), F32),
                        pltpu.VMEM((n_hp, LANES + EXTRA_ROWS, cols), F32)],
        compiler_params=_cparams("arbitrary", "arbitrary", "arbitrary"),
        name="attn_prompt",
    )(qtb, kb, vtb, pf, rel, *[p.reshape(1, D_QK) for p in lam_params], g2)


def _attn_sample_kernel(pt_ref, q_ref, kn_ref, vn_ref, lq1_ref, lk1_ref, lq2_ref, lk2_ref, g_ref, *rest,
                        pps, t_new, past, lam_init):
    k_refs = rest[:pps]
    v_refs = rest[pps:2 * pps]
    o_ref = rest[2 * pps]
    qbd_ref, m_ref, l_ref, acc_ref = rest[2 * pps + 1:]
    j = pl.program_id(1)
    n_steps = pl.num_programs(1)
    w = q_ref.shape[-1]
    n_hj = 2 * N_ATT_HEADS
    rows = n_hj * t_new
    blk = pps * PAGE_SIZE

    row_id = lax.broadcasted_iota(jnp.int32, (rows, 1), 0)
    head = row_id // (2 * t_new)
    slope = jnp.zeros((rows, 1), F32)
    for h in range(N_ATT_HEADS):
        slope = jnp.where(head == h, 2.0 ** -(h + 1), slope)
    slope = slope * LOG2E
    qpos = (past + row_id % t_new).astype(F32)

    @pl.when(j == 0)
    def _():
        q = q_ref[0]
        lane = lax.broadcasted_iota(jnp.int32, (t_new, w), 1)
        for hj in range(n_hj):
            qbd_ref[hj * t_new:(hj + 1) * t_new, :] = jnp.where(
                (lane >= D_QK * hj) & (lane < D_QK * (hj + 1)), q, jnp.zeros_like(q))
        m_ref[...] = jnp.full((rows, 1), NEG_INF, F32)
        l_ref[...] = jnp.zeros((rows, 1), F32)
        acc_ref[...] = jnp.zeros((rows, w), F32)

    def update(s, vmat, v_contract_dim):
        m_old = m_ref[...]
        m_new = jnp.maximum(m_old, jnp.max(s, axis=-1, keepdims=True))
        p = jnp.exp2(s - m_new)
        corr = jnp.exp2(m_old - m_new)
        l_ref[...] = corr * l_ref[...] + jnp.sum(p, axis=-1, keepdims=True)
        pv = lax.dot_general(p.astype(vmat.dtype), vmat, (((1,), (v_contract_dim,)), ((), ())),
                             preferred_element_type=F32)
        acc_ref[...] = corr * acc_ref[...] + pv
        m_ref[...] = m_new

    kt_all = jnp.concatenate([kr[0].astype(BF16) for kr in k_refs], axis=1)
    vt_all = jnp.concatenate([vr[0].astype(BF16) for vr in v_refs], axis=1)
    s = jnp.dot(qbd_ref[...], kt_all, preferred_element_type=F32)
    kpos = (j * blk + lax.broadcasted_iota(jnp.int32, (1, blk), 1)).astype(F32)
    update(s - slope * (qpos - kpos), vt_all, 1)

    @pl.when(j == n_steps - 1)
    def _():
        kn = kn_ref[0]
        sn = lax.dot_general(qbd_ref[...].astype(F32), kn, (((1,), (1,)), ((), ())), preferred_element_type=F32)
        kposn = (past + lax.broadcasted_iota(jnp.int32, (1, t_new), 1)).astype(F32)
        dist = qpos - kposn
        update(jnp.where(dist >= 0, sn - slope * dist, NEG_INF), vn_ref[0], 0)

        lam = _diff_lambda(lq1_ref[...], lk1_ref[...], lq2_ref[...], lk2_ref[...], lam_init)
        o = acc_ref[...] / l_ref[...]
        lane = lax.broadcasted_iota(jnp.int32, (t_new, w), 1)
        a = jnp.zeros((t_new, w), F32)
        for h in range(N_ATT_HEADS):
            o1 = o[(2 * h) * t_new:(2 * h + 1) * t_new]
            o2 = o[(2 * h + 1) * t_new:(2 * h + 2) * t_new]
            a = jnp.where((lane >= D_V * h) & (lane < D_V * (h + 1)), o1 - lam * o2, a)
        a2 = a * a
        ms = jnp.zeros((t_new, w), F32)
        for h in range(N_ATT_HEADS):
            in_h = (lane >= D_V * h) & (lane < D_V * (h + 1))
            ssh = jnp.sum(jnp.where(in_h, a2, 0.0), axis=-1, keepdims=True)
            ms = jnp.where(in_h, ssh, ms)
        o_ref[0] = (a * lax.rsqrt(ms * (1.0 / D_V) + EPS) * g_ref[...] * (1.0 - lam_init)).astype(BF16)


def _attn_sample(page_table, qb, kb, vb, cache_k, cache_v, lam_params, g_subln, lam_init, *, pps=32):
    bd, t_new, w = qb.shape
    n_pages = page_table.shape[1]
    n_pool = cache_k.shape[0]
    past = n_pages * PAGE_SIZE
    ck = jnp.transpose(cache_k, (0, 2, 3, 1)).reshape(n_pool, w, PAGE_SIZE)
    cv = jnp.transpose(cache_v, (0, 2, 3, 1)).reshape(n_pool, w, PAGE_SIZE)
    g8 = jnp.tile(g_subln, N_ATT_HEADS).reshape(1, w)
    new_spec = pl.BlockSpec((1, t_new, w), lambda b, j, pt: (b, 0, 0))
    small = lambda shape: pl.BlockSpec(shape, lambda b, j, pt: (0, 0))

    def page_spec(i):
        return pl.BlockSpec((1, w, PAGE_SIZE), lambda b, j, pt: (pt[b, j * pps + i], 0, 0))

    grid_spec = pltpu.PrefetchScalarGridSpec(
        num_scalar_prefetch=1,
        grid=(bd, n_pages // pps),
        in_specs=[new_spec, new_spec, new_spec,
                  small((1, D_QK)), small((1, D_QK)), small((1, D_QK)), small((1, D_QK)), small((1, w)),
                  *[page_spec(i) for i in range(pps)],
                  *[page_spec(i) for i in range(pps)]],
        out_specs=pl.BlockSpec((1, t_new, w), lambda b, j, pt: (b, 0, 0)),
        scratch_shapes=[pltpu.VMEM((2 * N_ATT_HEADS * t_new, w), BF16),
                        pltpu.VMEM((2 * N_ATT_HEADS * t_new, 1), F32),
                        pltpu.VMEM((2 * N_ATT_HEADS * t_new, 1), F32),
                        pltpu.VMEM((2 * N_ATT_HEADS * t_new, w), F32)],
    )
    return pl.pallas_call(
        functools.partial(_attn_sample_kernel, pps=pps, t_new=t_new, past=past, lam_init=lam_init),
        grid_spec=grid_spec,
        out_shape=jax.ShapeDtypeStruct((bd, t_new, w), BF16),
        compiler_params=_cparams("arbitrary", "arbitrary"),
        name="attn_sample",
    )(page_table, qb, kb, vb, *[p.reshape(1, D_QK) for p in lam_params], g8,
      *([ck] * pps), *([cv] * pps))


def _outproj_kernel(a_ref, u_ref, vs_ref, x_ref, ga1_ref, sc2_ref, sh2_ref, wcat_ref, sbias_ref, wout_ref,
                    gpost_ref, gpre_ref, wr_cat_ref, br_ref, tril_ref, *rest, group_dispatch):
    if group_dispatch:
        x1_ref, pay_ref, info_ref, counts_ref, sg_ref, run_ref = rest
    else:
        x1_ref, h2_ref, gates_ref, sg_ref = rest
    tm = x_ref.shape[0]
    sgu_w = u_ref.shape[1]
    lane = lax.broadcasted_iota(jnp.int32, (CHUNK, LANES), 1)
    for ci in range(tm // CHUNK):
        rs = slice(ci * CHUNK, (ci + 1) * CHUNK)
        for gp in range(sgu_w // LANES):
            cs = slice(gp * LANES, (gp + 1) * LANES)
            vp = vs_ref[rs, cs]
            zero = jnp.zeros_like(vp)
            rhs = jnp.concatenate([jnp.where(lane < SGU_GC, vp, zero), jnp.where(lane >= SGU_GC, vp, zero)], axis=0)
            mixed = jnp.dot(wcat_ref[gp], rhs, preferred_element_type=F32) + sbias_ref[:, cs]
            sg_ref[rs, cs] = (u_ref[rs, cs].astype(F32) * mixed).astype(BF16)
    att_w = a_ref.shape[1]
    mix = (jnp.dot(a_ref[...], wout_ref[0:att_w, :], preferred_element_type=F32)
           + jnp.dot(sg_ref[...], wout_ref[att_w:, :], preferred_element_type=F32))
    x1 = x_ref[...] + ga1_ref[0] * _rms(mix, gpost_ref[...])
    x1_ref[...] = x1
    h2 = _rms(x1, gpre_ref[...]) * (1.0 + sc2_ref[0]) + sh2_ref[0]
    h2_hi = h2.astype(BF16)
    h2_lo = (h2 - h2_hi.astype(F32)).astype(BF16)
    hh = jnp.dot(h2_hi, wr_cat_ref[...], preferred_element_type=F32)
    logits = (hh[:, :LANES] + hh[:, LANES:]
              + jnp.dot(h2_lo, wr_cat_ref[:, :LANES], preferred_element_type=F32)) + br_ref[...]
    ln = lax.broadcasted_iota(jnp.int32, (tm, LANES), 1)
    big = jnp.int32(LANES)
    is_g = (ln >= N_EXPERTS) & (ln < N_EXPERTS + N_EXPERT_GROUPS)
    gl = jnp.where(is_g, logits, NEG_INF)
    gmax = jnp.max(gl, axis=-1, keepdims=True)
    g_idx = jnp.min(jnp.where(gl == gmax, ln, big), axis=-1, keepdims=True) - N_EXPERTS
    g_w = 1.0 / jnp.sum(jnp.exp(gl - gmax), axis=-1, keepdims=True)
    in_grp = (ln >= g_idx * EXPERTS_PER_GROUP) & (ln < (g_idx + 1) * EXPERTS_PER_GROUP)
    el = jnp.where(in_grp, logits, NEG_INF)
    emax = jnp.max(el, axis=-1, keepdims=True)
    ez = jnp.exp(el - emax)
    prob = ez / jnp.sum(ez, axis=-1, keepdims=True)
    p1 = jnp.max(prob, axis=-1, keepdims=True)
    i1 = jnp.min(jnp.where((prob == p1) & in_grp, ln, big), axis=-1, keepdims=True)
    rest = jnp.where(in_grp & (ln != i1), prob, -1.0)
    p2 = jnp.max(rest, axis=-1, keepdims=True)
    i2 = jnp.min(jnp.where(rest == p2, ln, big), axis=-1, keepdims=True)
    den = p1 + p2
    gates = jnp.where(ln == i1, p1 / den * g_w, 0.0) + jnp.where(ln == i2, p2 / den * g_w, 0.0)
    if not group_dispatch:
        h2_ref[...] = h2_hi
        gates_ref[...] = gates[:, :N_EXPERTS]
        return

    bits = pltpu.bitcast(h2_hi.astype(F32), jnp.uint32)
    half = PAYLOAD_H2_SEGS * LANES
    for s in range(PAYLOAD_H2_SEGS):
        lo = bits[:, s * LANES:(s + 1) * LANES]
        hi = bits[:, half + s * LANES:half + (s + 1) * LANES]
        pay_ref[s] = hi | lax.shift_right_logical(lo, jnp.uint32(16))
    pay_ref[PAYLOAD_H2_SEGS] = pltpu.bitcast(gates, jnp.uint32)

    @pl.when(pl.program_id(0) == 0)
    def _():
        run_ref[...] = jnp.zeros_like(run_ref)

    onehot = jnp.where(ln == g_idx, 1.0, 0.0).astype(BF16)
    csum = jnp.dot(tril_ref[...], onehot, preferred_element_type=F32) + run_ref[...]
    rank = jnp.sum(jnp.where(ln == g_idx, csum, 0.0), axis=-1, keepdims=True) - 1.0
    run_ref[...] = csum[tm - 1:tm, :]
    counts_ref[...] = csum[tm - 1:tm, :]
    info = jnp.where(ln == 0, g_idx, jnp.where(ln == 1, rank.astype(jnp.int32), 0))
    info_ref[...] = info[:, :INFO_W]


def _outproj(a, u, vsb, x, ga1, sc2, sh2, wcat, sbias, w_out_bf, g_post, g_pre, wr_cat, br,
             *, tm, tiles_per_mod, group_dispatch):
    n, d = x.shape
    att_w = a.shape[1]
    sgu_w = u.shape[1]
    r = ga1.shape[1]
    row = lambda i: (i, 0)
    modmap = lambda i: (i // tiles_per_mod, 0, 0)
    const = lambda i: (0, 0)
    tril = jnp.tril(jnp.ones((tm, tm), BF16))
    if group_dispatch:
        out_specs = [pl.BlockSpec((tm, d), row),
                     pl.BlockSpec((PAYLOAD_SEGS, tm, LANES), lambda i: (0, i, 0)),
                     pl.BlockSpec((tm, INFO_W), row),
                     pl.BlockSpec((1, LANES), const)]
        out_shape = [jax.ShapeDtypeStruct((n, d), F32),
                     jax.ShapeDtypeStruct((PAYLOAD_SEGS, n, LANES), jnp.uint32),
                     jax.ShapeDtypeStruct((n, INFO_W), jnp.int32),
                     jax.ShapeDtypeStruct((1, LANES), F32)]
        scratch = [pltpu.VMEM((tm, sgu_w), BF16), pltpu.VMEM((1, LANES), F32)]
    else:
        out_specs = [pl.BlockSpec((tm, d), row),
                     pl.BlockSpec((tm, d), row),
                     pl.BlockSpec((tm, N_EXPERTS), row)]
        out_shape = [jax.ShapeDtypeStruct((n, d), F32),
                     jax.ShapeDtypeStruct((n, d), BF16),
                     jax.ShapeDtypeStruct((n, N_EXPERTS), F32)]
        scratch = [pltpu.VMEM((tm, sgu_w), BF16)]
    return pl.pallas_call(
        functools.partial(_outproj_kernel, group_dispatch=group_dispatch),
        grid=(n // tm,),
        in_specs=[pl.BlockSpec((tm, att_w), row),
                  pl.BlockSpec((tm, sgu_w), row),
                  pl.BlockSpec((tm, sgu_w), row),
                  pl.BlockSpec((tm, d), row),
                  pl.BlockSpec((1, r, d), modmap),
                  pl.BlockSpec((1, r, d), modmap),
                  pl.BlockSpec((1, r, d), modmap),
                  pl.BlockSpec(wcat.shape, lambda i: (0, 0, 0)),
                  pl.BlockSpec(sbias.shape, const),
                  pl.BlockSpec(w_out_bf.shape, const),
                  pl.BlockSpec((1, d), const),
                  pl.BlockSpec((1, d), const),
                  pl.BlockSpec(wr_cat.shape, const),
                  pl.BlockSpec((1, LANES), const),
                  pl.BlockSpec((tm, tm), const)],
        out_specs=out_specs,
        out_shape=out_shape,
        scratch_shapes=scratch,
        compiler_params=_cparams("arbitrary"),
        name="outproj",
    )(a, u, vsb, x, ga1, sc2, sh2, wcat, sbias, w_out_bf, g_post.reshape(1, d), g_pre.reshape(1, d),
      wr_cat, br, tril)


def _moe_kernel(h_ref, gates_ref, x1_ref, ga2_ref, gpost_ref, wg_ref, wu_ref, wd_ref, o_ref, acc_ref):
    e = pl.program_id(1)

    @pl.when(e == 0)
    def _():
        acc_ref[...] = jnp.zeros_like(acc_ref)

    h = h_ref[...]
    gate = jnp.dot(h, wg_ref[0], preferred_element_type=F32)
    up = jnp.dot(h, wu_ref[0], preferred_element_type=F32)
    hdn = (gate * jax.nn.sigmoid(gate) * up).astype(BF16)
    y = jnp.dot(hdn, wd_ref[0], preferred_element_type=F32)
    ln = lax.broadcasted_iota(jnp.int32, gates_ref.shape, 1)
    ge = jnp.sum(jnp.where(ln == e, gates_ref[...], 0.0), axis=-1, keepdims=True)
    acc_ref[...] += ge * y

    @pl.when(e == pl.num_programs(1) - 1)
    def _():
        o_ref[...] = x1_ref[...] + ga2_ref[0] * _rms(acc_ref[...], gpost_ref[...])


def _moe(h2, gates, x1, ga2, g_post, wg_bf, wu_bf, wd_bf, *, tm, tiles_per_mod):
    n, d = x1.shape
    n_e, _, de = wg_bf.shape
    r = ga2.shape[1]
    row = lambda i, e: (i, 0)
    return pl.pallas_call(
        _moe_kernel,
        grid=(n // tm, n_e),
        in_specs=[pl.BlockSpec((tm, d), row),
                  pl.BlockSpec((tm, n_e), row),
                  pl.BlockSpec((tm, d), row),
                  pl.BlockSpec((1, r, d), lambda i, e: (i // tiles_per_mod, 0, 0)),
                  pl.BlockSpec((1, d), lambda i, e: (0, 0)),
                  pl.BlockSpec((1, d, de), lambda i, e: (e, 0, 0)),
                  pl.BlockSpec((1, d, de), lambda i, e: (e, 0, 0)),
                  pl.BlockSpec((1, de, d), lambda i, e: (e, 0, 0))],
        out_specs=pl.BlockSpec((tm, d), row),
        out_shape=jax.ShapeDtypeStruct((n, d), F32),
        scratch_shapes=[pltpu.VMEM((tm, d), F32)],
        compiler_params=_cparams("arbitrary", "arbitrary"),
        name="moe",
    )(h2, gates, x1, ga2, g_post.reshape(1, d), wg_bf, wu_bf, wd_bf)


def _sc_mesh():
    return plsc.VectorSubcoreMesh(core_axis_name="core", subcore_axis_name="subcore")


def _sc_scatter_rows(x, idx, n_out_rows):
    n_rows, width = x.shape

    @pl.kernel(out_type=jax.ShapeDtypeStruct((n_out_rows, width), x.dtype), mesh=_sc_mesh(), scratch_types=[])
    def scatter(x_hbm, i_hbm, o_hbm):
        def body(x_vmem, i_vmem):
            pltpu.sync_copy(x_vmem, o_hbm.at[i_vmem.at[0]])

        pltpu.emit_pipeline(
            body,
            grid=(n_rows // SC_WINDOW,),
            in_specs=[pl.BlockSpec((SC_WINDOW, width), lambda i: (i, 0)),
                      pl.BlockSpec((1, SC_WINDOW), lambda i: (0, i))],
            out_specs=[],
            core_axis_name=("core", "subcore"),
            dimension_semantics=(pltpu.PARALLEL,),
        )(x_hbm, i_hbm)

    return scatter(x, idx.reshape(1, n_rows))


def _sc_gather_rows(x, idx):
    n_rows = idx.shape[0]
    width = x.shape[1]

    @pl.kernel(out_type=jax.ShapeDtypeStruct((n_rows, width), x.dtype), mesh=_sc_mesh(), scratch_types=[])
    def gather(x_hbm, i_hbm, o_hbm):
        def body(i_vmem, o_vmem):
            pltpu.sync_copy(x_hbm.at[i_vmem.at[0]], o_vmem)

        pltpu.emit_pipeline(
            body,
            grid=(n_rows // SC_WINDOW,),
            in_specs=[pl.BlockSpec((1, SC_WINDOW), lambda i: (0, i))],
            out_specs=[pl.BlockSpec((SC_WINDOW, width), lambda i: (i, 0))],
            core_axis_name=("core", "subcore"),
            dimension_semantics=(pltpu.PARALLEL,),
        )(i_hbm, o_hbm)

    return gather(x, idx.reshape(1, n_rows))


def _group_moe_kernel(tg_ref, nused_ref, xs_ref, gpost_ref, wg_ref, wu_ref, wd_ref, rs_ref):
    t = pl.program_id(0)
    tm = xs_ref.shape[1]

    @pl.when(t >= nused_ref[0])
    def _():
        rs_ref[...] = jnp.zeros_like(rs_ref)

    @pl.when(t < nused_ref[0])
    def _():
        words = [xs_ref[s] for s in range(PAYLOAD_H2_SEGS)]
        lo = [pltpu.bitcast(lax.shift_left(w, jnp.uint32(16)), F32) for w in words]
        hi = [pltpu.bitcast(w & jnp.uint32(0xFFFF0000), F32) for w in words]
        x = jnp.concatenate(lo + hi, axis=1).astype(BF16)
        gates = pltpu.bitcast(xs_ref[PAYLOAD_H2_SEGS], F32)
        ln = lax.broadcasted_iota(jnp.int32, (tm, LANES), 1)
        first = tg_ref[t] * EXPERTS_PER_GROUP
        y = jnp.zeros((tm, x.shape[1]), F32)
        for el in range(EXPERTS_PER_GROUP):
            ge = jnp.sum(jnp.where(ln == first + el, gates, 0.0), axis=-1, keepdims=True)
            gate = jnp.dot(x, wg_ref[el], preferred_element_type=F32)
            up = jnp.dot(x, wu_ref[el], preferred_element_type=F32)
            hdn = (gate * jax.nn.sigmoid(gate) * up).astype(BF16)
            y = y + ge * jnp.dot(hdn, wd_ref[el], preferred_element_type=F32)
        r = _rms(y, gpost_ref[...])
        for s in range(rs_ref.shape[0]):
            rs_ref[s] = r[:, s * LANES:(s + 1) * LANES]


def _group_moe(xs, tile_group, n_used, g_post, wg_bf, wu_bf, wd_bf, *, tm):
    _, p_rows, _ = xs.shape
    n_e, d, de = wg_bf.shape
    g = EXPERTS_PER_GROUP
    grid_spec = pltpu.PrefetchScalarGridSpec(
        num_scalar_prefetch=2,
        grid=(p_rows // tm,),
        in_specs=[pl.BlockSpec((PAYLOAD_SEGS, tm, LANES), lambda t, tg, nu: (0, t, 0)),
                  pl.BlockSpec((1, d), lambda t, tg, nu: (0, 0)),
                  pl.BlockSpec((g, d, de), lambda t, tg, nu: (tg[t], 0, 0)),
                  pl.BlockSpec((g, d, de), lambda t, tg, nu: (tg[t], 0, 0)),
                  pl.BlockSpec((g, de, d), lambda t, tg, nu: (tg[t], 0, 0))],
        out_specs=pl.BlockSpec((d // LANES, tm, LANES), lambda t, tg, nu: (0, t, 0)),
    )
    return pl.pallas_call(
        _group_moe_kernel,
        grid_spec=grid_spec,
        out_shape=jax.ShapeDtypeStruct((d // LANES, p_rows, LANES), F32),
        compiler_params=_cparams("arbitrary"),
        name="group_moe",
    )(tile_group, n_used, xs, g_post.reshape(1, d), wg_bf, wu_bf, wd_bf)


def _residual_kernel(x1_ref, ga2_ref, r_ref, o_ref):
    r = jnp.concatenate([r_ref[s] for s in range(r_ref.shape[0])], axis=1)
    o_ref[...] = x1_ref[...] + ga2_ref[0] * r


def _residual(x1, ga2, r_segs, *, tm, tiles_per_mod):
    n, d = x1.shape
    return pl.pallas_call(
        _residual_kernel,
        grid=(n // tm,),
        in_specs=[pl.BlockSpec((tm, d), lambda i: (i, 0)),
                  pl.BlockSpec((1, ga2.shape[1], d), lambda i: (i // tiles_per_mod, 0, 0)),
                  pl.BlockSpec((d // LANES, tm, LANES), lambda i: (0, i, 0))],
        out_specs=pl.BlockSpec((tm, d), lambda i: (i, 0)),
        out_shape=jax.ShapeDtypeStruct((n, d), F32),
        compiler_params=_cparams("arbitrary"),
        name="residual",
    )(x1, ga2, r_segs)


def _group_plan(info, counts, n_tokens, tm):
    n_tiles = n_tokens // tm + N_EXPERT_GROUPS
    cnt = counts[0, :N_EXPERT_GROUPS].astype(jnp.int32)
    padded = (cnt + tm - 1) // tm * tm
    ends = jnp.cumsum(padded)
    starts = ends - padded
    gid, rank = info[:, 0], info[:, 1]
    onehot = (gid[:, None] == jnp.arange(N_EXPERT_GROUPS, dtype=jnp.int32)[None, :]).astype(jnp.int32)
    pos = jnp.sum(onehot * starts[None, :], axis=1) + rank
    tile_ends = ends // tm
    t = jnp.arange(n_tiles, dtype=jnp.int32)
    tile_group = jnp.minimum(jnp.sum((t[:, None] >= tile_ends[None, :]).astype(jnp.int32), axis=1),
                             N_EXPERT_GROUPS - 1)
    return pos, tile_group, tile_ends[-1:], n_tiles * tm


def _sgu_weights(ws, bs, chunk_len):
    causal = jnp.tril(jnp.ones((chunk_len, chunk_len), ws.dtype))
    wm = ws[:, :chunk_len, :chunk_len] * causal
    reps = CHUNK // chunk_len
    if reps > 1:
        eye = jnp.eye(reps, dtype=ws.dtype)
        wm = jnp.einsum("ab,gts->gatbs", eye, wm).reshape(N_SGU_GROUPS, CHUNK, CHUNK)
    wcat = jnp.concatenate([wm[0::2], wm[1::2]], axis=2).astype(BF16)
    bt = jnp.tile(bs[:, :chunk_len].T, (reps, 1))
    sbias = jnp.repeat(bt, SGU_GC, axis=1)
    return wcat, sbias


def _split_mod(mod):
    return [m[:, None, :] for m in jnp.split(mod, 6, axis=-1)]


def kernel(x_prompt, x_sample, cache_k, cache_v, page_table, c_prompt, c_sample, w_ada, b_ada, g_pre_mix, g_post_mix, g_pre_ffn, g_post_ffn, w_in, lam_q1, lam_k1, lam_q2, lam_k2, g_subln, sgu_ln_g, sgu_ln_b, sgu_ws, sgu_bs, w_out, w_rg, b_rg, w_re, b_re, w_gate, w_up, w_down):
    depth = w_in.shape[0]
    assert depth == 1
    l = 0
    lam_init = 0.8 - 0.6 * math.exp(-0.3 * l)
    bp, sp, d = x_prompt.shape
    bs_, ts, _ = x_sample.shape
    n_s = bs_ * ts

    mod = _ada(jnp.concatenate([c_prompt, c_sample], axis=0), w_ada[l], b_ada[l])
    sh1p, sc1p, ga1p, sh2p, sc2p, ga2p = _split_mod(mod[:bp])
    rep = lambda m: jnp.repeat(m, ts, axis=0).reshape(1, n_s, d)
    sh1s, sc1s, ga1s, sh2s, sc2s, ga2s = [rep(m) for m in jnp.split(mod[bp:], 6, axis=-1)]

    w_in_bf = w_in[l].astype(BF16)
    wqv_t_bf = jnp.concatenate([w_in_bf[:, 0:ATT_W], w_in_bf[:, 2 * ATT_W:3 * ATT_W]], axis=1).T
    w_out_bf = w_out[l].astype(BF16)
    wg_bf = w_gate[l].astype(BF16)
    wu_bf = w_up[l].astype(BF16)
    wd_bf = w_down[l].astype(BF16)
    wr = jnp.concatenate([w_re[l], w_rg[l]], axis=1)
    wr = jnp.pad(wr, ((0, 0), (0, LANES - wr.shape[1])))
    wr_hi = wr.astype(BF16)
    wr_cat = jnp.concatenate([wr_hi, (wr - wr_hi.astype(F32)).astype(BF16)], axis=1)
    br = jnp.pad(jnp.concatenate([b_re[l], b_rg[l]]), (0, LANES - N_EXPERTS - N_EXPERT_GROUPS)).reshape(1, LANES)
    lam_params = (lam_q1[l], lam_k1[l], lam_q2[l], lam_k2[l])

    tm_p = 512
    xp = x_prompt.reshape(bp * sp, d)
    ktp, vtp, qtb, kb, vtb, u, vsb, vs_last = _inproj(
        xp, sc1p, sh1p, g_pre_mix[l], w_in_bf, wqv_t_bf, sgu_ln_g[l], sgu_ln_b[l],
        tm=tm_p, tiles_per_seq=sp // tm_p, last_rows=CHUNK, transposed_qv=True)
    w3 = ATT_W
    a = _attn_prompt(qtb, kb.reshape(bp, sp, w3), vtb, lam_params, g_subln[l], lam_init)
    kp = jnp.transpose(ktp.reshape(bp, N_ATT_HEADS, 2 * D_QK, sp), (0, 3, 1, 2))
    vp = jnp.transpose(vtp.reshape(bp, N_ATT_HEADS, D_V, sp), (0, 3, 1, 2))
    wcat_p, sbias_p = _sgu_weights(sgu_ws[l], sgu_bs[l], CHUNK)
    n_p = bp * sp
    x1, payload, info, counts = _outproj(
        a.reshape(n_p, w3), u, vsb, xp, ga1p, sc2p, sh2p, wcat_p, sbias_p, w_out_bf,
        g_post_mix[l], g_pre_ffn[l], wr_cat, br, tm=tm_p, tiles_per_mod=sp // tm_p, group_dispatch=True)
    tm_moe = 512
    pos, tile_group, n_used, p_rows = _group_plan(info, counts, n_p, tm_moe)
    seg_p = lambda k: (jnp.arange(k, dtype=jnp.int32)[:, None] * p_rows + pos[None, :]).reshape(-1)
    xs = _sc_scatter_rows(payload.reshape(PAYLOAD_SEGS * n_p, LANES), seg_p(PAYLOAD_SEGS), PAYLOAD_SEGS * p_rows)
    rs = _group_moe(xs.reshape(PAYLOAD_SEGS, p_rows, LANES), tile_group, n_used, g_post_ffn[l],
                    wg_bf, wu_bf, wd_bf, tm=tm_moe)
    n_seg = d // LANES
    r_tok = _sc_gather_rows(rs.reshape(n_seg * p_rows, LANES), seg_p(n_seg))
    yp = _residual(x1, ga2p, r_tok.reshape(n_seg, n_p, LANES), tm=tm_p, tiles_per_mod=sp // tm_p)

    xs = x_sample.reshape(n_s, d)
    ks, vs_, qbs, us, vsbs, vs_last_s = _inproj(
        xs, sc1s, sh1s, g_pre_mix[l], w_in_bf, wqv_t_bf, sgu_ln_g[l], sgu_ln_b[l],
        tm=n_s, tiles_per_seq=1, last_rows=n_s, transposed_qv=False)
    a_s = _attn_sample(page_table, qbs.reshape(bs_, ts, w3), ks.reshape(bs_, ts, w3), vs_.reshape(bs_, ts, w3),
                       cache_k[l], cache_v[l], lam_params, g_subln[l], lam_init)
    wcat_s, sbias_s = _sgu_weights(sgu_ws[l], sgu_bs[l], ts)
    x1s, h2s, gates_s = _outproj(a_s.reshape(n_s, w3), us, vsbs, xs, ga1s, sc2s, sh2s, wcat_s, sbias_s, w_out_bf,
                                 g_post_mix[l], g_pre_ffn[l], wr_cat, br, tm=n_s, tiles_per_mod=1,
                                 group_dispatch=False)
    ys = _moe(h2s, gates_s, x1s, ga2s, g_post_ffn[l], wg_bf, wu_bf, wd_bf, tm=n_s, tiles_per_mod=1)

    return (yp.reshape(bp, sp, d), ys.reshape(bs_, ts, d),
            kp[None], vp[None],
            vs_last.reshape(1, bp, CHUNK, -1),
            ks.reshape(1, bs_, ts, N_ATT_HEADS, 2 * D_QK), vs_.reshape(1, bs_, ts, N_ATT_HEADS, D_V),
            vs_last_s.reshape(1, bs_, ts, -1))
```

```python
import functools
import math
import struct

import jax
import jax.numpy as jnp
from jax import lax
from jax.experimental import pallas as pl
from jax.experimental.pallas import tpu as pltpu
from jax.experimental.pallas import tpu_sc as plsc

F32 = jnp.float32
BF16 = jnp.bfloat16

EPS = 1e-6
N_ATT_HEADS = 8
D_QK = 32
D_V = 64
ATT_W = N_ATT_HEADS * D_V
N_SGU_GROUPS = 8
SGU_GC = 64
SGU_W = N_SGU_GROUPS * SGU_GC
CHUNK = 128
N_EXPERT_GROUPS = 4
EXPERTS_PER_GROUP = 4
N_EXPERTS = 16
PAGE_SIZE = 128
LANES = 128
VMEM_LIMIT = 56 * 1024 * 1024
NEG_INF = float("-inf")
LOG2E = math.log2(math.e)
PAYLOAD_H2_SEGS = 4
PAYLOAD_SEGS = PAYLOAD_H2_SEGS + 1
INFO_W = 8
SC_WINDOW = 128


def _cparams(*sem):
    return pltpu.CompilerParams(dimension_semantics=sem, vmem_limit_bytes=VMEM_LIMIT)


def _rms(x, g):
    return x * lax.rsqrt(jnp.mean(x * x, axis=-1, keepdims=True) + EPS) * g


def _ada_kernel(c_ref, w_ref, b_ref, o_ref):
    c = c_ref[...]
    s = c * jax.nn.sigmoid(c)
    o_ref[...] = jnp.dot(s.astype(BF16), w_ref[...].astype(BF16), preferred_element_type=F32) + b_ref[...]


def _ada(c, w_ada, b_ada):
    n, d = c.shape
    nout = w_ada.shape[1]
    tn = d
    return pl.pallas_call(
        _ada_kernel,
        grid=(nout // tn,),
        in_specs=[pl.BlockSpec((n, d), lambda j: (0, 0)),
                  pl.BlockSpec((d, tn), lambda j: (0, j)),
                  pl.BlockSpec((1, tn), lambda j: (0, j))],
        out_specs=pl.BlockSpec((n, tn), lambda j: (0, j)),
        out_shape=jax.ShapeDtypeStruct((n, nout), F32),
        compiler_params=_cparams("arbitrary"),
        name="ada",
    )(c, w_ada, b_ada.reshape(1, nout))


Q_SCALE = D_QK ** -0.5 * LOG2E


def _inproj_kernel(x_ref, sc_ref, sh_ref, g_ref, w_ref, wqv_t_ref, lng_ref, lnb_ref, *out_refs,
                   last_rows, transposed_qv):
    x = x_ref[...]
    h = (_rms(x, g_ref[...]) * (1.0 + sc_ref[0]) + sh_ref[0]).astype(BF16)

    def proj(lo):
        return jnp.dot(h, w_ref[:, lo:lo + ATT_W], preferred_element_type=F32)

    def proj_t(lo):
        return lax.dot_general(wqv_t_ref[lo:lo + ATT_W, :], h, (((1,), (1,)), ((), ())),
                               preferred_element_type=F32)

    zk = proj(ATT_W)
    if transposed_qv:
        kt_ref, vt_ref, qtb_ref, kb_ref, vtb_ref, u_ref, vsb_ref, vsl_ref = out_refs
        kt_ref[0] = zk.T
        kb_ref[...] = zk.astype(BF16)
        qtb_ref[0] = (proj_t(0) * Q_SCALE).astype(BF16)
        zvt = proj_t(ATT_W)
        vt_ref[0] = zvt
        vtb_ref[0, 0] = zvt.astype(BF16)
    else:
        k_ref, v_ref, qb_ref, u_ref, vsb_ref, vsl_ref = out_refs
        k_ref[...] = zk
        qb_ref[...] = (proj(0) * Q_SCALE).astype(BF16)
        v_ref[...] = proj(2 * ATT_W)
    u_ref[...] = jax.nn.gelu(proj(3 * ATT_W)).astype(BF16)
    gs = jax.nn.gelu(proj(3 * ATT_W + SGU_W))
    mu = jnp.mean(gs, axis=-1, keepdims=True)
    xc = gs - mu
    vs = xc * lax.rsqrt(jnp.mean(xc * xc, axis=-1, keepdims=True) + EPS) * lng_ref[...] + lnb_ref[...]
    vsb_ref[...] = vs.astype(BF16)
    tm = vs.shape[0]
    vsl_ref[0] = vs[tm - last_rows:, :]


def _inproj(x, sc, sh, g, w_in_bf, wqv_t_bf, ln_g, ln_b, *, tm, tiles_per_seq, last_rows, transposed_qv):
    n, d = x.shape
    in_w = w_in_bf.shape[1]
    r = sc.shape[1]
    n_tiles = n // tm
    n_seq = n_tiles // tiles_per_seq
    s_len = tiles_per_seq * tm
    row = lambda i: (i, 0)
    modmap = lambda i: (i // tiles_per_seq, 0, 0)
    const = lambda i: (0, 0)
    col_t = lambda i: (i // tiles_per_seq, 0, i % tiles_per_seq)
    row_spec = pl.BlockSpec((tm, ATT_W), row)
    tail = [(pl.BlockSpec((tm, SGU_W), row), jax.ShapeDtypeStruct((n, SGU_W), BF16)),
            (pl.BlockSpec((tm, SGU_W), row), jax.ShapeDtypeStruct((n, SGU_W), BF16)),
            (pl.BlockSpec((1, last_rows, SGU_W), lambda i: (i // tiles_per_seq, 0, 0)),
             jax.ShapeDtypeStruct((n_seq, last_rows, SGU_W), F32))]
    if transposed_qv:
        head = [(pl.BlockSpec((1, ATT_W, tm), col_t), jax.ShapeDtypeStruct((n_seq, ATT_W, s_len), F32)),
                (pl.BlockSpec((1, ATT_W, tm), col_t), jax.ShapeDtypeStruct((n_seq, ATT_W, s_len), F32)),
                (pl.BlockSpec((1, ATT_W, tm), col_t), jax.ShapeDtypeStruct((n_seq, ATT_W, s_len), BF16)),
                (row_spec, jax.ShapeDtypeStruct((n, ATT_W), BF16)),
                (pl.BlockSpec((1, 1, ATT_W, tm), lambda i: (i // tiles_per_seq, i % tiles_per_seq, 0, 0)),
                 jax.ShapeDtypeStruct((n_seq, tiles_per_seq, ATT_W, tm), BF16))]
    else:
        head = [(row_spec, jax.ShapeDtypeStruct((n, ATT_W), F32)),
                (row_spec, jax.ShapeDtypeStruct((n, ATT_W), F32)),
                (row_spec, jax.ShapeDtypeStruct((n, ATT_W), BF16))]
    specs, shapes = zip(*(head + tail))
    return pl.pallas_call(
        functools.partial(_inproj_kernel, last_rows=last_rows, transposed_qv=transposed_qv),
        grid=(n_tiles,),
        in_specs=[pl.BlockSpec((tm, d), row),
                  pl.BlockSpec((1, r, d), modmap),
                  pl.BlockSpec((1, r, d), modmap),
                  pl.BlockSpec((1, d), const),
                  pl.BlockSpec((d, in_w), const),
                  pl.BlockSpec(wqv_t_bf.shape, const),
                  pl.BlockSpec((1, SGU_W), const),
                  pl.BlockSpec((1, SGU_W), const)],
        out_specs=list(specs),
        out_shape=list(shapes),
        compiler_params=_cparams("arbitrary"),
        name="inproj",
    )(x, sc, sh, g.reshape(1, d), w_in_bf, wqv_t_bf, ln_g.reshape(1, SGU_W), ln_b.reshape(1, SGU_W))


def _diff_lambda(lq1, lk1, lq2, lk2, lam_init):
    return (jnp.exp(jnp.sum(lq1 * lk1, axis=-1, keepdims=True))
            - jnp.exp(jnp.sum(lq2 * lk2, axis=-1, keepdims=True)) + lam_init)


def _pair_slopes(hp):
    s0 = jnp.where(hp == 0, 2.0 ** -1, jnp.where(hp == 1, 2.0 ** -3, jnp.where(hp == 2, 2.0 ** -5, 2.0 ** -7)))
    return s0.astype(F32), (s0 * 0.5).astype(F32)


def _bf16_round(x):
    bits = struct.unpack("<I", struct.pack("<f", x))[0]
    bits = (bits + 0x7FFF + ((bits >> 16) & 1)) & 0xFFFF0000
    return struct.unpack("<f", struct.pack("<I", bits))[0]


L2E_PARTS = []
_rest = LOG2E
for _ in range(3):
    L2E_PARTS.append(_bf16_round(_rest))
    _rest -= L2E_PARTS[-1]
POS_SPLIT = 64
N_POS_FEATS = 2 * len(L2E_PARTS)
EXTRA_ROWS = 16


def _attn_kernel(pt_ref, qt_ref, k_ref, vt_ref, pf_ref, rel_ref, lq1_ref, lk1_ref, lq2_ref, lk2_ref, g_ref,
                 qs_ref, kn_ref, vn_ref, g8_ref, ck_hbm, cv_hbm, o_ref, os_ref,
                 qst_ref, m_ref, acc_ref, qbd_ref, ms_ref, ls_ref, accs_ref, kbuf_ref, vbuf_ref, sem_ref,
                 *, tq, tk, n_hp, kv_unroll, lam_init, pps, subs_per_step, past):
    qi = pl.program_id(2)
    cols = 4 * tq

    t_new, w = qs_ref.shape[1], qs_ref.shape[2]
    n_steps = pl.num_programs(0) * pl.num_programs(2)
    step = pl.program_id(0) * pl.num_programs(2) + qi
    subs_per_seq = past // (pps * PAGE_SIZE)
    steps_per_seq = subs_per_seq // subs_per_step
    seq = step // steps_per_seq
    first_sub = (step % steps_per_seq) * subs_per_step
    n_hj = 2 * N_ATT_HEADS
    srows = n_hj * t_new
    blk = pps * PAGE_SIZE
    srow_id = lax.broadcasted_iota(jnp.int32, (srows, 1), 0)
    shead = srow_id // (2 * t_new)
    sslope = jnp.zeros((srows, 1), F32)
    for h in range(N_ATT_HEADS):
        sslope = jnp.where(shead == h, 2.0 ** -(h + 1), sslope)
    sslope = sslope * LOG2E
    sqpos = (past + srow_id % t_new).astype(F32)

    def page_copies(seq_i, sub_i, slot):
        copies = []
        for i in range(pps):
            page = pt_ref[seq_i, sub_i * pps + i]
            copies.append(pltpu.make_async_copy(ck_hbm.at[page], kbuf_ref.at[slot, i], sem_ref.at[0, slot]))
            copies.append(pltpu.make_async_copy(cv_hbm.at[page], vbuf_ref.at[slot, i], sem_ref.at[1, slot]))
        return copies

    def sample_update(s, vmat, v_contract_dim):
        m_old = ms_ref[...]
        m_new = jnp.maximum(m_old, jnp.max(s, axis=-1, keepdims=True))
        p = jnp.exp2(s - m_new)
        corr = jnp.exp2(m_old - m_new)
        ls_ref[...] = corr * ls_ref[...] + jnp.sum(p, axis=-1, keepdims=True)
        pv = lax.dot_general(p.astype(vmat.dtype), vmat, (((1,), (v_contract_dim,)), ((), ())),
                             preferred_element_type=F32)
        accs_ref[...] = corr * accs_ref[...] + pv
        ms_ref[...] = m_new

    def sample_sub(j):
        slot = j % 2
        if j + 1 < subs_per_step:
            for c in page_copies(seq, first_sub + j + 1, 1 - slot):
                c.start()
        else:
            @pl.when(step + 1 < n_steps)
            def _():
                nxt = step + 1
                for c in page_copies(nxt // steps_per_seq, (nxt % steps_per_seq) * subs_per_step, 1 - slot):
                    c.start()
        for c in page_copies(seq, first_sub + j, slot):
            c.wait()
        kt_all = jnp.concatenate([kbuf_ref[slot, i].astype(BF16) for i in range(pps)], axis=1)
        vt_all = jnp.concatenate([vbuf_ref[slot, i].astype(BF16) for i in range(pps)], axis=1)
        s = jnp.dot(qbd_ref[...], kt_all, preferred_element_type=F32)
        kpos = ((first_sub + j) * blk + lax.broadcasted_iota(jnp.int32, (1, blk), 1)).astype(F32)
        sample_update(s - sslope * (sqpos - kpos), vt_all, 1)

    assert subs_per_step % 2 == 0

    @pl.when(step == 0)
    def _():
        for c in page_copies(0, 0, 0):
            c.start()

    @pl.when(step % steps_per_seq == 0)
    def _():
        q = qs_ref[seq]
        lane = lax.broadcasted_iota(jnp.int32, (t_new, w), 1)
        for hj in range(n_hj):
            qbd_ref[hj * t_new:(hj + 1) * t_new, :] = jnp.where(
                (lane >= D_QK * hj) & (lane < D_QK * (hj + 1)), q, jnp.zeros_like(q))
        ms_ref[...] = jnp.full((srows, 1), NEG_INF, F32)
        ls_ref[...] = jnp.zeros((srows, 1), F32)
        accs_ref[...] = jnp.zeros((srows, w), F32)

    def sample_finish():
        kn = kn_ref[seq]
        sn = lax.dot_general(qbd_ref[...].astype(F32), kn, (((1,), (1,)), ((), ())), preferred_element_type=F32)
        kposn = (past + lax.broadcasted_iota(jnp.int32, (1, t_new), 1)).astype(F32)
        dist = sqpos - kposn
        sample_update(jnp.where(dist >= 0, sn - sslope * dist, NEG_INF), vn_ref[seq], 0)
        lam_s = _diff_lambda(lq1_ref[...], lk1_ref[...], lq2_ref[...], lk2_ref[...], lam_init)
        o = accs_ref[...] / ls_ref[...]
        lane = lax.broadcasted_iota(jnp.int32, (t_new, w), 1)
        a = jnp.zeros((t_new, w), F32)
        for h in range(N_ATT_HEADS):
            o1 = o[(2 * h) * t_new:(2 * h + 1) * t_new]
            o2 = o[(2 * h + 1) * t_new:(2 * h + 2) * t_new]
            a = jnp.where((lane >= D_V * h) & (lane < D_V * (h + 1)), o1 - lam_s * o2, a)
        a2 = a * a
        msq = jnp.zeros((t_new, w), F32)
        for h in range(N_ATT_HEADS):
            in_h = (lane >= D_V * h) & (lane < D_V * (h + 1))
            ssh = jnp.sum(jnp.where(in_h, a2, 0.0), axis=-1, keepdims=True)
            msq = jnp.where(in_h, ssh, msq)
        os_ref[seq] = (a * lax.rsqrt(msq * (1.0 / D_V) + EPS) * g8_ref[...] * (1.0 - lam_init)).astype(BF16)

    sub = lax.broadcasted_iota(jnp.int32, (LANES, tq), 0)
    col = lax.broadcasted_iota(jnp.int32, (EXTRA_ROWS, cols), 1)
    row = lax.broadcasted_iota(jnp.int32, (EXTRA_ROWS, cols), 0)
    feat = jnp.zeros((EXTRA_ROWS, cols), F32)
    for i, part in enumerate(L2E_PARTS):
        feat = jnp.where(row == i, POS_SPLIT * part, feat)
        feat = jnp.where(row == len(L2E_PARTS) + i, part, feat)
    for h in range(n_hp):
        qt = qt_ref[0, h * LANES:(h + 1) * LANES, :]
        for c in range(4):
            qst_ref[h, 0:LANES, c * tq:(c + 1) * tq] = jnp.where((sub >= D_QK * c) & (sub < D_QK * (c + 1)), qt,
                                                                 jnp.zeros_like(qt))
        s0, s1 = _pair_slopes(pl.program_id(1) * n_hp + h)
        qst_ref[h, LANES:LANES + EXTRA_ROWS, :] = (jnp.where(col < 2 * tq, s0, s1) * feat).astype(BF16)
        qst_ref[h, LANES + EXTRA_ROWS:, :] = jnp.zeros((LANES - EXTRA_ROWS, cols), BF16)
    m_ref[...] = jnp.full(m_ref.shape, NEG_INF, F32)
    acc_ref[...] = jnp.zeros(acc_ref.shape, F32)

    def chunks(kj0, n_sub, masked, rows=tk):
        def scores(h):
            hl = slice(h * LANES, (h + 1) * LANES)
            sts = []
            for i in range(n_sub):
                start = pl.multiple_of((kj0 + i) * tk, tk)
                kaug = jnp.concatenate([k_ref[0, pl.ds(start, rows), hl], pf_ref[pl.ds(start, rows), :]], axis=1)
                st = jnp.dot(kaug, qst_ref[h], preferred_element_type=F32)
                if masked:
                    off = ((kj0 + i) * tk - qi * tq).astype(F32)
                    st = jnp.where(rel_ref[0:rows, :] + off <= 0.0, st, NEG_INF)
                sts.append(st)
            return sts

        all_sts = [scores(h) for h in range(n_hp)]
        for h in range(n_hp):
            hl = slice(h * LANES, (h + 1) * LANES)
            sts = all_sts[h]
            m_old = m_ref[h]
            m_new = m_old
            for st in sts:
                m_new = jnp.maximum(m_new, jnp.max(st, axis=0, keepdims=True))
            acc = acc_ref[h] * jnp.exp2(m_old - m_new)
            for i, st in enumerate(sts):
                pt = jnp.exp2(st - m_new).astype(BF16)
                vaug = jnp.concatenate([vt_ref[0, kj0 + i, hl, 0:rows], jnp.ones((EXTRA_ROWS, rows), BF16)], axis=0)
                acc = acc + jnp.dot(vaug, pt, preferred_element_type=F32)
            acc_ref[h] = acc
            m_ref[h] = m_new

    n_full = (qi * tq) // tk

    def body(i, carry):
        chunks(kv_unroll * i, kv_unroll, False)
        return carry

    assert subs_per_step == 4
    sample_sub(0)
    lax.fori_loop(0, n_full // kv_unroll, body, 0)
    if kv_unroll == 2:
        @pl.when(n_full % 2 == 1)
        def _():
            chunks(n_full - 1, 1, False)
    else:
        assert kv_unroll == 1
    sample_sub(1)

    assert tk == 2 * tq
    tile_at_chunk_start = (qi * tq) % tk == 0

    @pl.when(tile_at_chunk_start)
    def _():
        chunks(n_full, 1, True, rows=tq)

    @pl.when(jnp.logical_not(tile_at_chunk_start))
    def _():
        chunks(n_full, 1, True)

    sample_sub(2)
    sample_sub(3)
    pl.when(step % steps_per_seq == steps_per_seq - 1)(sample_finish)

    lam = _diff_lambda(lq1_ref[...], lk1_ref[...], lq2_ref[...], lk2_ref[...], lam_init)
    for h in range(n_hp):
        ot = acc_ref[h, 0:LANES, :] / acc_ref[h, LANES:LANES + 1, :]
        at = jnp.where(sub < D_V, ot[:, 0:tq] - lam * ot[:, tq:2 * tq], ot[:, 2 * tq:3 * tq] - lam * ot[:, 3 * tq:])
        a2 = at * at
        ss0 = jnp.sum(a2[0:D_V], axis=0, keepdims=True)
        ss1 = jnp.sum(a2[D_V:], axis=0, keepdims=True)
        ms = jnp.where(sub < D_V, ss0, ss1) * (1.0 / D_V)
        an = at * lax.rsqrt(ms + EPS) * g_ref[...] * (1.0 - lam_init)
        o_ref[0, :, h * LANES:(h + 1) * LANES] = an.T.astype(BF16)


def _attn(qtb, kb, vtb, page_table, qsb, ks, vs, cache_k, cache_v, lam_params, g_subln, lam_init,
          *, tq=256, kv_unroll=1, pps=16):
    b, w, s = qtb.shape
    n_chunks, tk = vtb.shape[1], vtb.shape[3]
    n_hp = w // LANES
    hw = n_hp * LANES
    cols = 4 * tq
    bd, t_new, _ = qsb.shape
    n_pages = page_table.shape[1]
    n_pool = cache_k.shape[0]
    past = n_pages * PAGE_SIZE
    n_steps = b * (s // tq)
    subs_per_step = (bd * n_pages) // (pps * n_steps)
    assert subs_per_step * pps * n_steps == bd * n_pages and (n_pages // pps) % subs_per_step == 0
    ck = jnp.transpose(cache_k, (0, 2, 3, 1)).reshape(n_pool, w, PAGE_SIZE)
    cv = jnp.transpose(cache_v, (0, 2, 3, 1)).reshape(n_pool, w, PAGE_SIZE)
    g8 = jnp.tile(g_subln, N_ATT_HEADS).reshape(1, w)
    srows = 2 * N_ATT_HEADS * t_new
    g2 = jnp.concatenate([g_subln, g_subln]).reshape(LANES, 1)
    pos = lax.broadcasted_iota(jnp.int32, (s, LANES), 0)
    lane = lax.broadcasted_iota(jnp.int32, (s, LANES), 1)
    n_parts = len(L2E_PARTS)
    pf = jnp.where(lane < n_parts, pos // POS_SPLIT,
                   jnp.where(lane < N_POS_FEATS, pos % POS_SPLIT, 0)).astype(BF16)
    rel = (jnp.arange(tk, dtype=jnp.int32)[:, None] - (jnp.arange(cols, dtype=jnp.int32) % tq)[None, :]).astype(F32)
    const2 = lambda bi, hp, qi, pt: (0, 0)
    const3 = lambda bi, hp, qi, pt: (0, 0, 0)
    lam_specs = [pl.BlockSpec((1, D_QK), const2)] * 4
    grid_spec = pltpu.PrefetchScalarGridSpec(
        num_scalar_prefetch=1,
        grid=(b, w // hw, s // tq),
        in_specs=[pl.BlockSpec((1, hw, tq), lambda bi, hp, qi, pt: (bi, hp, qi)),
                  pl.BlockSpec((1, s, hw), lambda bi, hp, qi, pt: (bi, 0, hp)),
                  pl.BlockSpec((1, n_chunks, hw, tk), lambda bi, hp, qi, pt: (bi, 0, hp, 0)),
                  pl.BlockSpec((s, LANES), const2),
                  pl.BlockSpec((tk, cols), const2),
                  *lam_specs,
                  pl.BlockSpec((LANES, 1), const2),
                  pl.BlockSpec((bd, t_new, w), const3),
                  pl.BlockSpec((bd, t_new, w), const3),
                  pl.BlockSpec((bd, t_new, w), const3),
                  pl.BlockSpec((1, w), const2),
                  pl.BlockSpec(memory_space=pl.ANY),
                  pl.BlockSpec(memory_space=pl.ANY)],
        out_specs=[pl.BlockSpec((1, tq, hw), lambda bi, hp, qi, pt: (bi, qi, hp)),
                   pl.BlockSpec((bd, t_new, w), const3)],
        scratch_shapes=[pltpu.VMEM((n_hp, 2 * LANES, cols), BF16),
                        pltpu.VMEM((n_hp, 1, cols), F32),
                        pltpu.VMEM((n_hp, LANES + EXTRA_ROWS, cols), F32),
                        pltpu.VMEM((srows, w), BF16),
                        pltpu.VMEM((srows, 1), F32),
                        pltpu.VMEM((srows, 1), F32),
                        pltpu.VMEM((srows, w), F32),
                        pltpu.VMEM((2, pps, w, PAGE_SIZE), F32),
                        pltpu.VMEM((2, pps, w, PAGE_SIZE), F32),
                        pltpu.SemaphoreType.DMA((2, 2))],
    )
    return pl.pallas_call(
        functools.partial(_attn_kernel, tq=tq, tk=tk, n_hp=n_hp, kv_unroll=kv_unroll, lam_init=lam_init,
                          pps=pps, subs_per_step=subs_per_step, past=past),
        grid_spec=grid_spec,
        out_shape=[jax.ShapeDtypeStruct((b, s, w), BF16), jax.ShapeDtypeStruct((bd, t_new, w), BF16)],
        compiler_params=_cparams("arbitrary", "arbitrary", "arbitrary"),
        name="attn",
    )(page_table, qtb, kb, vtb, pf, rel, *[p.reshape(1, D_QK) for p in lam_params], g2,
      qsb, ks, vs, g8, ck, cv)


def _outproj_kernel(a_ref, u_ref, vs_ref, x_ref, ga1_ref, sc2_ref, sh2_ref, wcat_ref, sbias_ref, wout_ref,
                    gpost_ref, gpre_ref, wr_cat_ref, br_ref, tril_ref, *rest, group_dispatch):
    if group_dispatch:
        x1_ref, pay_ref, info_ref, counts_ref, sg_ref, run_ref = rest
    else:
        x1_ref, h2_ref, gates_ref, sg_ref = rest
    tm = x_ref.shape[0]
    sgu_w = u_ref.shape[1]
    lane = lax.broadcasted_iota(jnp.int32, (CHUNK, LANES), 1)
    for ci in range(tm // CHUNK):
        rs = slice(ci * CHUNK, (ci + 1) * CHUNK)
        for gp in range(sgu_w // LANES):
            cs = slice(gp * LANES, (gp + 1) * LANES)
            vp = vs_ref[rs, cs]
            zero = jnp.zeros_like(vp)
            rhs = jnp.concatenate([jnp.where(lane < SGU_GC, vp, zero), jnp.where(lane >= SGU_GC, vp, zero)], axis=0)
            mixed = jnp.dot(wcat_ref[gp], rhs, preferred_element_type=F32) + sbias_ref[:, cs]
            sg_ref[rs, cs] = (u_ref[rs, cs].astype(F32) * mixed).astype(BF16)
    att_w = a_ref.shape[1]
    mix = (jnp.dot(a_ref[...], wout_ref[0:att_w, :], preferred_element_type=F32)
           + jnp.dot(sg_ref[...], wout_ref[att_w:, :], preferred_element_type=F32))
    x1 = x_ref[...] + ga1_ref[0] * _rms(mix, gpost_ref[...])
    x1_ref[...] = x1
    h2 = _rms(x1, gpre_ref[...]) * (1.0 + sc2_ref[0]) + sh2_ref[0]
    h2_hi = h2.astype(BF16)
    h2_lo = (h2 - h2_hi.astype(F32)).astype(BF16)
    hh = jnp.dot(h2_hi, wr_cat_ref[...], preferred_element_type=F32)
    logits = (hh[:, :LANES] + hh[:, LANES:]
              + jnp.dot(h2_lo, wr_cat_ref[:, :LANES], preferred_element_type=F32)) + br_ref[...]
    ln = lax.broadcasted_iota(jnp.int32, (tm, LANES), 1)
    big = jnp.int32(LANES)
    is_g = (ln >= N_EXPERTS) & (ln < N_EXPERTS + N_EXPERT_GROUPS)
    gl = jnp.where(is_g, logits, NEG_INF)
    gmax = jnp.max(gl, axis=-1, keepdims=True)
    g_idx = jnp.min(jnp.where(gl == gmax, ln, big), axis=-1, keepdims=True) - N_EXPERTS
    g_w = 1.0 / jnp.sum(jnp.exp(gl - gmax), axis=-1, keepdims=True)
    in_grp = (ln >= g_idx * EXPERTS_PER_GROUP) & (ln < (g_idx + 1) * EXPERTS_PER_GROUP)
    el = jnp.where(in_grp, logits, NEG_INF)
    emax = jnp.max(el, axis=-1, keepdims=True)
    ez = jnp.exp(el - emax)
    prob = ez / jnp.sum(ez, axis=-1, keepdims=True)
    p1 = jnp.max(prob, axis=-1, keepdims=True)
    i1 = jnp.min(jnp.where((prob == p1) & in_grp, ln, big), axis=-1, keepdims=True)
    rest = jnp.where(in_grp & (ln != i1), prob, -1.0)
    p2 = jnp.max(rest, axis=-1, keepdims=True)
    i2 = jnp.min(jnp.where(rest == p2, ln, big), axis=-1, keepdims=True)
    den = p1 + p2
    gates = jnp.where(ln == i1, p1 / den * g_w, 0.0) + jnp.where(ln == i2, p2 / den * g_w, 0.0)
    if not group_dispatch:
        h2_ref[...] = h2_hi
        gates_ref[...] = gates[:, :N_EXPERTS]
        return

    bits = pltpu.bitcast(h2_hi.astype(F32), jnp.uint32)
    half = PAYLOAD_H2_SEGS * LANES
    for s in range(PAYLOAD_H2_SEGS):
        lo = bits[:, s * LANES:(s + 1) * LANES]
        hi = bits[:, half + s * LANES:half + (s + 1) * LANES]
        pay_ref[s] = hi | lax.shift_right_logical(lo, jnp.uint32(16))
    pay_ref[PAYLOAD_H2_SEGS] = pltpu.bitcast(gates, jnp.uint32)

    @pl.when(pl.program_id(0) == 0)
    def _():
        run_ref[...] = jnp.zeros_like(run_ref)

    onehot = jnp.where(ln == g_idx, 1.0, 0.0).astype(BF16)
    csum = jnp.dot(tril_ref[...], onehot, preferred_element_type=F32) + run_ref[...]
    rank = jnp.sum(jnp.where(ln == g_idx, csum, 0.0), axis=-1, keepdims=True) - 1.0
    run_ref[...] = csum[tm - 1:tm, :]
    counts_ref[...] = csum[tm - 1:tm, :]
    info = jnp.where(ln == 0, g_idx, jnp.where(ln == 1, rank.astype(jnp.int32), 0))
    info_ref[...] = info[:, :INFO_W]


def _outproj(a, u, vsb, x, ga1, sc2, sh2, wcat, sbias, w_out_bf, g_post, g_pre, wr_cat, br,
             *, tm, tiles_per_mod, group_dispatch):
    n, d = x.shape
    att_w = a.shape[1]
    sgu_w = u.shape[1]
    r = ga1.shape[1]
    row = lambda i: (i, 0)
    modmap = lambda i: (i // tiles_per_mod, 0, 0)
    const = lambda i: (0, 0)
    tril = jnp.tril(jnp.ones((tm, tm), BF16))
    if group_dispatch:
        out_specs = [pl.BlockSpec((tm, d), row),
                     pl.BlockSpec((PAYLOAD_SEGS, tm, LANES), lambda i: (0, i, 0)),
                     pl.BlockSpec((tm, INFO_W), row),
                     pl.BlockSpec((1, LANES), const)]
        out_shape = [jax.ShapeDtypeStruct((n, d), F32),
                     jax.ShapeDtypeStruct((PAYLOAD_SEGS, n, LANES), jnp.uint32),
                     jax.ShapeDtypeStruct((n, INFO_W), jnp.int32),
                     jax.ShapeDtypeStruct((1, LANES), F32)]
        scratch = [pltpu.VMEM((tm, sgu_w), BF16), pltpu.VMEM((1, LANES), F32)]
    else:
        out_specs = [pl.BlockSpec((tm, d), row),
                     pl.BlockSpec((tm, d), row),
                     pl.BlockSpec((tm, N_EXPERTS), row)]
        out_shape = [jax.ShapeDtypeStruct((n, d), F32),
                     jax.ShapeDtypeStruct((n, d), BF16),
                     jax.ShapeDtypeStruct((n, N_EXPERTS), F32)]
        scratch = [pltpu.VMEM((tm, sgu_w), BF16)]
    return pl.pallas_call(
        functools.partial(_outproj_kernel, group_dispatch=group_dispatch),
        grid=(n // tm,),
        in_specs=[pl.BlockSpec((tm, att_w), row),
                  pl.BlockSpec((tm, sgu_w), row),
                  pl.BlockSpec((tm, sgu_w), row),
                  pl.BlockSpec((tm, d), row),
                  pl.BlockSpec((1, r, d), modmap),
                  pl.BlockSpec((1, r, d), modmap),
                  pl.BlockSpec((1, r, d), modmap),
                  pl.BlockSpec(wcat.shape, lambda i: (0, 0, 0)),
                  pl.BlockSpec(sbias.shape, const),
                  pl.BlockSpec(w_out_bf.shape, const),
                  pl.BlockSpec((1, d), const),
                  pl.BlockSpec((1, d), const),
                  pl.BlockSpec(wr_cat.shape, const),
                  pl.BlockSpec((1, LANES), const),
                  pl.BlockSpec((tm, tm), const)],
        out_specs=out_specs,
        out_shape=out_shape,
        scratch_shapes=scratch,
        compiler_params=_cparams("arbitrary"),
        name="outproj",
    )(a, u, vsb, x, ga1, sc2, sh2, wcat, sbias, w_out_bf, g_post.reshape(1, d), g_pre.reshape(1, d),
      wr_cat, br, tril)


def _moe_kernel(h_ref, gates_ref, x1_ref, ga2_ref, gpost_ref, wg_ref, wu_ref, wd_ref, o_ref, acc_ref):
    e = pl.program_id(1)

    @pl.when(e == 0)
    def _():
        acc_ref[...] = jnp.zeros_like(acc_ref)

    h = h_ref[...]
    gate = jnp.dot(h, wg_ref[0], preferred_element_type=F32)
    up = jnp.dot(h, wu_ref[0], preferred_element_type=F32)
    hdn = (gate * jax.nn.sigmoid(gate) * up).astype(BF16)
    y = jnp.dot(hdn, wd_ref[0], preferred_element_type=F32)
    ln = lax.broadcasted_iota(jnp.int32, gates_ref.shape, 1)
    ge = jnp.sum(jnp.where(ln == e, gates_ref[...], 0.0), axis=-1, keepdims=True)
    acc_ref[...] += ge * y

    @pl.when(e == pl.num_programs(1) - 1)
    def _():
        o_ref[...] = x1_ref[...] + ga2_ref[0] * _rms(acc_ref[...], gpost_ref[...])


def _moe(h2, gates, x1, ga2, g_post, wg_bf, wu_bf, wd_bf, *, tm, tiles_per_mod):
    n, d = x1.shape
    n_e, _, de = wg_bf.shape
    r = ga2.shape[1]
    row = lambda i, e: (i, 0)
    return pl.pallas_call(
        _moe_kernel,
        grid=(n // tm, n_e),
        in_specs=[pl.BlockSpec((tm, d), row),
                  pl.BlockSpec((tm, n_e), row),
                  pl.BlockSpec((tm, d), row),
                  pl.BlockSpec((1, r, d), lambda i, e: (i // tiles_per_mod, 0, 0)),
                  pl.BlockSpec((1, d), lambda i, e: (0, 0)),
                  pl.BlockSpec((1, d, de), lambda i, e: (e, 0, 0)),
                  pl.BlockSpec((1, d, de), lambda i, e: (e, 0, 0)),
                  pl.BlockSpec((1, de, d), lambda i, e: (e, 0, 0))],
        out_specs=pl.BlockSpec((tm, d), row),
        out_shape=jax.ShapeDtypeStruct((n, d), F32),
        scratch_shapes=[pltpu.VMEM((tm, d), F32)],
        compiler_params=_cparams("arbitrary", "arbitrary"),
        name="moe",
    )(h2, gates, x1, ga2, g_post.reshape(1, d), wg_bf, wu_bf, wd_bf)


def _sc_mesh():
    return plsc.VectorSubcoreMesh(core_axis_name="core", subcore_axis_name="subcore")


def _sc_scatter_rows(x, idx, n_out_rows):
    n_rows, width = x.shape

    @pl.kernel(out_type=jax.ShapeDtypeStruct((n_out_rows, width), x.dtype), mesh=_sc_mesh(), scratch_types=[])
    def scatter(x_hbm, i_hbm, o_hbm):
        def body(x_vmem, i_vmem):
            pltpu.sync_copy(x_vmem, o_hbm.at[i_vmem.at[0]])

        pltpu.emit_pipeline(
            body,
            grid=(n_rows // SC_WINDOW,),
            in_specs=[pl.BlockSpec((SC_WINDOW, width), lambda i: (i, 0)),
                      pl.BlockSpec((1, SC_WINDOW), lambda i: (0, i))],
            out_specs=[],
            core_axis_name=("core", "subcore"),
            dimension_semantics=(pltpu.PARALLEL,),
        )(x_hbm, i_hbm)

    return scatter(x, idx.reshape(1, n_rows))


def _sc_gather_rows(x, idx):
    n_rows = idx.shape[0]
    width = x.shape[1]

    @pl.kernel(out_type=jax.ShapeDtypeStruct((n_rows, width), x.dtype), mesh=_sc_mesh(), scratch_types=[])
    def gather(x_hbm, i_hbm, o_hbm):
        def body(i_vmem, o_vmem):
            pltpu.sync_copy(x_hbm.at[i_vmem.at[0]], o_vmem)

        pltpu.emit_pipeline(
            body,
            grid=(n_rows // SC_WINDOW,),
            in_specs=[pl.BlockSpec((1, SC_WINDOW), lambda i: (0, i))],
            out_specs=[pl.BlockSpec((SC_WINDOW, width), lambda i: (i, 0))],
            core_axis_name=("core", "subcore"),
            dimension_semantics=(pltpu.PARALLEL,),
        )(i_hbm, o_hbm)

    return gather(x, idx.reshape(1, n_rows))


def _group_moe_kernel(tg_ref, nused_ref, xs_ref, gpost_ref, wg_ref, wu_ref, wd_ref, rs_ref):
    t = pl.program_id(0)
    tm = xs_ref.shape[1]

    @pl.when(t >= nused_ref[0])
    def _():
        rs_ref[...] = jnp.zeros_like(rs_ref)

    @pl.when(t < nused_ref[0])
    def _():
        words = [xs_ref[s] for s in range(PAYLOAD_H2_SEGS)]
        lo = [pltpu.bitcast(lax.shift_left(w, jnp.uint32(16)), F32) for w in words]
        hi = [pltpu.bitcast(w & jnp.uint32(0xFFFF0000), F32) for w in words]
        x = jnp.concatenate(lo + hi, axis=1).astype(BF16)
        gates = pltpu.bitcast(xs_ref[PAYLOAD_H2_SEGS], F32)
        ln = lax.broadcasted_iota(jnp.int32, (tm, LANES), 1)
        first = tg_ref[t] * EXPERTS_PER_GROUP
        y = jnp.zeros((tm, x.shape[1]), F32)
        for el in range(EXPERTS_PER_GROUP):
            ge = jnp.sum(jnp.where(ln == first + el, gates, 0.0), axis=-1, keepdims=True)
            gate = jnp.dot(x, wg_ref[el], preferred_element_type=F32)
            up = jnp.dot(x, wu_ref[el], preferred_element_type=F32)
            hdn = (gate * jax.nn.sigmoid(gate) * up).astype(BF16)
            y = y + ge * jnp.dot(hdn, wd_ref[el], preferred_element_type=F32)
        r = _rms(y, gpost_ref[...])
        for s in range(rs_ref.shape[0]):
            rs_ref[s] = r[:, s * LANES:(s + 1) * LANES]


def _group_moe(xs, tile_group, n_used, g_post, wg_bf, wu_bf, wd_bf, *, tm):
    _, p_rows, _ = xs.shape
    n_e, d, de = wg_bf.shape
    g = EXPERTS_PER_GROUP
    grid_spec = pltpu.PrefetchScalarGridSpec(
        num_scalar_prefetch=2,
        grid=(p_rows // tm,),
        in_specs=[pl.BlockSpec((PAYLOAD_SEGS, tm, LANES), lambda t, tg, nu: (0, t, 0)),
                  pl.BlockSpec((1, d), lambda t, tg, nu: (0, 0)),
                  pl.BlockSpec((g, d, de), lambda t, tg, nu: (tg[t], 0, 0)),
                  pl.BlockSpec((g, d, de), lambda t, tg, nu: (tg[t], 0, 0)),
                  pl.BlockSpec((g, de, d), lambda t, tg, nu: (tg[t], 0, 0))],
        out_specs=pl.BlockSpec((d // LANES, tm, LANES), lambda t, tg, nu: (0, t, 0)),
    )
    return pl.pallas_call(
        _group_moe_kernel,
        grid_spec=grid_spec,
        out_shape=jax.ShapeDtypeStruct((d // LANES, p_rows, LANES), F32),
        compiler_params=_cparams("arbitrary"),
        name="group_moe",
    )(tile_group, n_used, xs, g_post.reshape(1, d), wg_bf, wu_bf, wd_bf)


def _residual_kernel(x1_ref, ga2_ref, r_ref, o_ref):
    r = jnp.concatenate([r_ref[s] for s in range(r_ref.shape[0])], axis=1)
    o_ref[...] = x1_ref[...] + ga2_ref[0] * r


def _residual(x1, ga2, r_segs, *, tm, tiles_per_mod):
    n, d = x1.shape
    return pl.pallas_call(
        _residual_kernel,
        grid=(n // tm,),
        in_specs=[pl.BlockSpec((tm, d), lambda i: (i, 0)),
                  pl.BlockSpec((1, ga2.shape[1], d), lambda i: (i // tiles_per_mod, 0, 0)),
                  pl.BlockSpec((d // LANES, tm, LANES), lambda i: (0, i, 0))],
        out_specs=pl.BlockSpec((tm, d), lambda i: (i, 0)),
        out_shape=jax.ShapeDtypeStruct((n, d), F32),
        compiler_params=_cparams("arbitrary"),
        name="residual",
    )(x1, ga2, r_segs)


def _group_plan(info, counts, n_tokens, tm):
    n_tiles = n_tokens // tm + N_EXPERT_GROUPS
    cnt = counts[0, :N_EXPERT_GROUPS].astype(jnp.int32)
    padded = (cnt + tm - 1) // tm * tm
    ends = jnp.cumsum(padded)
    starts = ends - padded
    gid, rank = info[:, 0], info[:, 1]
    onehot = (gid[:, None] == jnp.arange(N_EXPERT_GROUPS, dtype=jnp.int32)[None, :]).astype(jnp.int32)
    pos = jnp.sum(onehot * starts[None, :], axis=1) + rank
    tile_ends = ends // tm
    t = jnp.arange(n_tiles, dtype=jnp.int32)
    tile_group = jnp.minimum(jnp.sum((t[:, None] >= tile_ends[None, :]).astype(jnp.int32), axis=1),
                             N_EXPERT_GROUPS - 1)
    return pos, tile_group, tile_ends[-1:], n_tiles * tm


def _sgu_weights(ws, bs, chunk_len):
    causal = jnp.tril(jnp.ones((chunk_len, chunk_len), ws.dtype))
    wm = ws[:, :chunk_len, :chunk_len] * causal
    reps = CHUNK // chunk_len
    if reps > 1:
        eye = jnp.eye(reps, dtype=ws.dtype)
        wm = jnp.einsum("ab,gts->gatbs", eye, wm).reshape(N_SGU_GROUPS, CHUNK, CHUNK)
    wcat = jnp.concatenate([wm[0::2], wm[1::2]], axis=2).astype(BF16)
    bt = jnp.tile(bs[:, :chunk_len].T, (reps, 1))
    sbias = jnp.repeat(bt, SGU_GC, axis=1)
    return wcat, sbias


def _split_mod(mod):
    return [m[:, None, :] for m in jnp.split(mod, 6, axis=-1)]


def kernel(x_prompt, x_sample, cache_k, cache_v, page_table, c_prompt, c_sample, w_ada, b_ada, g_pre_mix, g_post_mix, g_pre_ffn, g_post_ffn, w_in, lam_q1, lam_k1, lam_q2, lam_k2, g_subln, sgu_ln_g, sgu_ln_b, sgu_ws, sgu_bs, w_out, w_rg, b_rg, w_re, b_re, w_gate, w_up, w_down):
    depth = w_in.shape[0]
    assert depth == 1
    l = 0
    lam_init = 0.8 - 0.6 * math.exp(-0.3 * l)
    bp, sp, d = x_prompt.shape
    bs_, ts, _ = x_sample.shape
    n_s = bs_ * ts

    mod = _ada(jnp.concatenate([c_prompt, c_sample], axis=0), w_ada[l], b_ada[l])
    sh1p, sc1p, ga1p, sh2p, sc2p, ga2p = _split_mod(mod[:bp])
    rep = lambda m: jnp.repeat(m, ts, axis=0).reshape(1, n_s, d)
    sh1s, sc1s, ga1s, sh2s, sc2s, ga2s = [rep(m) for m in jnp.split(mod[bp:], 6, axis=-1)]

    w_in_bf = w_in[l].astype(BF16)
    wqv_t_bf = jnp.concatenate([w_in_bf[:, 0:ATT_W], w_in_bf[:, 2 * ATT_W:3 * ATT_W]], axis=1).T
    w_out_bf = w_out[l].astype(BF16)
    wg_bf = w_gate[l].astype(BF16)
    wu_bf = w_up[l].astype(BF16)
    wd_bf = w_down[l].astype(BF16)
    wr = jnp.concatenate([w_re[l], w_rg[l]], axis=1)
    wr = jnp.pad(wr, ((0, 0), (0, LANES - wr.shape[1])))
    wr_hi = wr.astype(BF16)
    wr_cat = jnp.concatenate([wr_hi, (wr - wr_hi.astype(F32)).astype(BF16)], axis=1)
    br = jnp.pad(jnp.concatenate([b_re[l], b_rg[l]]), (0, LANES - N_EXPERTS - N_EXPERT_GROUPS)).reshape(1, LANES)
    lam_params = (lam_q1[l], lam_k1[l], lam_q2[l], lam_k2[l])

    tm_p = 512
    xp = x_prompt.reshape(bp * sp, d)
    ktp, vtp, qtb, kb, vtb, u, vsb, vs_last = _inproj(
        xp, sc1p, sh1p, g_pre_mix[l], w_in_bf, wqv_t_bf, sgu_ln_g[l], sgu_ln_b[l],
        tm=tm_p, tiles_per_seq=sp // tm_p, last_rows=CHUNK, transposed_qv=True)
    w3 = ATT_W
    xs = x_sample.reshape(n_s, d)
    ks, vs_, qbs, us, vsbs, vs_last_s = _inproj(
        xs, sc1s, sh1s, g_pre_mix[l], w_in_bf, wqv_t_bf, sgu_ln_g[l], sgu_ln_b[l],
        tm=n_s, tiles_per_seq=1, last_rows=n_s, transposed_qv=False)
    a, a_s = _attn(qtb, kb.reshape(bp, sp, w3), vtb, page_table, qbs.reshape(bs_, ts, w3),
                   ks.reshape(bs_, ts, w3), vs_.reshape(bs_, ts, w3), cache_k[l], cache_v[l],
                   lam_params, g_subln[l], lam_init)
    kp = jnp.transpose(ktp.reshape(bp, N_ATT_HEADS, 2 * D_QK, sp), (0, 3, 1, 2))
    vp = jnp.transpose(vtp.reshape(bp, N_ATT_HEADS, D_V, sp), (0, 3, 1, 2))
    wcat_p, sbias_p = _sgu_weights(sgu_ws[l], sgu_bs[l], CHUNK)
    n_p = bp * sp
    x1, payload, info, counts = _outproj(
        a.reshape(n_p, w3), u, vsb, xp, ga1p, sc2p, sh2p, wcat_p, sbias_p, w_out_bf,
        g_post_mix[l], g_pre_ffn[l], wr_cat, br, tm=tm_p, tiles_per_mod=sp // tm_p, group_dispatch=True)
    tm_moe = 512
    pos, tile_group, n_used, p_rows = _group_plan(info, counts, n_p, tm_moe)
    seg_p = lambda k: (jnp.arange(k, dtype=jnp.int32)[:, None] * p_rows + pos[None, :]).reshape(-1)
    sorted_rows = _sc_scatter_rows(payload.reshape(PAYLOAD_SEGS * n_p, LANES), seg_p(PAYLOAD_SEGS),
                                   PAYLOAD_SEGS * p_rows)
    rs = _group_moe(sorted_rows.reshape(PAYLOAD_SEGS, p_rows, LANES), tile_group, n_used, g_post_ffn[l],
                    wg_bf, wu_bf, wd_bf, tm=tm_moe)
    n_seg = d // LANES
    r_tok = _sc_gather_rows(rs.reshape(n_seg * p_rows, LANES), seg_p(n_seg))
    yp = _residual(x1, ga2p, r_tok.reshape(n_seg, n_p, LANES), tm=tm_p, tiles_per_mod=sp // tm_p)

    wcat_s, sbias_s = _sgu_weights(sgu_ws[l], sgu_bs[l], ts)
    x1s, h2s, gates_s = _outproj(a_s.reshape(n_s, w3), us, vsbs, xs, ga1s, sc2s, sh2s, wcat_s, sbias_s, w_out_bf,
                                 g_post_mix[l], g_pre_ffn[l], wr_cat, br, tm=n_s, tiles_per_mod=1,
                                 group_dispatch=False)
    ys = _moe(h2s, gates_s, x1s, ga2s, g_post_ffn[l], wg_bf, wu_bf, wd_bf, tm=n_s, tiles_per_mod=1)

    return (yp.reshape(bp, sp, d), ys.reshape(bs_, ts, d),
            kp[None], vp[None],
            vs_last.reshape(1, bp, CHUNK, -1),
            ks.reshape(1, bs_, ts, N_ATT_HEADS, 2 * D_QK), vs_.reshape(1, bs_, ts, N_ATT_HEADS, D_V),
            vs_last_s.reshape(1, bs_, ts, -1))
```

```python
import functools
import math
import struct

import jax
import jax.numpy as jnp
from jax import lax
from jax.experimental import pallas as pl
from jax.experimental.pallas import tpu as pltpu
from jax.experimental.pallas import tpu_sc as plsc

F32 = jnp.float32
BF16 = jnp.bfloat16

EPS = 1e-6
N_ATT_HEADS = 8
D_QK = 32
D_V = 64
ATT_W = N_ATT_HEADS * D_V
N_SGU_GROUPS = 8
SGU_GC = 64
SGU_W = N_SGU_GROUPS * SGU_GC
CHUNK = 128
N_EXPERT_GROUPS = 4
EXPERTS_PER_GROUP = 4
N_EXPERTS = 16
PAGE_SIZE = 128
LANES = 128
VMEM_LIMIT = 56 * 1024 * 1024
NEG_INF = float("-inf")
LOG2E = math.log2(math.e)
PAYLOAD_H2_SEGS = 4
PAYLOAD_SEGS = PAYLOAD_H2_SEGS + 1
INFO_W = 8
SC_WINDOW = 128


def _cparams(*sem):
    return pltpu.CompilerParams(dimension_semantics=sem, vmem_limit_bytes=VMEM_LIMIT)


def _rms(x, g):
    return x * lax.rsqrt(jnp.mean(x * x, axis=-1, keepdims=True) + EPS) * g


def _ada_kernel(c_ref, w_ref, b_ref, o_ref):
    c = c_ref[...]
    s = c * jax.nn.sigmoid(c)
    o_ref[...] = jnp.dot(s.astype(BF16), w_ref[...].astype(BF16), preferred_element_type=F32) + b_ref[...]


def _ada(c, w_ada, b_ada):
    n, d = c.shape
    nout = w_ada.shape[1]
    tn = d
    return pl.pallas_call(
        _ada_kernel,
        grid=(nout // tn,),
        in_specs=[pl.BlockSpec((n, d), lambda j: (0, 0)),
                  pl.BlockSpec((d, tn), lambda j: (0, j)),
                  pl.BlockSpec((1, tn), lambda j: (0, j))],
        out_specs=pl.BlockSpec((n, tn), lambda j: (0, j)),
        out_shape=jax.ShapeDtypeStruct((n, nout), F32),
        compiler_params=_cparams("arbitrary"),
        name="ada",
    )(c, w_ada, b_ada.reshape(1, nout))


Q_SCALE = D_QK ** -0.5 * LOG2E


def _inproj_kernel(x_ref, sc_ref, sh_ref, g_ref, w_ref, wqv_t_ref, lng_ref, lnb_ref, *out_refs,
                   last_rows, transposed_qv):
    x = x_ref[...]
    h = (_rms(x, g_ref[...]) * (1.0 + sc_ref[0]) + sh_ref[0]).astype(BF16)

    def proj(lo):
        return jnp.dot(h, w_ref[:, lo:lo + ATT_W], preferred_element_type=F32)

    def proj_t(lo):
        return lax.dot_general(wqv_t_ref[lo:lo + ATT_W, :], h, (((1,), (1,)), ((), ())),
                               preferred_element_type=F32)

    zk = proj(ATT_W)
    if transposed_qv:
        kt_ref, vt_ref, qtb_ref, kb_ref, vtb_ref, u_ref, vsb_ref, vsl_ref = out_refs
        kt_ref[0] = zk.T
        kb_ref[...] = zk.astype(BF16)
        qtb_ref[0] = (proj_t(0) * Q_SCALE).astype(BF16)
        zvt = proj_t(ATT_W)
        vt_ref[0] = zvt
        vtb_ref[0, 0] = zvt.astype(BF16)
    else:
        k_ref, v_ref, qb_ref, u_ref, vsb_ref, vsl_ref = out_refs
        k_ref[...] = zk
        qb_ref[...] = (proj(0) * Q_SCALE).astype(BF16)
        v_ref[...] = proj(2 * ATT_W)
    u_ref[...] = jax.nn.gelu(proj(3 * ATT_W)).astype(BF16)
    gs = jax.nn.gelu(proj(3 * ATT_W + SGU_W))
    mu = jnp.mean(gs, axis=-1, keepdims=True)
    xc = gs - mu
    vs = xc * lax.rsqrt(jnp.mean(xc * xc, axis=-1, keepdims=True) + EPS) * lng_ref[...] + lnb_ref[...]
    vsb_ref[...] = vs.astype(BF16)
    tm = vs.shape[0]
    vsl_ref[0] = vs[tm - last_rows:, :]


def _inproj(x, sc, sh, g, w_in_bf, wqv_t_bf, ln_g, ln_b, *, tm, tiles_per_seq, last_rows, transposed_qv):
    n, d = x.shape
    in_w = w_in_bf.shape[1]
    r = sc.shape[1]
    n_tiles = n // tm
    n_seq = n_tiles // tiles_per_seq
    s_len = tiles_per_seq * tm
    row = lambda i: (i, 0)
    modmap = lambda i: (i // tiles_per_seq, 0, 0)
    const = lambda i: (0, 0)
    col_t = lambda i: (i // tiles_per_seq, 0, i % tiles_per_seq)
    row_spec = pl.BlockSpec((tm, ATT_W), row)
    tail = [(pl.BlockSpec((tm, SGU_W), row), jax.ShapeDtypeStruct((n, SGU_W), BF16)),
            (pl.BlockSpec((tm, SGU_W), row), jax.ShapeDtypeStruct((n, SGU_W), BF16)),
            (pl.BlockSpec((1, last_rows, SGU_W), lambda i: (i // tiles_per_seq, 0, 0)),
             jax.ShapeDtypeStruct((n_seq, last_rows, SGU_W), F32))]
    if transposed_qv:
        head = [(pl.BlockSpec((1, ATT_W, tm), col_t), jax.ShapeDtypeStruct((n_seq, ATT_W, s_len), F32)),
                (pl.BlockSpec((1, ATT_W, tm), col_t), jax.ShapeDtypeStruct((n_seq, ATT_W, s_len), F32)),
                (pl.BlockSpec((1, ATT_W, tm), col_t), jax.ShapeDtypeStruct((n_seq, ATT_W, s_len), BF16)),
                (row_spec, jax.ShapeDtypeStruct((n, ATT_W), BF16)),
                (pl.BlockSpec((1, 1, ATT_W, tm), lambda i: (i // tiles_per_seq, i % tiles_per_seq, 0, 0)),
                 jax.ShapeDtypeStruct((n_seq, tiles_per_seq, ATT_W, tm), BF16))]
    else:
        head = [(row_spec, jax.ShapeDtypeStruct((n, ATT_W), F32)),
                (row_spec, jax.ShapeDtypeStruct((n, ATT_W), F32)),
                (row_spec, jax.ShapeDtypeStruct((n, ATT_W), BF16))]
    specs, shapes = zip(*(head + tail))
    return pl.pallas_call(
        functools.partial(_inproj_kernel, last_rows=last_rows, transposed_qv=transposed_qv),
        grid=(n_tiles,),
        in_specs=[pl.BlockSpec((tm, d), row),
                  pl.BlockSpec((1, r, d), modmap),
                  pl.BlockSpec((1, r, d), modmap),
                  pl.BlockSpec((1, d), const),
                  pl.BlockSpec((d, in_w), const),
                  pl.BlockSpec(wqv_t_bf.shape, const),
                  pl.BlockSpec((1, SGU_W), const),
                  pl.BlockSpec((1, SGU_W), const)],
        out_specs=list(specs),
        out_shape=list(shapes),
        compiler_params=_cparams("arbitrary"),
        name="inproj",
    )(x, sc, sh, g.reshape(1, d), w_in_bf, wqv_t_bf, ln_g.reshape(1, SGU_W), ln_b.reshape(1, SGU_W))


def _diff_lambda(lq1, lk1, lq2, lk2, lam_init):
    return (jnp.exp(jnp.sum(lq1 * lk1, axis=-1, keepdims=True))
            - jnp.exp(jnp.sum(lq2 * lk2, axis=-1, keepdims=True)) + lam_init)


def _pair_slopes(hp):
    s0 = jnp.where(hp == 0, 2.0 ** -1, jnp.where(hp == 1, 2.0 ** -3, jnp.where(hp == 2, 2.0 ** -5, 2.0 ** -7)))
    return s0.astype(F32), (s0 * 0.5).astype(F32)


def _bf16_round(x):
    bits = struct.unpack("<I", struct.pack("<f", x))[0]
    bits = (bits + 0x7FFF + ((bits >> 16) & 1)) & 0xFFFF0000
    return struct.unpack("<f", struct.pack("<I", bits))[0]


L2E_PARTS = []
_rest = LOG2E
for _ in range(3):
    L2E_PARTS.append(_bf16_round(_rest))
    _rest -= L2E_PARTS[-1]
POS_SPLIT = 64
N_POS_FEATS = 2 * len(L2E_PARTS)
EXTRA_ROWS = 16


def _attn_kernel(pt_ref, qt_ref, k_ref, vt_ref, pf_ref, rel_ref, lq1_ref, lk1_ref, lq2_ref, lk2_ref, g_ref,
                 qs_ref, kn_ref, vn_ref, g8_ref, ck_hbm, cv_hbm, o_ref, os_ref,
                 qst_ref, m_ref, acc_ref, qbd_ref, ms_ref, ls_ref, accs_ref, kbuf_ref, vbuf_ref, sem_ref,
                 *, tq, tk, n_hp, lam_init, pps, subs_per_batch, n_subs, past):
    qi = pl.program_id(2)
    cols = 4 * tq

    t_new, w = qs_ref.shape[1], qs_ref.shape[2]
    step = pl.program_id(0) * pl.num_programs(2) + qi
    subs_per_seq = past // (pps * PAGE_SIZE)
    half_q = qi // 2
    sub_base = pl.program_id(0) * subs_per_batch + half_q * half_q + (qi % 2) * half_q
    n_hj = 2 * N_ATT_HEADS
    srows = n_hj * t_new
    blk = pps * PAGE_SIZE
    srow_id = lax.broadcasted_iota(jnp.int32, (srows, 1), 0)
    shead = srow_id // (2 * t_new)
    sslope = jnp.zeros((srows, 1), F32)
    for h in range(N_ATT_HEADS):
        sslope = jnp.where(shead == h, 2.0 ** -(h + 1), sslope)
    sslope = sslope * LOG2E
    sqpos = (past + srow_id % t_new).astype(F32)

    def page_copies(k, slot):
        kc = jnp.minimum(k, n_subs - 1)
        seq_i, sub_i = kc // subs_per_seq, kc % subs_per_seq
        copies = []
        for i in range(pps):
            page = pt_ref[seq_i, sub_i * pps + i]
            copies.append(pltpu.make_async_copy(ck_hbm.at[page], kbuf_ref.at[slot, i], sem_ref.at[0, slot]))
            copies.append(pltpu.make_async_copy(cv_hbm.at[page], vbuf_ref.at[slot, i], sem_ref.at[1, slot]))
        return copies

    def sample_update(s, vmat, v_contract_dim):
        m_old = ms_ref[...]
        m_new = jnp.maximum(m_old, jnp.max(s, axis=-1, keepdims=True))
        p = jnp.exp2(s - m_new)
        corr = jnp.exp2(m_old - m_new)
        ls_ref[...] = corr * ls_ref[...] + jnp.sum(p, axis=-1, keepdims=True)
        pv = lax.dot_general(p.astype(vmat.dtype), vmat, (((1,), (v_contract_dim,)), ((), ())),
                             preferred_element_type=F32)
        accs_ref[...] = corr * accs_ref[...] + pv
        ms_ref[...] = m_new

    def sample_begin(k):
        @pl.when(k % subs_per_seq == 0)
        def _():
            q = qs_ref[k // subs_per_seq]
            lane = lax.broadcasted_iota(jnp.int32, (t_new, w), 1)
            for hj in range(n_hj):
                qbd_ref[hj * t_new:(hj + 1) * t_new, :] = jnp.where(
                    (lane >= D_QK * hj) & (lane < D_QK * (hj + 1)), q, jnp.zeros_like(q))
            ms_ref[...] = jnp.full((srows, 1), NEG_INF, F32)
            ls_ref[...] = jnp.zeros((srows, 1), F32)
            accs_ref[...] = jnp.zeros((srows, w), F32)

    def sample_sub(k):
        slot = k % 2
        for c in page_copies(k, slot):
            c.wait()
        for c in page_copies(k + 1, 1 - slot):
            c.start()
        kt_all = jnp.concatenate([kbuf_ref[slot, i].astype(BF16) for i in range(pps)], axis=1)
        vt_all = jnp.concatenate([vbuf_ref[slot, i].astype(BF16) for i in range(pps)], axis=1)
        s = jnp.dot(qbd_ref[...], kt_all, preferred_element_type=F32)
        kpos = ((k % subs_per_seq) * blk + lax.broadcasted_iota(jnp.int32, (1, blk), 1)).astype(F32)
        sample_update(s - sslope * (sqpos - kpos), vt_all, 1)

    @pl.when(step == 0)
    def _():
        for c in page_copies(0, 0):
            c.start()

    def sample_end(k):
        @pl.when(k % subs_per_seq == subs_per_seq - 1)
        def _():
            sample_finish(k // subs_per_seq)

        @pl.when(k == n_subs - 1)
        def _():
            for c in page_copies(n_subs, n_subs % 2):
                c.wait()

    def sample_finish(seq):
        kn = kn_ref[seq]
        sn = lax.dot_general(qbd_ref[...].astype(F32), kn, (((1,), (1,)), ((), ())), preferred_element_type=F32)
        kposn = (past + lax.broadcasted_iota(jnp.int32, (1, t_new), 1)).astype(F32)
        dist = sqpos - kposn
        sample_update(jnp.where(dist >= 0, sn - sslope * dist, NEG_INF), vn_ref[seq], 0)
        lam_s = _diff_lambda(lq1_ref[...], lk1_ref[...], lq2_ref[...], lk2_ref[...], lam_init)
        o = accs_ref[...] / ls_ref[...]
        lane = lax.broadcasted_iota(jnp.int32, (t_new, w), 1)
        a = jnp.zeros((t_new, w), F32)
        for h in range(N_ATT_HEADS):
            o1 = o[(2 * h) * t_new:(2 * h + 1) * t_new]
            o2 = o[(2 * h + 1) * t_new:(2 * h + 2) * t_new]
            a = jnp.where((lane >= D_V * h) & (lane < D_V * (h + 1)), o1 - lam_s * o2, a)
        a2 = a * a
        msq = jnp.zeros((t_new, w), F32)
        for h in range(N_ATT_HEADS):
            in_h = (lane >= D_V * h) & (lane < D_V * (h + 1))
            ssh = jnp.sum(jnp.where(in_h, a2, 0.0), axis=-1, keepdims=True)
            msq = jnp.where(in_h, ssh, msq)
        os_ref[seq] = (a * lax.rsqrt(msq * (1.0 / D_V) + EPS) * g8_ref[...] * (1.0 - lam_init)).astype(BF16)

    sub = lax.broadcasted_iota(jnp.int32, (LANES, tq), 0)
    col = lax.broadcasted_iota(jnp.int32, (EXTRA_ROWS, cols), 1)
    row = lax.broadcasted_iota(jnp.int32, (EXTRA_ROWS, cols), 0)
    feat = jnp.zeros((EXTRA_ROWS, cols), F32)
    for i, part in enumerate(L2E_PARTS):
        feat = jnp.where(row == i, POS_SPLIT * part, feat)
        feat = jnp.where(row == len(L2E_PARTS) + i, part, feat)
    for h in range(n_hp):
        qt = qt_ref[0, h * LANES:(h + 1) * LANES, :]
        for c in range(4):
            qst_ref[h, 0:LANES, c * tq:(c + 1) * tq] = jnp.where((sub >= D_QK * c) & (sub < D_QK * (c + 1)), qt,
                                                                 jnp.zeros_like(qt))
        s0, s1 = _pair_slopes(pl.program_id(1) * n_hp + h)
        qst_ref[h, LANES:LANES + EXTRA_ROWS, :] = (jnp.where(col < 2 * tq, s0, s1) * feat).astype(BF16)
        qst_ref[h, LANES + EXTRA_ROWS:, :] = jnp.zeros((LANES - EXTRA_ROWS, cols), BF16)
    m_ref[...] = jnp.full(m_ref.shape, NEG_INF, F32)
    acc_ref[...] = jnp.zeros(acc_ref.shape, F32)

    def chunks(kj0, n_sub, masked, rows=tk, between=None):
        def scores(h):
            hl = slice(h * LANES, (h + 1) * LANES)
            sts = []
            for i in range(n_sub):
                start = pl.multiple_of((kj0 + i) * tk, tk)
                kaug = jnp.concatenate([k_ref[0, pl.ds(start, rows), hl], pf_ref[pl.ds(start, rows), :]], axis=1)
                st = jnp.dot(kaug, qst_ref[h], preferred_element_type=F32)
                if masked:
                    off = ((kj0 + i) * tk - qi * tq).astype(F32)
                    st = jnp.where(rel_ref[0:rows, :] + off <= 0.0, st, NEG_INF)
                sts.append(st)
            return sts

        all_sts = [scores(h) for h in range(n_hp)]
        if between is not None:
            between()
        for h in range(n_hp):
            hl = slice(h * LANES, (h + 1) * LANES)
            sts = all_sts[h]
            m_old = m_ref[h]
            m_new = m_old
            for st in sts:
                m_new = jnp.maximum(m_new, jnp.max(st, axis=0, keepdims=True))
            acc = acc_ref[h] * jnp.exp2(m_old - m_new)
            for i, st in enumerate(sts):
                pt = jnp.exp2(st - m_new).astype(BF16)
                vaug = jnp.concatenate([vt_ref[0, kj0 + i, hl, 0:rows], jnp.ones((EXTRA_ROWS, rows), BF16)], axis=0)
                acc = acc + jnp.dot(vaug, pt, preferred_element_type=F32)
            acc_ref[h] = acc
            m_ref[h] = m_new

    n_full = (qi * tq) // tk

    def chunk_with_sample_sub(k, kj, masked):
        sample_begin(k)
        chunks(kj, 1, masked, between=lambda: sample_sub(k))
        sample_end(k)

    def body(i, carry):
        chunk_with_sample_sub(sub_base + i, i, False)
        return carry

    lax.fori_loop(0, n_full, body, 0)

    assert tk == 2 * tq
    tile_at_chunk_start = (qi * tq) % tk == 0

    @pl.when(tile_at_chunk_start)
    def _():
        chunks(n_full, 1, True, rows=tq)

    @pl.when(jnp.logical_not(tile_at_chunk_start))
    def _():
        chunk_with_sample_sub(sub_base + n_full, n_full, True)

    lam = _diff_lambda(lq1_ref[...], lk1_ref[...], lq2_ref[...], lk2_ref[...], lam_init)
    for h in range(n_hp):
        ot = acc_ref[h, 0:LANES, :] / acc_ref[h, LANES:LANES + 1, :]
        at = jnp.where(sub < D_V, ot[:, 0:tq] - lam * ot[:, tq:2 * tq], ot[:, 2 * tq:3 * tq] - lam * ot[:, 3 * tq:])
        a2 = at * at
        ss0 = jnp.sum(a2[0:D_V], axis=0, keepdims=True)
        ss1 = jnp.sum(a2[D_V:], axis=0, keepdims=True)
        ms = jnp.where(sub < D_V, ss0, ss1) * (1.0 / D_V)
        an = at * lax.rsqrt(ms + EPS) * g_ref[...] * (1.0 - lam_init)
        o_ref[0, :, h * LANES:(h + 1) * LANES] = an.T.astype(BF16)


def _attn(qtb, kb, vtb, page_table, qsb, ks, vs, cache_k, cache_v, lam_params, g_subln, lam_init,
          *, tq=256, pps=16):
    b, w, s = qtb.shape
    n_chunks, tk = vtb.shape[1], vtb.shape[3]
    n_hp = w // LANES
    hw = n_hp * LANES
    cols = 4 * tq
    bd, t_new, _ = qsb.shape
    n_pages = page_table.shape[1]
    n_pool = cache_k.shape[0]
    past = n_pages * PAGE_SIZE
    n_subs = bd * (n_pages // pps)
    nq = s // tq
    subs_per_batch = (nq // 2) ** 2
    assert tk == 2 * tq and nq % 2 == 0 and n_pages % pps == 0 and b * subs_per_batch == n_subs
    ck = jnp.transpose(cache_k, (0, 2, 3, 1)).reshape(n_pool, w, PAGE_SIZE)
    cv = jnp.transpose(cache_v, (0, 2, 3, 1)).reshape(n_pool, w, PAGE_SIZE)
    g8 = jnp.tile(g_subln, N_ATT_HEADS).reshape(1, w)
    srows = 2 * N_ATT_HEADS * t_new
    g2 = jnp.concatenate([g_subln, g_subln]).reshape(LANES, 1)
    pos = lax.broadcasted_iota(jnp.int32, (s, LANES), 0)
    lane = lax.broadcasted_iota(jnp.int32, (s, LANES), 1)
    n_parts = len(L2E_PARTS)
    pf = jnp.where(lane < n_parts, pos // POS_SPLIT,
                   jnp.where(lane < N_POS_FEATS, pos % POS_SPLIT, 0)).astype(BF16)
    rel = (jnp.arange(tk, dtype=jnp.int32)[:, None] - (jnp.arange(cols, dtype=jnp.int32) % tq)[None, :]).astype(F32)
    const2 = lambda bi, hp, qi, pt: (0, 0)
    const3 = lambda bi, hp, qi, pt: (0, 0, 0)
    lam_specs = [pl.BlockSpec((1, D_QK), const2)] * 4
    grid_spec = pltpu.PrefetchScalarGridSpec(
        num_scalar_prefetch=1,
        grid=(b, w // hw, s // tq),
        in_specs=[pl.BlockSpec((1, hw, tq), lambda bi, hp, qi, pt: (bi, hp, qi)),
                  pl.BlockSpec((1, s, hw), lambda bi, hp, qi, pt: (bi, 0, hp)),
                  pl.BlockSpec((1, n_chunks, hw, tk), lambda bi, hp, qi, pt: (bi, 0, hp, 0)),
                  pl.BlockSpec((s, LANES), const2),
                  pl.BlockSpec((tk, cols), const2),
                  *lam_specs,
                  pl.BlockSpec((LANES, 1), const2),
                  pl.BlockSpec((bd, t_new, w), const3),
                  pl.BlockSpec((bd, t_new, w), const3),
                  pl.BlockSpec((bd, t_new, w), const3),
                  pl.BlockSpec((1, w), const2),
                  pl.BlockSpec(memory_space=pl.ANY),
                  pl.BlockSpec(memory_space=pl.ANY)],
        out_specs=[pl.BlockSpec((1, tq, hw), lambda bi, hp, qi, pt: (bi, qi, hp)),
                   pl.BlockSpec((bd, t_new, w), const3)],
        scratch_shapes=[pltpu.VMEM((n_hp, 2 * LANES, cols), BF16),
                        pltpu.VMEM((n_hp, 1, cols), F32),
                        pltpu.VMEM((n_hp, LANES + EXTRA_ROWS, cols), F32),
                        pltpu.VMEM((srows, w), BF16),
                        pltpu.VMEM((srows, 1), F32),
                        pltpu.VMEM((srows, 1), F32),
                        pltpu.VMEM((srows, w), F32),
                        pltpu.VMEM((2, pps, w, PAGE_SIZE), F32),
                        pltpu.VMEM((2, pps, w, PAGE_SIZE), F32),
                        pltpu.SemaphoreType.DMA((2, 2))],
    )
    return pl.pallas_call(
        functools.partial(_attn_kernel, tq=tq, tk=tk, n_hp=n_hp, lam_init=lam_init,
                          pps=pps, subs_per_batch=subs_per_batch, n_subs=n_subs, past=past),
        grid_spec=grid_spec,
        out_shape=[jax.ShapeDtypeStruct((b, s, w), BF16), jax.ShapeDtypeStruct((bd, t_new, w), BF16)],
        compiler_params=_cparams("arbitrary", "arbitrary", "arbitrary"),
        name="attn",
    )(page_table, qtb, kb, vtb, pf, rel, *[p.reshape(1, D_QK) for p in lam_params], g2,
      qsb, ks, vs, g8, ck, cv)


def _outproj_kernel(a_ref, u_ref, vs_ref, x_ref, ga1_ref, sc2_ref, sh2_ref, wcat_ref, sbias_ref, wout_ref,
                    gpost_ref, gpre_ref, wr_cat_ref, br_ref, tril_ref, *rest, group_dispatch):
    if group_dispatch:
        x1_ref, pay_ref, info_ref, counts_ref, sg_ref, run_ref = rest
    else:
        x1_ref, h2_ref, gates_ref, sg_ref = rest
    tm = x_ref.shape[0]
    sgu_w = u_ref.shape[1]
    lane = lax.broadcasted_iota(jnp.int32, (CHUNK, LANES), 1)
    for ci in range(tm // CHUNK):
        rs = slice(ci * CHUNK, (ci + 1) * CHUNK)
        for gp in range(sgu_w // LANES):
            cs = slice(gp * LANES, (gp + 1) * LANES)
            vp = vs_ref[rs, cs]
            zero = jnp.zeros_like(vp)
            rhs = jnp.concatenate([jnp.where(lane < SGU_GC, vp, zero), jnp.where(lane >= SGU_GC, vp, zero)], axis=0)
            mixed = jnp.dot(wcat_ref[gp], rhs, preferred_element_type=F32) + sbias_ref[:, cs]
            sg_ref[rs, cs] = (u_ref[rs, cs].astype(F32) * mixed).astype(BF16)
    att_w = a_ref.shape[1]
    mix = (jnp.dot(a_ref[...], wout_ref[0:att_w, :], preferred_element_type=F32)
           + jnp.dot(sg_ref[...], wout_ref[att_w:, :], preferred_element_type=F32))
    x1 = x_ref[...] + ga1_ref[0] * _rms(mix, gpost_ref[...])
    x1_ref[...] = x1
    h2 = _rms(x1, gpre_ref[...]) * (1.0 + sc2_ref[0]) + sh2_ref[0]
    h2_hi = h2.astype(BF16)
    h2_lo = (h2 - h2_hi.astype(F32)).astype(BF16)
    hh = jnp.dot(h2_hi, wr_cat_ref[...], preferred_element_type=F32)
    logits = (hh[:, :LANES] + hh[:, LANES:]
              + jnp.dot(h2_lo, wr_cat_ref[:, :LANES], preferred_element_type=F32)) + br_ref[...]
    ln = lax.broadcasted_iota(jnp.int32, (tm, LANES), 1)
    big = jnp.int32(LANES)
    is_g = (ln >= N_EXPERTS) & (ln < N_EXPERTS + N_EXPERT_GROUPS)
    gl = jnp.where(is_g, logits, NEG_INF)
    gmax = jnp.max(gl, axis=-1, keepdims=True)
    g_idx = jnp.min(jnp.where(gl == gmax, ln, big), axis=-1, keepdims=True) - N_EXPERTS
    g_w = 1.0 / jnp.sum(jnp.exp(gl - gmax), axis=-1, keepdims=True)
    in_grp = (ln >= g_idx * EXPERTS_PER_GROUP) & (ln < (g_idx + 1) * EXPERTS_PER_GROUP)
    el = jnp.where(in_grp, logits, NEG_INF)
    emax = jnp.max(el, axis=-1, keepdims=True)
    ez = jnp.exp(el - emax)
    prob = ez / jnp.sum(ez, axis=-1, keepdims=True)
    p1 = jnp.max(prob, axis=-1, keepdims=True)
    i1 = jnp.min(jnp.where((prob == p1) & in_grp, ln, big), axis=-1, keepdims=True)
    rest = jnp.where(in_grp & (ln != i1), prob, -1.0)
    p2 = jnp.max(rest, axis=-1, keepdims=True)
    i2 = jnp.min(jnp.where(rest == p2, ln, big), axis=-1, keepdims=True)
    den = p1 + p2
    gates = jnp.where(ln == i1, p1 / den * g_w, 0.0) + jnp.where(ln == i2, p2 / den * g_w, 0.0)
    if not group_dispatch:
        h2_ref[...] = h2_hi
        gates_ref[...] = gates[:, :N_EXPERTS]
        return

    bits = pltpu.bitcast(h2_hi.astype(F32), jnp.uint32)
    half = PAYLOAD_H2_SEGS * LANES
    for s in range(PAYLOAD_H2_SEGS):
        lo = bits[:, s * LANES:(s + 1) * LANES]
        hi = bits[:, half + s * LANES:half + (s + 1) * LANES]
        pay_ref[s] = hi | lax.shift_right_logical(lo, jnp.uint32(16))
    pay_ref[PAYLOAD_H2_SEGS] = pltpu.bitcast(gates, jnp.uint32)

    @pl.when(pl.program_id(0) == 0)
    def _():
        run_ref[...] = jnp.zeros_like(run_ref)

    onehot = jnp.where(ln == g_idx, 1.0, 0.0).astype(BF16)
    csum = jnp.dot(tril_ref[...], onehot, preferred_element_type=F32) + run_ref[...]
    rank = jnp.sum(jnp.where(ln == g_idx, csum, 0.0), axis=-1, keepdims=True) - 1.0
    run_ref[...] = csum[tm - 1:tm, :]
    counts_ref[...] = csum[tm - 1:tm, :]
    info = jnp.where(ln == 0, g_idx, jnp.where(ln == 1, rank.astype(jnp.int32), 0))
    info_ref[...] = info[:, :INFO_W]


def _outproj(a, u, vsb, x, ga1, sc2, sh2, wcat, sbias, w_out_bf, g_post, g_pre, wr_cat, br,
             *, tm, tiles_per_mod, group_dispatch):
    n, d = x.shape
    att_w = a.shape[1]
    sgu_w = u.shape[1]
    r = ga1.shape[1]
    row = lambda i: (i, 0)
    modmap = lambda i: (i // tiles_per_mod, 0, 0)
    const = lambda i: (0, 0)
    tril = jnp.tril(jnp.ones((tm, tm), BF16))
    if group_dispatch:
        out_specs = [pl.BlockSpec((tm, d), row),
                     pl.BlockSpec((PAYLOAD_SEGS, tm, LANES), lambda i: (0, i, 0)),
                     pl.BlockSpec((tm, INFO_W), row),
                     pl.BlockSpec((1, LANES), const)]
        out_shape = [jax.ShapeDtypeStruct((n, d), F32),
                     jax.ShapeDtypeStruct((PAYLOAD_SEGS, n, LANES), jnp.uint32),
                     jax.ShapeDtypeStruct((n, INFO_W), jnp.int32),
                     jax.ShapeDtypeStruct((1, LANES), F32)]
        scratch = [pltpu.VMEM((tm, sgu_w), BF16), pltpu.VMEM((1, LANES), F32)]
    else:
        out_specs = [pl.BlockSpec((tm, d), row),
                     pl.BlockSpec((tm, d), row),
                     pl.BlockSpec((tm, N_EXPERTS), row)]
        out_shape = [jax.ShapeDtypeStruct((n, d), F32),
                     jax.ShapeDtypeStruct((n, d), BF16),
                     jax.ShapeDtypeStruct((n, N_EXPERTS), F32)]
        scratch = [pltpu.VMEM((tm, sgu_w), BF16)]
    return pl.pallas_call(
        functools.partial(_outproj_kernel, group_dispatch=group_dispatch),
        grid=(n // tm,),
        in_specs=[pl.BlockSpec((tm, att_w), row),
                  pl.BlockSpec((tm, sgu_w), row),
                  pl.BlockSpec((tm, sgu_w), row),
                  pl.BlockSpec((tm, d), row),
                  pl.BlockSpec((1, r, d), modmap),
                  pl.BlockSpec((1, r, d), modmap),
                  pl.BlockSpec((1, r, d), modmap),
                  pl.BlockSpec(wcat.shape, lambda i: (0, 0, 0)),
                  pl.BlockSpec(sbias.shape, const),
                  pl.BlockSpec(w_out_bf.shape, const),
                  pl.BlockSpec((1, d), const),
                  pl.BlockSpec((1, d), const),
                  pl.BlockSpec(wr_cat.shape, const),
                  pl.BlockSpec((1, LANES), const),
                  pl.BlockSpec((tm, tm), const)],
        out_specs=out_specs,
        out_shape=out_shape,
        scratch_shapes=scratch,
        compiler_params=_cparams("arbitrary"),
        name="outproj",
    )(a, u, vsb, x, ga1, sc2, sh2, wcat, sbias, w_out_bf, g_post.reshape(1, d), g_pre.reshape(1, d),
      wr_cat, br, tril)


def _moe_kernel(h_ref, gates_ref, x1_ref, ga2_ref, gpost_ref, wg_ref, wu_ref, wd_ref, o_ref, acc_ref):
    e = pl.program_id(1)

    @pl.when(e == 0)
    def _():
        acc_ref[...] = jnp.zeros_like(acc_ref)

    h = h_ref[...]
    gate = jnp.dot(h, wg_ref[0], preferred_element_type=F32)
    up = jnp.dot(h, wu_ref[0], preferred_element_type=F32)
    hdn = (gate * jax.nn.sigmoid(gate) * up).astype(BF16)
    y = jnp.dot(hdn, wd_ref[0], preferred_element_type=F32)
    ln = lax.broadcasted_iota(jnp.int32, gates_ref.shape, 1)
    ge = jnp.sum(jnp.where(ln == e, gates_ref[...], 0.0), axis=-1, keepdims=True)
    acc_ref[...] += ge * y

    @pl.when(e == pl.num_programs(1) - 1)
    def _():
        o_ref[...] = x1_ref[...] + ga2_ref[0] * _rms(acc_ref[...], gpost_ref[...])


def _moe(h2, gates, x1, ga2, g_post, wg_bf, wu_bf, wd_bf, *, tm, tiles_per_mod):
    n, d = x1.shape
    n_e, _, de = wg_bf.shape
    r = ga2.shape[1]
    row = lambda i, e: (i, 0)
    return pl.pallas_call(
        _moe_kernel,
        grid=(n // tm, n_e),
        in_specs=[pl.BlockSpec((tm, d), row),
                  pl.BlockSpec((tm, n_e), row),
                  pl.BlockSpec((tm, d), row),
                  pl.BlockSpec((1, r, d), lambda i, e: (i // tiles_per_mod, 0, 0)),
                  pl.BlockSpec((1, d), lambda i, e: (0, 0)),
                  pl.BlockSpec((1, d, de), lambda i, e: (e, 0, 0)),
                  pl.BlockSpec((1, d, de), lambda i, e: (e, 0, 0)),
                  pl.BlockSpec((1, de, d), lambda i, e: (e, 0, 0))],
        out_specs=pl.BlockSpec((tm, d), row),
        out_shape=jax.ShapeDtypeStruct((n, d), F32),
        scratch_shapes=[pltpu.VMEM((tm, d), F32)],
        compiler_params=_cparams("arbitrary", "arbitrary"),
        name="moe",
    )(h2, gates, x1, ga2, g_post.reshape(1, d), wg_bf, wu_bf, wd_bf)


def _sc_mesh():
    return plsc.VectorSubcoreMesh(core_axis_name="core", subcore_axis_name="subcore")


def _sc_scatter_rows(x, idx, n_out_rows):
    n_rows, width = x.shape

    @pl.kernel(out_type=jax.ShapeDtypeStruct((n_out_rows, width), x.dtype), mesh=_sc_mesh(), scratch_types=[])
    def scatter(x_hbm, i_hbm, o_hbm):
        def body(x_vmem, i_vmem):
            pltpu.sync_copy(x_vmem, o_hbm.at[i_vmem.at[0]])

        pltpu.emit_pipeline(
            body,
            grid=(n_rows // SC_WINDOW,),
            in_specs=[pl.BlockSpec((SC_WINDOW, width), lambda i: (i, 0)),
                      pl.BlockSpec((1, SC_WINDOW), lambda i: (0, i))],
            out_specs=[],
            core_axis_name=("core", "subcore"),
            dimension_semantics=(pltpu.PARALLEL,),
        )(x_hbm, i_hbm)

    return scatter(x, idx.reshape(1, n_rows))


def _sc_gather_rows(x, idx):
    n_rows = idx.shape[0]
    width = x.shape[1]

    @pl.kernel(out_type=jax.ShapeDtypeStruct((n_rows, width), x.dtype), mesh=_sc_mesh(), scratch_types=[])
    def gather(x_hbm, i_hbm, o_hbm):
        def body(i_vmem, o_vmem):
            pltpu.sync_copy(x_hbm.at[i_vmem.at[0]], o_vmem)

        pltpu.emit_pipeline(
            body,
            grid=(n_rows // SC_WINDOW,),
            in_specs=[pl.BlockSpec((1, SC_WINDOW), lambda i: (0, i))],
            out_specs=[pl.BlockSpec((SC_WINDOW, width), lambda i: (i, 0))],
            core_axis_name=("core", "subcore"),
            dimension_semantics=(pltpu.PARALLEL,),
        )(i_hbm, o_hbm)

    return gather(x, idx.reshape(1, n_rows))


def _group_moe_kernel(tg_ref, nused_ref, xs_ref, gpost_ref, wg_ref, wu_ref, wd_ref, rs_ref):
    t = pl.program_id(0)
    tm = xs_ref.shape[1]

    @pl.when(t >= nused_ref[0])
    def _():
        rs_ref[...] = jnp.zeros_like(rs_ref)

    @pl.when(t < nused_ref[0])
    def _():
        words = [xs_ref[s] for s in range(PAYLOAD_H2_SEGS)]
        lo = [pltpu.bitcast(lax.shift_left(w, jnp.uint32(16)), F32) for w in words]
        hi = [pltpu.bitcast(w & jnp.uint32(0xFFFF0000), F32) for w in words]
        x = jnp.concatenate(lo + hi, axis=1).astype(BF16)
        gates = pltpu.bitcast(xs_ref[PAYLOAD_H2_SEGS], F32)
        ln = lax.broadcasted_iota(jnp.int32, (tm, LANES), 1)
        first = tg_ref[t] * EXPERTS_PER_GROUP
        y = jnp.zeros((tm, x.shape[1]), F32)
        for el in range(EXPERTS_PER_GROUP):
            ge = jnp.sum(jnp.where(ln == first + el, gates, 0.0), axis=-1, keepdims=True)
            gate = jnp.dot(x, wg_ref[el], preferred_element_type=F32)
            up = jnp.dot(x, wu_ref[el], preferred_element_type=F32)
            hdn = (gate * jax.nn.sigmoid(gate) * up).astype(BF16)
            y = y + ge * jnp.dot(hdn, wd_ref[el], preferred_element_type=F32)
        r = _rms(y, gpost_ref[...])
        for s in range(rs_ref.shape[0]):
            rs_ref[s] = r[:, s * LANES:(s + 1) * LANES]


def _group_moe(xs, tile_group, n_used, g_post, wg_bf, wu_bf, wd_bf, *, tm):
    _, p_rows, _ = xs.shape
    n_e, d, de = wg_bf.shape
    g = EXPERTS_PER_GROUP
    grid_spec = pltpu.PrefetchScalarGridSpec(
        num_scalar_prefetch=2,
        grid=(p_rows // tm,),
        in_specs=[pl.BlockSpec((PAYLOAD_SEGS, tm, LANES), lambda t, tg, nu: (0, t, 0)),
                  pl.BlockSpec((1, d), lambda t, tg, nu: (0, 0)),
                  pl.BlockSpec((g, d, de), lambda t, tg, nu: (tg[t], 0, 0)),
                  pl.BlockSpec((g, d, de), lambda t, tg, nu: (tg[t], 0, 0)),
                  pl.BlockSpec((g, de, d), lambda t, tg, nu: (tg[t], 0, 0))],
        out_specs=pl.BlockSpec((d // LANES, tm, LANES), lambda t, tg, nu: (0, t, 0)),
    )
    return pl.pallas_call(
        _group_moe_kernel,
        grid_spec=grid_spec,
        out_shape=jax.ShapeDtypeStruct((d // LANES, p_rows, LANES), F32),
        compiler_params=_cparams("arbitrary"),
        name="group_moe",
    )(tile_group, n_used, xs, g_post.reshape(1, d), wg_bf, wu_bf, wd_bf)


def _residual_kernel(x1_ref, ga2_ref, r_ref, o_ref):
    r = jnp.concatenate([r_ref[s] for s in range(r_ref.shape[0])], axis=1)
    o_ref[...] = x1_ref[...] + ga2_ref[0] * r


def _residual(x1, ga2, r_segs, *, tm, tiles_per_mod):
    n, d = x1.shape
    return pl.pallas_call(
        _residual_kernel,
        grid=(n // tm,),
        in_specs=[pl.BlockSpec((tm, d), lambda i: (i, 0)),
                  pl.BlockSpec((1, ga2.shape[1], d), lambda i: (i // tiles_per_mod, 0, 0)),
                  pl.BlockSpec((d // LANES, tm, LANES), lambda i: (0, i, 0))],
        out_specs=pl.BlockSpec((tm, d), lambda i: (i, 0)),
        out_shape=jax.ShapeDtypeStruct((n, d), F32),
        compiler_params=_cparams("arbitrary"),
        name="residual",
    )(x1, ga2, r_segs)


def _group_plan(info, counts, n_tokens, tm):
    n_tiles = n_tokens // tm + N_EXPERT_GROUPS
    cnt = counts[0, :N_EXPERT_GROUPS].astype(jnp.int32)
    padded = (cnt + tm - 1) // tm * tm
    ends = jnp.cumsum(padded)
    starts = ends - padded
    gid, rank = info[:, 0], info[:, 1]
    onehot = (gid[:, None] == jnp.arange(N_EXPERT_GROUPS, dtype=jnp.int32)[None, :]).astype(jnp.int32)
    pos = jnp.sum(onehot * starts[None, :], axis=1) + rank
    tile_ends = ends // tm
    t = jnp.arange(n_tiles, dtype=jnp.int32)
    tile_group = jnp.minimum(jnp.sum((t[:, None] >= tile_ends[None, :]).astype(jnp.int32), axis=1),
                             N_EXPERT_GROUPS - 1)
    return pos, tile_group, tile_ends[-1:], n_tiles * tm


def _sgu_weights(ws, bs, chunk_len):
    causal = jnp.tril(jnp.ones((chunk_len, chunk_len), ws.dtype))
    wm = ws[:, :chunk_len, :chunk_len] * causal
    reps = CHUNK // chunk_len
    if reps > 1:
        eye = jnp.eye(reps, dtype=ws.dtype)
        wm = jnp.einsum("ab,gts->gatbs", eye, wm).reshape(N_SGU_GROUPS, CHUNK, CHUNK)
    wcat = jnp.concatenate([wm[0::2], wm[1::2]], axis=2).astype(BF16)
    bt = jnp.tile(bs[:, :chunk_len].T, (reps, 1))
    sbias = jnp.repeat(bt, SGU_GC, axis=1)
    return wcat, sbias


def _split_mod(mod):
    return [m[:, None, :] for m in jnp.split(mod, 6, axis=-1)]


def kernel(x_prompt, x_sample, cache_k, cache_v, page_table, c_prompt, c_sample, w_ada, b_ada, g_pre_mix, g_post_mix, g_pre_ffn, g_post_ffn, w_in, lam_q1, lam_k1, lam_q2, lam_k2, g_subln, sgu_ln_g, sgu_ln_b, sgu_ws, sgu_bs, w_out, w_rg, b_rg, w_re, b_re, w_gate, w_up, w_down):
    depth = w_in.shape[0]
    assert depth == 1
    l = 0
    lam_init = 0.8 - 0.6 * math.exp(-0.3 * l)
    bp, sp, d = x_prompt.shape
    bs_, ts, _ = x_sample.shape
    n_s = bs_ * ts

    mod = _ada(jnp.concatenate([c_prompt, c_sample], axis=0), w_ada[l], b_ada[l])
    sh1p, sc1p, ga1p, sh2p, sc2p, ga2p = _split_mod(mod[:bp])
    rep = lambda m: jnp.repeat(m, ts, axis=0).reshape(1, n_s, d)
    sh1s, sc1s, ga1s, sh2s, sc2s, ga2s = [rep(m) for m in jnp.split(mod[bp:], 6, axis=-1)]

    w_in_bf = w_in[l].astype(BF16)
    wqv_t_bf = jnp.concatenate([w_in_bf[:, 0:ATT_W], w_in_bf[:, 2 * ATT_W:3 * ATT_W]], axis=1).T
    w_out_bf = w_out[l].astype(BF16)
    wg_bf = w_gate[l].astype(BF16)
    wu_bf = w_up[l].astype(BF16)
    wd_bf = w_down[l].astype(BF16)
    wr = jnp.concatenate([w_re[l], w_rg[l]], axis=1)
    wr = jnp.pad(wr, ((0, 0), (0, LANES - wr.shape[1])))
    wr_hi = wr.astype(BF16)
    wr_cat = jnp.concatenate([wr_hi, (wr - wr_hi.astype(F32)).astype(BF16)], axis=1)
    br = jnp.pad(jnp.concatenate([b_re[l], b_rg[l]]), (0, LANES - N_EXPERTS - N_EXPERT_GROUPS)).reshape(1, LANES)
    lam_params = (lam_q1[l], lam_k1[l], lam_q2[l], lam_k2[l])

    tm_p = 512
    xp = x_prompt.reshape(bp * sp, d)
    ktp, vtp, qtb, kb, vtb, u, vsb, vs_last = _inproj(
        xp, sc1p, sh1p, g_pre_mix[l], w_in_bf, wqv_t_bf, sgu_ln_g[l], sgu_ln_b[l],
        tm=tm_p, tiles_per_seq=sp // tm_p, last_rows=CHUNK, transposed_qv=True)
    w3 = ATT_W
    xs = x_sample.reshape(n_s, d)
    ks, vs_, qbs, us, vsbs, vs_last_s = _inproj(
        xs, sc1s, sh1s, g_pre_mix[l], w_in_bf, wqv_t_bf, sgu_ln_g[l], sgu_ln_b[l],
        tm=n_s, tiles_per_seq=1, last_rows=n_s, transposed_qv=False)
    a, a_s = _attn(qtb, kb.reshape(bp, sp, w3), vtb, page_table, qbs.reshape(bs_, ts, w3),
                   ks.reshape(bs_, ts, w3), vs_.reshape(bs_, ts, w3), cache_k[l], cache_v[l],
                   lam_params, g_subln[l], lam_init)
    kp = jnp.transpose(ktp.reshape(bp, N_ATT_HEADS, 2 * D_QK, sp), (0, 3, 1, 2))
    vp = jnp.transpose(vtp.reshape(bp, N_ATT_HEADS, D_V, sp), (0, 3, 1, 2))
    wcat_p, sbias_p = _sgu_weights(sgu_ws[l], sgu_bs[l], CHUNK)
    n_p = bp * sp
    x1, payload, info, counts = _outproj(
        a.reshape(n_p, w3), u, vsb, xp, ga1p, sc2p, sh2p, wcat_p, sbias_p, w_out_bf,
        g_post_mix[l], g_pre_ffn[l], wr_cat, br, tm=tm_p, tiles_per_mod=sp // tm_p, group_dispatch=True)
    tm_moe = 512
    pos, tile_group, n_used, p_rows = _group_plan(info, counts, n_p, tm_moe)
    seg_p = lambda k: (jnp.arange(k, dtype=jnp.int32)[:, None] * p_rows + pos[None, :]).reshape(-1)
    sorted_rows = _sc_scatter_rows(payload.reshape(PAYLOAD_SEGS * n_p, LANES), seg_p(PAYLOAD_SEGS),
                                   PAYLOAD_SEGS * p_rows)
    rs = _group_moe(sorted_rows.reshape(PAYLOAD_SEGS, p_rows, LANES), tile_group, n_used, g_post_ffn[l],
                    wg_bf, wu_bf, wd_bf, tm=tm_moe)
    n_seg = d // LANES
    r_tok = _sc_gather_rows(rs.reshape(n_seg * p_rows, LANES), seg_p(n_seg))
    yp = _residual(x1, ga2p, r_tok.reshape(n_seg, n_p, LANES), tm=tm_p, tiles_per_mod=sp // tm_p)

    wcat_s, sbias_s = _sgu_weights(sgu_ws[l], sgu_bs[l], ts)
    x1s, h2s, gates_s = _outproj(a_s.reshape(n_s, w3), us, vsbs, xs, ga1s, sc2s, sh2s, wcat_s, sbias_s, w_out_bf,
                                 g_post_mix[l], g_pre_ffn[l], wr_cat, br, tm=n_s, tiles_per_mod=1,
                                 group_dispatch=False)
    ys = _moe(h2s, gates_s, x1s, ga2s, g_post_ffn[l], wg_bf, wu_bf, wd_bf, tm=n_s, tiles_per_mod=1)

    return (yp.reshape(bp, sp, d), ys.reshape(bs_, ts, d),
            kp[None], vp[None],
            vs_last.reshape(1, bp, CHUNK, -1),
            ks.reshape(1, bs_, ts, N_ATT_HEADS, 2 * D_QK), vs_.reshape(1, bs_, ts, N_ATT_HEADS, D_V),
            vs_last_s.reshape(1, bs_, ts, -1))
```

```python
import functools
import math
import struct

import jax
import jax.numpy as jnp
from jax import lax
from jax.experimental import pallas as pl
from jax.experimental.pallas import tpu as pltpu
from jax.experimental.pallas import tpu_sc as plsc

F32 = jnp.float32
BF16 = jnp.bfloat16

EPS = 1e-6
N_ATT_HEADS = 8
D_QK = 32
D_V = 64
ATT_W = N_ATT_HEADS * D_V
N_SGU_GROUPS = 8
SGU_GC = 64
SGU_W = N_SGU_GROUPS * SGU_GC
CHUNK = 128
N_EXPERT_GROUPS = 4
EXPERTS_PER_GROUP = 4
N_EXPERTS = 16
PAGE_SIZE = 128
LANES = 128
VMEM_LIMIT = 56 * 1024 * 1024
NEG_INF = float("-inf")
LOG2E = math.log2(math.e)
PAYLOAD_H2_SEGS = 4
PAYLOAD_SEGS = PAYLOAD_H2_SEGS + 1
INFO_W = 8
SC_WINDOW = 128
PAGE_SLOTS = 3


def _cparams(*sem):
    return pltpu.CompilerParams(dimension_semantics=sem, vmem_limit_bytes=VMEM_LIMIT)


def _rms(x, g):
    return x * lax.rsqrt(jnp.mean(x * x, axis=-1, keepdims=True) + EPS) * g


def _ada_kernel(c_ref, w_ref, b_ref, o_ref):
    c = c_ref[...]
    s = c * jax.nn.sigmoid(c)
    o_ref[...] = jnp.dot(s.astype(BF16), w_ref[...].astype(BF16), preferred_element_type=F32) + b_ref[...]


def _ada(c, w_ada, b_ada):
    n, d = c.shape
    nout = w_ada.shape[1]
    tn = d
    return pl.pallas_call(
        _ada_kernel,
        grid=(nout // tn,),
        in_specs=[pl.BlockSpec((n, d), lambda j: (0, 0)),
                  pl.BlockSpec((d, tn), lambda j: (0, j)),
                  pl.BlockSpec((1, tn), lambda j: (0, j))],
        out_specs=pl.BlockSpec((n, tn), lambda j: (0, j)),
        out_shape=jax.ShapeDtypeStruct((n, nout), F32),
        compiler_params=_cparams("arbitrary"),
        name="ada",
    )(c, w_ada, b_ada.reshape(1, nout))


Q_SCALE = D_QK ** -0.5 * LOG2E


def _inproj_kernel(x_ref, sc_ref, sh_ref, g_ref, w_ref, wqv_t_ref, lng_ref, lnb_ref, *out_refs,
                   last_rows, transposed_qv):
    x = x_ref[...]
    h = (_rms(x, g_ref[...]) * (1.0 + sc_ref[0]) + sh_ref[0]).astype(BF16)

    def proj(lo):
        return jnp.dot(h, w_ref[:, lo:lo + ATT_W], preferred_element_type=F32)

    def proj_t(lo):
        return lax.dot_general(wqv_t_ref[lo:lo + ATT_W, :], h, (((1,), (1,)), ((), ())),
                               preferred_element_type=F32)

    zk = proj(ATT_W)
    if transposed_qv:
        kt_ref, vt_ref, qtb_ref, kb_ref, vtb_ref, u_ref, vsb_ref, vsl_ref = out_refs
        kt_ref[0] = zk.T
        kb_ref[...] = zk.astype(BF16)
        qtb_ref[0] = (proj_t(0) * Q_SCALE).astype(BF16)
        zvt = proj_t(ATT_W)
        vt_ref[0] = zvt
        vtb_ref[0, 0] = zvt.astype(BF16)
    else:
        k_ref, v_ref, qb_ref, u_ref, vsb_ref, vsl_ref = out_refs
        k_ref[...] = zk
        qb_ref[...] = (proj(0) * Q_SCALE).astype(BF16)
        v_ref[...] = proj(2 * ATT_W)
    u_ref[...] = jax.nn.gelu(proj(3 * ATT_W)).astype(BF16)
    gs = jax.nn.gelu(proj(3 * ATT_W + SGU_W))
    mu = jnp.mean(gs, axis=-1, keepdims=True)
    xc = gs - mu
    vs = xc * lax.rsqrt(jnp.mean(xc * xc, axis=-1, keepdims=True) + EPS) * lng_ref[...] + lnb_ref[...]
    vsb_ref[...] = vs.astype(BF16)
    tm = vs.shape[0]
    vsl_ref[0] = vs[tm - last_rows:, :]


def _inproj(x, sc, sh, g, w_in_bf, wqv_t_bf, ln_g, ln_b, *, tm, tiles_per_seq, last_rows, transposed_qv):
    n, d = x.shape
    in_w = w_in_bf.shape[1]
    r = sc.shape[1]
    n_tiles = n // tm
    n_seq = n_tiles // tiles_per_seq
    s_len = tiles_per_seq * tm
    row = lambda i: (i, 0)
    modmap = lambda i: (i // tiles_per_seq, 0, 0)
    const = lambda i: (0, 0)
    col_t = lambda i: (i // tiles_per_seq, 0, i % tiles_per_seq)
    row_spec = pl.BlockSpec((tm, ATT_W), row)
    tail = [(pl.BlockSpec((tm, SGU_W), row), jax.ShapeDtypeStruct((n, SGU_W), BF16)),
            (pl.BlockSpec((tm, SGU_W), row), jax.ShapeDtypeStruct((n, SGU_W), BF16)),
            (pl.BlockSpec((1, last_rows, SGU_W), lambda i: (i // tiles_per_seq, 0, 0)),
             jax.ShapeDtypeStruct((n_seq, last_rows, SGU_W), F32))]
    if transposed_qv:
        head = [(pl.BlockSpec((1, ATT_W, tm), col_t), jax.ShapeDtypeStruct((n_seq, ATT_W, s_len), F32)),
                (pl.BlockSpec((1, ATT_W, tm), col_t), jax.ShapeDtypeStruct((n_seq, ATT_W, s_len), F32)),
                (pl.BlockSpec((1, ATT_W, tm), col_t), jax.ShapeDtypeStruct((n_seq, ATT_W, s_len), BF16)),
                (row_spec, jax.ShapeDtypeStruct((n, ATT_W), BF16)),
                (pl.BlockSpec((1, 1, ATT_W, tm), lambda i: (i // tiles_per_seq, i % tiles_per_seq, 0, 0)),
                 jax.ShapeDtypeStruct((n_seq, tiles_per_seq, ATT_W, tm), BF16))]
    else:
        head = [(row_spec, jax.ShapeDtypeStruct((n, ATT_W), F32)),
                (row_spec, jax.ShapeDtypeStruct((n, ATT_W), F32)),
                (row_spec, jax.ShapeDtypeStruct((n, ATT_W), BF16))]
    specs, shapes = zip(*(head + tail))
    return pl.pallas_call(
        functools.partial(_inproj_kernel, last_rows=last_rows, transposed_qv=transposed_qv),
        grid=(n_tiles,),
        in_specs=[pl.BlockSpec((tm, d), row),
                  pl.BlockSpec((1, r, d), modmap),
                  pl.BlockSpec((1, r, d), modmap),
                  pl.BlockSpec((1, d), const),
                  pl.BlockSpec((d, in_w), const),
                  pl.BlockSpec(wqv_t_bf.shape, const),
                  pl.BlockSpec((1, SGU_W), const),
                  pl.BlockSpec((1, SGU_W), const)],
        out_specs=list(specs),
        out_shape=list(shapes),
        compiler_params=_cparams("arbitrary"),
        name="inproj",
    )(x, sc, sh, g.reshape(1, d), w_in_bf, wqv_t_bf, ln_g.reshape(1, SGU_W), ln_b.reshape(1, SGU_W))


def _diff_lambda(lq1, lk1, lq2, lk2, lam_init):
    return (jnp.exp(jnp.sum(lq1 * lk1, axis=-1, keepdims=True))
            - jnp.exp(jnp.sum(lq2 * lk2, axis=-1, keepdims=True)) + lam_init)


def _pair_slopes(hp):
    s0 = jnp.where(hp == 0, 2.0 ** -1, jnp.where(hp == 1, 2.0 ** -3, jnp.where(hp == 2, 2.0 ** -5, 2.0 ** -7)))
    return s0.astype(F32), (s0 * 0.5).astype(F32)


def _bf16_round(x):
    bits = struct.unpack("<I", struct.pack("<f", x))[0]
    bits = (bits + 0x7FFF + ((bits >> 16) & 1)) & 0xFFFF0000
    return struct.unpack("<f", struct.pack("<I", bits))[0]


L2E_PARTS = []
_rest = LOG2E
for _ in range(3):
    L2E_PARTS.append(_bf16_round(_rest))
    _rest -= L2E_PARTS[-1]
POS_SPLIT = 64
N_POS_FEATS = 2 * len(L2E_PARTS)
EXTRA_ROWS = 16


def _attn_kernel(pt_ref, qt_ref, k_ref, vt_ref, pf_ref, rel_ref, lq1_ref, lk1_ref, lq2_ref, lk2_ref, g_ref,
                 qs_ref, kn_ref, vn_ref, g8_ref, ck_hbm, cv_hbm, o_ref, os_ref,
                 qst_ref, m_ref, acc_ref, qbd_ref, ms_ref, ls_ref, accs_ref, kbuf_ref, vbuf_ref, sem_ref,
                 *, tq, tk, n_hp, lam_init, pps, subs_per_batch, n_subs, past):
    qi = pl.program_id(2)
    cols = 4 * tq

    t_new, w = qs_ref.shape[1], qs_ref.shape[2]
    step = pl.program_id(0) * pl.num_programs(2) + qi
    subs_per_seq = past // (pps * PAGE_SIZE)
    half_q = qi // 2
    sub_base = pl.program_id(0) * subs_per_batch + half_q * half_q + (qi % 2) * half_q
    n_hj = 2 * N_ATT_HEADS
    srows = n_hj * t_new
    blk = pps * PAGE_SIZE
    srow_id = lax.broadcasted_iota(jnp.int32, (srows, 1), 0)
    shead = srow_id // (2 * t_new)
    sslope = jnp.zeros((srows, 1), F32)
    for h in range(N_ATT_HEADS):
        sslope = jnp.where(shead == h, 2.0 ** -(h + 1), sslope)
    sslope = sslope * LOG2E
    sqpos = (past + srow_id % t_new).astype(F32)

    def page_copies(k, slot):
        kc = jnp.minimum(k, n_subs - 1)
        seq_i, sub_i = kc // subs_per_seq, kc % subs_per_seq
        copies = []
        for i in range(pps):
            page = pt_ref[seq_i, sub_i * pps + i]
            copies.append(pltpu.make_async_copy(ck_hbm.at[page], kbuf_ref.at[slot, i], sem_ref.at[0, slot]))
            copies.append(pltpu.make_async_copy(cv_hbm.at[page], vbuf_ref.at[slot, i], sem_ref.at[1, slot]))
        return copies

    def sample_update(s, vmat, v_contract_dim):
        m_old = ms_ref[...]
        m_new = jnp.maximum(m_old, jnp.max(s, axis=-1, keepdims=True))
        p = jnp.exp2(s - m_new)
        corr = jnp.exp2(m_old - m_new)
        ls_ref[...] = corr * ls_ref[...] + jnp.sum(p, axis=-1, keepdims=True)
        pv = lax.dot_general(p.astype(vmat.dtype), vmat, (((1,), (v_contract_dim,)), ((), ())),
                             preferred_element_type=F32)
        accs_ref[...] = corr * accs_ref[...] + pv
        ms_ref[...] = m_new

    def sample_begin(k):
        @pl.when(k % subs_per_seq == 0)
        def _():
            q = qs_ref[k // subs_per_seq]
            lane = lax.broadcasted_iota(jnp.int32, (t_new, w), 1)
            for hj in range(n_hj):
                qbd_ref[hj * t_new:(hj + 1) * t_new, :] = jnp.where(
                    (lane >= D_QK * hj) & (lane < D_QK * (hj + 1)), q, jnp.zeros_like(q))
            ms_ref[...] = jnp.full((srows, 1), NEG_INF, F32)
            ls_ref[...] = jnp.zeros((srows, 1), F32)
            accs_ref[...] = jnp.zeros((srows, w), F32)

    def sample_sub(k):
        slot = k % PAGE_SLOTS
        for c in page_copies(k, slot):
            c.wait()
        ahead = k + PAGE_SLOTS - 1
        for c in page_copies(ahead, ahead % PAGE_SLOTS):
            c.start()
        kt_all = jnp.concatenate([kbuf_ref[slot, i].astype(BF16) for i in range(pps)], axis=1)
        vt_all = jnp.concatenate([vbuf_ref[slot, i].astype(BF16) for i in range(pps)], axis=1)
        s = jnp.dot(qbd_ref[...], kt_all, preferred_element_type=F32)
        kpos = ((k % subs_per_seq) * blk + lax.broadcasted_iota(jnp.int32, (1, blk), 1)).astype(F32)
        sample_update(s - sslope * (sqpos - kpos), vt_all, 1)

    @pl.when(step == 0)
    def _():
        for k0 in range(PAGE_SLOTS - 1):
            for c in page_copies(k0, k0):
                c.start()

    def sample_end(k):
        @pl.when(k % subs_per_seq == subs_per_seq - 1)
        def _():
            sample_finish(k // subs_per_seq)

        @pl.when(k == n_subs - 1)
        def _():
            for extra in range(n_subs, n_subs + PAGE_SLOTS - 1):
                for c in page_copies(extra, extra % PAGE_SLOTS):
                    c.wait()

    def sample_finish(seq):
        kn = kn_ref[seq]
        sn = lax.dot_general(qbd_ref[...].astype(F32), kn, (((1,), (1,)), ((), ())), preferred_element_type=F32)
        kposn = (past + lax.broadcasted_iota(jnp.int32, (1, t_new), 1)).astype(F32)
        dist = sqpos - kposn
        sample_update(jnp.where(dist >= 0, sn - sslope * dist, NEG_INF), vn_ref[seq], 0)
        lam_s = _diff_lambda(lq1_ref[...], lk1_ref[...], lq2_ref[...], lk2_ref[...], lam_init)
        o = accs_ref[...] / ls_ref[...]
        lane = lax.broadcasted_iota(jnp.int32, (t_new, w), 1)
        a = jnp.zeros((t_new, w), F32)
        for h in range(N_ATT_HEADS):
            o1 = o[(2 * h) * t_new:(2 * h + 1) * t_new]
            o2 = o[(2 * h + 1) * t_new:(2 * h + 2) * t_new]
            a = jnp.where((lane >= D_V * h) & (lane < D_V * (h + 1)), o1 - lam_s * o2, a)
        a2 = a * a
        msq = jnp.zeros((t_new, w), F32)
        for h in range(N_ATT_HEADS):
            in_h = (lane >= D_V * h) & (lane < D_V * (h + 1))
            ssh = jnp.sum(jnp.where(in_h, a2, 0.0), axis=-1, keepdims=True)
            msq = jnp.where(in_h, ssh, msq)
        os_ref[seq] = (a * lax.rsqrt(msq * (1.0 / D_V) + EPS) * g8_ref[...] * (1.0 - lam_init)).astype(BF16)

    sub = lax.broadcasted_iota(jnp.int32, (LANES, tq), 0)
    col = lax.broadcasted_iota(jnp.int32, (EXTRA_ROWS, cols), 1)
    row = lax.broadcasted_iota(jnp.int32, (EXTRA_ROWS, cols), 0)
    feat = jnp.zeros((EXTRA_ROWS, cols), F32)
    for i, part in enumerate(L2E_PARTS):
        feat = jnp.where(row == i, POS_SPLIT * part, feat)
        feat = jnp.where(row == len(L2E_PARTS) + i, part, feat)
    for h in range(n_hp):
        qt = qt_ref[0, h * LANES:(h + 1) * LANES, :]
        for c in range(4):
            qst_ref[h, 0:LANES, c * tq:(c + 1) * tq] = jnp.where((sub >= D_QK * c) & (sub < D_QK * (c + 1)), qt,
                                                                 jnp.zeros_like(qt))
        s0, s1 = _pair_slopes(pl.program_id(1) * n_hp + h)
        qst_ref[h, LANES:LANES + EXTRA_ROWS, :] = (jnp.where(col < 2 * tq, s0, s1) * feat).astype(BF16)
        qst_ref[h, LANES + EXTRA_ROWS:, :] = jnp.zeros((LANES - EXTRA_ROWS, cols), BF16)
    m_ref[...] = jnp.full(m_ref.shape, NEG_INF, F32)
    acc_ref[...] = jnp.zeros(acc_ref.shape, F32)

    def chunks(kj0, n_sub, masked, rows=tk, between=None):
        def scores(h):
            hl = slice(h * LANES, (h + 1) * LANES)
            sts = []
            for i in range(n_sub):
                start = pl.multiple_of((kj0 + i) * tk, tk)
                kaug = jnp.concatenate([k_ref[0, pl.ds(start, rows), hl], pf_ref[pl.ds(start, rows), :]], axis=1)
                st = jnp.dot(kaug, qst_ref[h], preferred_element_type=F32)
                if masked:
                    off = ((kj0 + i) * tk - qi * tq).astype(F32)
                    st = jnp.where(rel_ref[0:rows, :] + off <= 0.0, st, NEG_INF)
                sts.append(st)
            return sts

        all_sts = [scores(h) for h in range(n_hp)]
        if between is not None:
            between()
        for h in range(n_hp):
            hl = slice(h * LANES, (h + 1) * LANES)
            sts = all_sts[h]
            m_old = m_ref[h]
            m_new = m_old
            for st in sts:
                m_new = jnp.maximum(m_new, jnp.max(st, axis=0, keepdims=True))
            acc = acc_ref[h] * jnp.exp2(m_old - m_new)
            for i, st in enumerate(sts):
                pt = jnp.exp2(st - m_new).astype(BF16)
                vaug = jnp.concatenate([vt_ref[0, kj0 + i, hl, 0:rows], jnp.ones((EXTRA_ROWS, rows), BF16)], axis=0)
                acc = acc + jnp.dot(vaug, pt, preferred_element_type=F32)
            acc_ref[h] = acc
            m_ref[h] = m_new

    n_full = (qi * tq) // tk

    def chunk_with_sample_sub(k, kj, masked):
        sample_begin(k)
        chunks(kj, 1, masked, between=lambda: sample_sub(k))
        sample_end(k)

    def body(i, carry):
        chunk_with_sample_sub(sub_base + i, i, False)
        return carry

    lax.fori_loop(0, n_full, body, 0)

    assert tk == 2 * tq
    tile_at_chunk_start = (qi * tq) % tk == 0

    @pl.when(tile_at_chunk_start)
    def _():
        chunks(n_full, 1, True, rows=tq)

    @pl.when(jnp.logical_not(tile_at_chunk_start))
    def _():
        chunk_with_sample_sub(sub_base + n_full, n_full, True)

    lam = _diff_lambda(lq1_ref[...], lk1_ref[...], lq2_ref[...], lk2_ref[...], lam_init)
    for h in range(n_hp):
        ot = acc_ref[h, 0:LANES, :] / acc_ref[h, LANES:LANES + 1, :]
        at = jnp.where(sub < D_V, ot[:, 0:tq] - lam * ot[:, tq:2 * tq], ot[:, 2 * tq:3 * tq] - lam * ot[:, 3 * tq:])
        a2 = at * at
        ss0 = jnp.sum(a2[0:D_V], axis=0, keepdims=True)
        ss1 = jnp.sum(a2[D_V:], axis=0, keepdims=True)
        ms = jnp.where(sub < D_V, ss0, ss1) * (1.0 / D_V)
        an = at * lax.rsqrt(ms + EPS) * g_ref[...] * (1.0 - lam_init)
        o_ref[0, :, h * LANES:(h + 1) * LANES] = an.T.astype(BF16)


def _attn(qtb, kb, vtb, page_table, qsb, ks, vs, cache_k, cache_v, lam_params, g_subln, lam_init,
          *, tq=256, pps=16):
    b, w, s = qtb.shape
    n_chunks, tk = vtb.shape[1], vtb.shape[3]
    n_hp = w // LANES
    hw = n_hp * LANES
    cols = 4 * tq
    bd, t_new, _ = qsb.shape
    n_pages = page_table.shape[1]
    n_pool = cache_k.shape[0]
    past = n_pages * PAGE_SIZE
    n_subs = bd * (n_pages // pps)
    nq = s // tq
    subs_per_batch = (nq // 2) ** 2
    assert tk == 2 * tq and nq % 2 == 0 and n_pages % pps == 0 and b * subs_per_batch == n_subs
    ck = jnp.transpose(cache_k, (0, 2, 3, 1)).reshape(n_pool, w, PAGE_SIZE)
    cv = jnp.transpose(cache_v, (0, 2, 3, 1)).reshape(n_pool, w, PAGE_SIZE)
    g8 = jnp.tile(g_subln, N_ATT_HEADS).reshape(1, w)
    srows = 2 * N_ATT_HEADS * t_new
    g2 = jnp.concatenate([g_subln, g_subln]).reshape(LANES, 1)
    pos = lax.broadcasted_iota(jnp.int32, (s, LANES), 0)
    lane = lax.broadcasted_iota(jnp.int32, (s, LANES), 1)
    n_parts = len(L2E_PARTS)
    pf = jnp.where(lane < n_parts, pos // POS_SPLIT,
                   jnp.where(lane < N_POS_FEATS, pos % POS_SPLIT, 0)).astype(BF16)
    rel = (jnp.arange(tk, dtype=jnp.int32)[:, None] - (jnp.arange(cols, dtype=jnp.int32) % tq)[None, :]).astype(F32)
    const2 = lambda bi, hp, qi, pt: (0, 0)
    const3 = lambda bi, hp, qi, pt: (0, 0, 0)
    lam_specs = [pl.BlockSpec((1, D_QK), const2)] * 4
    grid_spec = pltpu.PrefetchScalarGridSpec(
        num_scalar_prefetch=1,
        grid=(b, w // hw, s // tq),
        in_specs=[pl.BlockSpec((1, hw, tq), lambda bi, hp, qi, pt: (bi, hp, qi)),
                  pl.BlockSpec((1, s, hw), lambda bi, hp, qi, pt: (bi, 0, hp), pipeline_mode=pl.Buffered(1)),
                  pl.BlockSpec((1, n_chunks, hw, tk), lambda bi, hp, qi, pt: (bi, 0, hp, 0),
                               pipeline_mode=pl.Buffered(1)),
                  pl.BlockSpec((s, LANES), const2),
                  pl.BlockSpec((tk, cols), const2),
                  *lam_specs,
                  pl.BlockSpec((LANES, 1), const2),
                  pl.BlockSpec((bd, t_new, w), const3),
                  pl.BlockSpec((bd, t_new, w), const3),
                  pl.BlockSpec((bd, t_new, w), const3),
                  pl.BlockSpec((1, w), const2),
                  pl.BlockSpec(memory_space=pl.ANY),
                  pl.BlockSpec(memory_space=pl.ANY)],
        out_specs=[pl.BlockSpec((1, tq, hw), lambda bi, hp, qi, pt: (bi, qi, hp)),
                   pl.BlockSpec((bd, t_new, w), const3)],
        scratch_shapes=[pltpu.VMEM((n_hp, 2 * LANES, cols), BF16),
                        pltpu.VMEM((n_hp, 1, cols), F32),
                        pltpu.VMEM((n_hp, LANES + EXTRA_ROWS, cols), F32),
                        pltpu.VMEM((srows, w), BF16),
                        pltpu.VMEM((srows, 1), F32),
                        pltpu.VMEM((srows, 1), F32),
                        pltpu.VMEM((srows, w), F32),
                        pltpu.VMEM((PAGE_SLOTS, pps, w, PAGE_SIZE), F32),
                        pltpu.VMEM((PAGE_SLOTS, pps, w, PAGE_SIZE), F32),
                        pltpu.SemaphoreType.DMA((2, PAGE_SLOTS))],
    )
    return pl.pallas_call(
        functools.partial(_attn_kernel, tq=tq, tk=tk, n_hp=n_hp, lam_init=lam_init,
                          pps=pps, subs_per_batch=subs_per_batch, n_subs=n_subs, past=past),
        grid_spec=grid_spec,
        out_shape=[jax.ShapeDtypeStruct((b, s, w), BF16), jax.ShapeDtypeStruct((bd, t_new, w), BF16)],
        compiler_params=_cparams("arbitrary", "arbitrary", "arbitrary"),
        name="attn",
    )(page_table, qtb, kb, vtb, pf, rel, *[p.reshape(1, D_QK) for p in lam_params], g2,
      qsb, ks, vs, g8, ck, cv)


def _outproj_kernel(a_ref, u_ref, vs_ref, x_ref, ga1_ref, sc2_ref, sh2_ref, wcat_ref, sbias_ref, wout_ref,
                    gpost_ref, gpre_ref, wr_cat_ref, br_ref, tril_ref, *rest, group_dispatch):
    if group_dispatch:
        x1_ref, pay_ref, info_ref, counts_ref, sg_ref, run_ref = rest
    else:
        x1_ref, h2_ref, gates_ref, sg_ref = rest
    tm = x_ref.shape[0]
    sgu_w = u_ref.shape[1]
    lane = lax.broadcasted_iota(jnp.int32, (CHUNK, LANES), 1)
    for ci in range(tm // CHUNK):
        rs = slice(ci * CHUNK, (ci + 1) * CHUNK)
        for gp in range(sgu_w // LANES):
            cs = slice(gp * LANES, (gp + 1) * LANES)
            vp = vs_ref[rs, cs]
            zero = jnp.zeros_like(vp)
            rhs = jnp.concatenate([jnp.where(lane < SGU_GC, vp, zero), jnp.where(lane >= SGU_GC, vp, zero)], axis=0)
            mixed = jnp.dot(wcat_ref[gp], rhs, preferred_element_type=F32) + sbias_ref[:, cs]
            sg_ref[rs, cs] = (u_ref[rs, cs].astype(F32) * mixed).astype(BF16)
    att_w = a_ref.shape[1]
    mix = (jnp.dot(a_ref[...], wout_ref[0:att_w, :], preferred_element_type=F32)
           + jnp.dot(sg_ref[...], wout_ref[att_w:, :], preferred_element_type=F32))
    x1 = x_ref[...] + ga1_ref[0] * _rms(mix, gpost_ref[...])
    x1_ref[...] = x1
    h2 = _rms(x1, gpre_ref[...]) * (1.0 + sc2_ref[0]) + sh2_ref[0]
    h2_hi = h2.astype(BF16)
    h2_lo = (h2 - h2_hi.astype(F32)).astype(BF16)
    hh = jnp.dot(h2_hi, wr_cat_ref[...], preferred_element_type=F32)
    logits = (hh[:, :LANES] + hh[:, LANES:]
              + jnp.dot(h2_lo, wr_cat_ref[:, :LANES], preferred_element_type=F32)) + br_ref[...]
    ln = lax.broadcasted_iota(jnp.int32, (tm, LANES), 1)
    big = jnp.int32(LANES)
    is_g = (ln >= N_EXPERTS) & (ln < N_EXPERTS + N_EXPERT_GROUPS)
    gl = jnp.where(is_g, logits, NEG_INF)
    gmax = jnp.max(gl, axis=-1, keepdims=True)
    g_idx = jnp.min(jnp.where(gl == gmax, ln, big), axis=-1, keepdims=True) - N_EXPERTS
    g_w = 1.0 / jnp.sum(jnp.exp(gl - gmax), axis=-1, keepdims=True)
    in_grp = (ln >= g_idx * EXPERTS_PER_GROUP) & (ln < (g_idx + 1) * EXPERTS_PER_GROUP)
    el = jnp.where(in_grp, logits, NEG_INF)
    emax = jnp.max(el, axis=-1, keepdims=True)
    ez = jnp.exp(el - emax)
    prob = ez / jnp.sum(ez, axis=-1, keepdims=True)
    p1 = jnp.max(prob, axis=-1, keepdims=True)
    i1 = jnp.min(jnp.where((prob == p1) & in_grp, ln, big), axis=-1, keepdims=True)
    rest = jnp.where(in_grp & (ln != i1), prob, -1.0)
    p2 = jnp.max(rest, axis=-1, keepdims=True)
    i2 = jnp.min(jnp.where(rest == p2, ln, big), axis=-1, keepdims=True)
    den = p1 + p2
    gates = jnp.where(ln == i1, p1 / den * g_w, 0.0) + jnp.where(ln == i2, p2 / den * g_w, 0.0)
    if not group_dispatch:
        h2_ref[...] = h2_hi
        gates_ref[...] = gates[:, :N_EXPERTS]
        return

    bits = pltpu.bitcast(h2_hi.astype(F32), jnp.uint32)
    half = PAYLOAD_H2_SEGS * LANES
    for s in range(PAYLOAD_H2_SEGS):
        lo = bits[:, s * LANES:(s + 1) * LANES]
        hi = bits[:, half + s * LANES:half + (s + 1) * LANES]
        pay_ref[s] = hi | lax.shift_right_logical(lo, jnp.uint32(16))
    pay_ref[PAYLOAD_H2_SEGS] = pltpu.bitcast(gates, jnp.uint32)

    @pl.when(pl.program_id(0) == 0)
    def _():
        run_ref[...] = jnp.zeros_like(run_ref)

    onehot = jnp.where(ln == g_idx, 1.0, 0.0).astype(BF16)
    csum = jnp.dot(tril_ref[...], onehot, preferred_element_type=F32) + run_ref[...]
    rank = jnp.sum(jnp.where(ln == g_idx, csum, 0.0), axis=-1, keepdims=True) - 1.0
    run_ref[...] = csum[tm - 1:tm, :]
    counts_ref[...] = csum[tm - 1:tm, :]
    info = jnp.where(ln == 0, g_idx, jnp.where(ln == 1, rank.astype(jnp.int32), 0))
    info_ref[...] = info[:, :INFO_W]


def _outproj(a, u, vsb, x, ga1, sc2, sh2, wcat, sbias, w_out_bf, g_post, g_pre, wr_cat, br,
             *, tm, tiles_per_mod, group_dispatch):
    n, d = x.shape
    att_w = a.shape[1]
    sgu_w = u.shape[1]
    r = ga1.shape[1]
    row = lambda i: (i, 0)
    modmap = lambda i: (i // tiles_per_mod, 0, 0)
    const = lambda i: (0, 0)
    tril = jnp.tril(jnp.ones((tm, tm), BF16))
    if group_dispatch:
        out_specs = [pl.BlockSpec((tm, d), row),
                     pl.BlockSpec((PAYLOAD_SEGS, tm, LANES), lambda i: (0, i, 0)),
                     pl.BlockSpec((tm, INFO_W), row),
                     pl.BlockSpec((1, LANES), const)]
        out_shape = [jax.ShapeDtypeStruct((n, d), F32),
                     jax.ShapeDtypeStruct((PAYLOAD_SEGS, n, LANES), jnp.uint32),
                     jax.ShapeDtypeStruct((n, INFO_W), jnp.int32),
                     jax.ShapeDtypeStruct((1, LANES), F32)]
        scratch = [pltpu.VMEM((tm, sgu_w), BF16), pltpu.VMEM((1, LANES), F32)]
    else:
        out_specs = [pl.BlockSpec((tm, d), row),
                     pl.BlockSpec((tm, d), row),
                     pl.BlockSpec((tm, N_EXPERTS), row)]
        out_shape = [jax.ShapeDtypeStruct((n, d), F32),
                     jax.ShapeDtypeStruct((n, d), BF16),
                     jax.ShapeDtypeStruct((n, N_EXPERTS), F32)]
        scratch = [pltpu.VMEM((tm, sgu_w), BF16)]
    return pl.pallas_call(
        functools.partial(_outproj_kernel, group_dispatch=group_dispatch),
        grid=(n // tm,),
        in_specs=[pl.BlockSpec((tm, att_w), row),
                  pl.BlockSpec((tm, sgu_w), row),
                  pl.BlockSpec((tm, sgu_w), row),
                  pl.BlockSpec((tm, d), row),
                  pl.BlockSpec((1, r, d), modmap),
                  pl.BlockSpec((1, r, d), modmap),
                  pl.BlockSpec((1, r, d), modmap),
                  pl.BlockSpec(wcat.shape, lambda i: (0, 0, 0)),
                  pl.BlockSpec(sbias.shape, const),
                  pl.BlockSpec(w_out_bf.shape, const),
                  pl.BlockSpec((1, d), const),
                  pl.BlockSpec((1, d), const),
                  pl.BlockSpec(wr_cat.shape, const),
                  pl.BlockSpec((1, LANES), const),
                  pl.BlockSpec((tm, tm), const)],
        out_specs=out_specs,
        out_shape=out_shape,
        scratch_shapes=scratch,
        compiler_params=_cparams("arbitrary"),
        name="outproj",
    )(a, u, vsb, x, ga1, sc2, sh2, wcat, sbias, w_out_bf, g_post.reshape(1, d), g_pre.reshape(1, d),
      wr_cat, br, tril)


def _moe_kernel(h_ref, gates_ref, x1_ref, ga2_ref, gpost_ref, wg_ref, wu_ref, wd_ref, o_ref, acc_ref):
    e = pl.program_id(1)

    @pl.when(e == 0)
    def _():
        acc_ref[...] = jnp.zeros_like(acc_ref)

    h = h_ref[...]
    ln = lax.broadcasted_iota(jnp.int32, gates_ref.shape, 1)
    per_step = wg_ref.shape[0]
    y = jnp.zeros(acc_ref.shape, F32)
    for el in range(per_step):
        gate = jnp.dot(h, wg_ref[el], preferred_element_type=F32)
        up = jnp.dot(h, wu_ref[el], preferred_element_type=F32)
        hdn = (gate * jax.nn.sigmoid(gate) * up).astype(BF16)
        ge = jnp.sum(jnp.where(ln == e * per_step + el, gates_ref[...], 0.0), axis=-1, keepdims=True)
        y = y + ge * jnp.dot(hdn, wd_ref[el], preferred_element_type=F32)
    acc_ref[...] += y

    @pl.when(e == pl.num_programs(1) - 1)
    def _():
        o_ref[...] = x1_ref[...] + ga2_ref[0] * _rms(acc_ref[...], gpost_ref[...])


def _moe(h2, gates, x1, ga2, g_post, wg_bf, wu_bf, wd_bf, *, tm, tiles_per_mod):
    n, d = x1.shape
    n_e, _, de = wg_bf.shape
    r = ga2.shape[1]
    row = lambda i, e: (i, 0)
    return pl.pallas_call(
        _moe_kernel,
        grid=(n // tm, n_e // EXPERTS_PER_GROUP),
        in_specs=[pl.BlockSpec((tm, d), row),
                  pl.BlockSpec((tm, n_e), row),
                  pl.BlockSpec((tm, d), row),
                  pl.BlockSpec((1, r, d), lambda i, e: (i // tiles_per_mod, 0, 0)),
                  pl.BlockSpec((1, d), lambda i, e: (0, 0)),
                  pl.BlockSpec((EXPERTS_PER_GROUP, d, de), lambda i, e: (e, 0, 0)),
                  pl.BlockSpec((EXPERTS_PER_GROUP, d, de), lambda i, e: (e, 0, 0)),
                  pl.BlockSpec((EXPERTS_PER_GROUP, de, d), lambda i, e: (e, 0, 0))],
        out_specs=pl.BlockSpec((tm, d), row),
        out_shape=jax.ShapeDtypeStruct((n, d), F32),
        scratch_shapes=[pltpu.VMEM((tm, d), F32)],
        compiler_params=_cparams("arbitrary", "arbitrary"),
        name="moe",
    )(h2, gates, x1, ga2, g_post.reshape(1, d), wg_bf, wu_bf, wd_bf)


def _sc_mesh():
    return plsc.VectorSubcoreMesh(core_axis_name="core", subcore_axis_name="subcore")


def _sc_scatter_rows(x, idx, n_out_rows):
    n_rows, width = x.shape

    @pl.kernel(out_type=jax.ShapeDtypeStruct((n_out_rows, width), x.dtype), mesh=_sc_mesh(), scratch_types=[])
    def scatter(x_hbm, i_hbm, o_hbm):
        def body(x_vmem, i_vmem):
            pltpu.sync_copy(x_vmem, o_hbm.at[i_vmem.at[0]])

        pltpu.emit_pipeline(
            body,
            grid=(n_rows // SC_WINDOW,),
            in_specs=[pl.BlockSpec((SC_WINDOW, width), lambda i: (i, 0)),
                      pl.BlockSpec((1, SC_WINDOW), lambda i: (0, i))],
            out_specs=[],
            core_axis_name=("core", "subcore"),
            dimension_semantics=(pltpu.PARALLEL,),
        )(x_hbm, i_hbm)

    return scatter(x, idx.reshape(1, n_rows))


def _sc_gather_rows(x, idx):
    n_rows = idx.shape[0]
    width = x.shape[1]

    @pl.kernel(out_type=jax.ShapeDtypeStruct((n_rows, width), x.dtype), mesh=_sc_mesh(), scratch_types=[])
    def gather(x_hbm, i_hbm, o_hbm):
        def body(i_vmem, o_vmem):
            pltpu.sync_copy(x_hbm.at[i_vmem.at[0]], o_vmem)

        pltpu.emit_pipeline(
            body,
            grid=(n_rows // SC_WINDOW,),
            in_specs=[pl.BlockSpec((1, SC_WINDOW), lambda i: (0, i))],
            out_specs=[pl.BlockSpec((SC_WINDOW, width), lambda i: (i, 0))],
            core_axis_name=("core", "subcore"),
            dimension_semantics=(pltpu.PARALLEL,),
        )(i_hbm, o_hbm)

    return gather(x, idx.reshape(1, n_rows))


def _group_moe_kernel(tg_ref, nused_ref, xs_ref, gpost_ref, wg_ref, wu_ref, wd_ref, rs_ref):
    t = pl.program_id(0)
    tm = xs_ref.shape[1]

    @pl.when(t >= nused_ref[0])
    def _():
        rs_ref[...] = jnp.zeros_like(rs_ref)

    @pl.when(t < nused_ref[0])
    def _():
        words = [xs_ref[s] for s in range(PAYLOAD_H2_SEGS)]
        lo = [pltpu.bitcast(lax.shift_left(w, jnp.uint32(16)), F32) for w in words]
        hi = [pltpu.bitcast(w & jnp.uint32(0xFFFF0000), F32) for w in words]
        x = jnp.concatenate(lo + hi, axis=1).astype(BF16)
        gates = pltpu.bitcast(xs_ref[PAYLOAD_H2_SEGS], F32)
        ln = lax.broadcasted_iota(jnp.int32, (tm, LANES), 1)
        first = tg_ref[t] * EXPERTS_PER_GROUP
        y = jnp.zeros((tm, x.shape[1]), F32)
        for el in range(EXPERTS_PER_GROUP):
            ge = jnp.sum(jnp.where(ln == first + el, gates, 0.0), axis=-1, keepdims=True)
            gate = jnp.dot(x, wg_ref[el], preferred_element_type=F32)
            up = jnp.dot(x, wu_ref[el], preferred_element_type=F32)
            hdn = (gate * jax.nn.sigmoid(gate) * up).astype(BF16)
            y = y + ge * jnp.dot(hdn, wd_ref[el], preferred_element_type=F32)
        r = _rms(y, gpost_ref[...])
        for s in range(rs_ref.shape[0]):
            rs_ref[s] = r[:, s * LANES:(s + 1) * LANES]


def _group_moe(xs, tile_group, n_used, g_post, wg_bf, wu_bf, wd_bf, *, tm):
    _, p_rows, _ = xs.shape
    n_e, d, de = wg_bf.shape
    g = EXPERTS_PER_GROUP
    grid_spec = pltpu.PrefetchScalarGridSpec(
        num_scalar_prefetch=2,
        grid=(p_rows // tm,),
        in_specs=[pl.BlockSpec((PAYLOAD_SEGS, tm, LANES), lambda t, tg, nu: (0, t, 0)),
                  pl.BlockSpec((1, d), lambda t, tg, nu: (0, 0)),
                  pl.BlockSpec((g, d, de), lambda t, tg, nu: (tg[t], 0, 0)),
                  pl.BlockSpec((g, d, de), lambda t, tg, nu: (tg[t], 0, 0)),
                  pl.BlockSpec((g, de, d), lambda t, tg, nu: (tg[t], 0, 0))],
        out_specs=pl.BlockSpec((d // LANES, tm, LANES), lambda t, tg, nu: (0, t, 0)),
    )
    return pl.pallas_call(
        _group_moe_kernel,
        grid_spec=grid_spec,
        out_shape=jax.ShapeDtypeStruct((d // LANES, p_rows, LANES), F32),
        compiler_params=_cparams("arbitrary"),
        name="group_moe",
    )(tile_group, n_used, xs, g_post.reshape(1, d), wg_bf, wu_bf, wd_bf)


def _residual_kernel(x1_ref, ga2_ref, r_ref, o_ref):
    r = jnp.concatenate([r_ref[s] for s in range(r_ref.shape[0])], axis=1)
    o_ref[...] = x1_ref[...] + ga2_ref[0] * r


def _residual(x1, ga2, r_segs, *, tm, tiles_per_mod):
    n, d = x1.shape
    return pl.pallas_call(
        _residual_kernel,
        grid=(n // tm,),
        in_specs=[pl.BlockSpec((tm, d), lambda i: (i, 0)),
                  pl.BlockSpec((1, ga2.shape[1], d), lambda i: (i // tiles_per_mod, 0, 0)),
                  pl.BlockSpec((d // LANES, tm, LANES), lambda i: (0, i, 0))],
        out_specs=pl.BlockSpec((tm, d), lambda i: (i, 0)),
        out_shape=jax.ShapeDtypeStruct((n, d), F32),
        compiler_params=_cparams("arbitrary"),
        name="residual",
    )(x1, ga2, r_segs)


def _group_plan(info, counts, n_tokens, tm):
    n_tiles = n_tokens // tm + N_EXPERT_GROUPS
    cnt = counts[0, :N_EXPERT_GROUPS].astype(jnp.int32)
    padded = (cnt + tm - 1) // tm * tm
    ends = jnp.cumsum(padded)
    starts = ends - padded
    gid, rank = info[:, 0], info[:, 1]
    onehot = (gid[:, None] == jnp.arange(N_EXPERT_GROUPS, dtype=jnp.int32)[None, :]).astype(jnp.int32)
    pos = jnp.sum(onehot * starts[None, :], axis=1) + rank
    tile_ends = ends // tm
    t = jnp.arange(n_tiles, dtype=jnp.int32)
    tile_group = jnp.minimum(jnp.sum((t[:, None] >= tile_ends[None, :]).astype(jnp.int32), axis=1),
                             N_EXPERT_GROUPS - 1)
    return pos, tile_group, tile_ends[-1:], n_tiles * tm


def _sgu_weights(ws, bs, chunk_len):
    causal = jnp.tril(jnp.ones((chunk_len, chunk_len), ws.dtype))
    wm = ws[:, :chunk_len, :chunk_len] * causal
    reps = CHUNK // chunk_len
    if reps > 1:
        eye = jnp.eye(reps, dtype=ws.dtype)
        wm = jnp.einsum("ab,gts->gatbs", eye, wm).reshape(N_SGU_GROUPS, CHUNK, CHUNK)
    wcat = jnp.concatenate([wm[0::2], wm[1::2]], axis=2).astype(BF16)
    bt = jnp.tile(bs[:, :chunk_len].T, (reps, 1))
    sbias = jnp.repeat(bt, SGU_GC, axis=1)
    return wcat, sbias


def _split_mod(mod):
    return [m[:, None, :] for m in jnp.split(mod, 6, axis=-1)]


def kernel(x_prompt, x_sample, cache_k, cache_v, page_table, c_prompt, c_sample, w_ada, b_ada, g_pre_mix, g_post_mix, g_pre_ffn, g_post_ffn, w_in, lam_q1, lam_k1, lam_q2, lam_k2, g_subln, sgu_ln_g, sgu_ln_b, sgu_ws, sgu_bs, w_out, w_rg, b_rg, w_re, b_re, w_gate, w_up, w_down):
    depth = w_in.shape[0]
    assert depth == 1
    l = 0
    lam_init = 0.8 - 0.6 * math.exp(-0.3 * l)
    bp, sp, d = x_prompt.shape
    bs_, ts, _ = x_sample.shape
    n_s = bs_ * ts

    mod = _ada(jnp.concatenate([c_prompt, c_sample], axis=0), w_ada[l], b_ada[l])
    sh1p, sc1p, ga1p, sh2p, sc2p, ga2p = _split_mod(mod[:bp])
    rep = lambda m: jnp.repeat(m, ts, axis=0).reshape(1, n_s, d)
    sh1s, sc1s, ga1s, sh2s, sc2s, ga2s = [rep(m) for m in jnp.split(mod[bp:], 6, axis=-1)]

    w_in_bf = w_in[l].astype(BF16)
    wqv_t_bf = jnp.concatenate([w_in_bf[:, 0:ATT_W], w_in_bf[:, 2 * ATT_W:3 * ATT_W]], axis=1).T
    w_out_bf = w_out[l].astype(BF16)
    wg_bf = w_gate[l].astype(BF16)
    wu_bf = w_up[l].astype(BF16)
    wd_bf = w_down[l].astype(BF16)
    wr = jnp.concatenate([w_re[l], w_rg[l]], axis=1)
    wr = jnp.pad(wr, ((0, 0), (0, LANES - wr.shape[1])))
    wr_hi = wr.astype(BF16)
    wr_cat = jnp.concatenate([wr_hi, (wr - wr_hi.astype(F32)).astype(BF16)], axis=1)
    br = jnp.pad(jnp.concatenate([b_re[l], b_rg[l]]), (0, LANES - N_EXPERTS - N_EXPERT_GROUPS)).reshape(1, LANES)
    lam_params = (lam_q1[l], lam_k1[l], lam_q2[l], lam_k2[l])

    tm_p = 512
    xp = x_prompt.reshape(bp * sp, d)
    ktp, vtp, qtb, kb, vtb, u, vsb, vs_last = _inproj(
        xp, sc1p, sh1p, g_pre_mix[l], w_in_bf, wqv_t_bf, sgu_ln_g[l], sgu_ln_b[l],
        tm=tm_p, tiles_per_seq=sp // tm_p, last_rows=CHUNK, transposed_qv=True)
    w3 = ATT_W
    xs = x_sample.reshape(n_s, d)
    ks, vs_, qbs, us, vsbs, vs_last_s = _inproj(
        xs, sc1s, sh1s, g_pre_mix[l], w_in_bf, wqv_t_bf, sgu_ln_g[l], sgu_ln_b[l],
        tm=n_s, tiles_per_seq=1, last_rows=n_s, transposed_qv=False)
    a, a_s = _attn(qtb, kb.reshape(bp, sp, w3), vtb, page_table, qbs.reshape(bs_, ts, w3),
                   ks.reshape(bs_, ts, w3), vs_.reshape(bs_, ts, w3), cache_k[l], cache_v[l],
                   lam_params, g_subln[l], lam_init)
    kp = jnp.transpose(ktp.reshape(bp, N_ATT_HEADS, 2 * D_QK, sp), (0, 3, 1, 2))
    vp = jnp.transpose(vtp.reshape(bp, N_ATT_HEADS, D_V, sp), (0, 3, 1, 2))
    wcat_p, sbias_p = _sgu_weights(sgu_ws[l], sgu_bs[l], CHUNK)
    n_p = bp * sp
    x1, payload, info, counts = _outproj(
        a.reshape(n_p, w3), u, vsb, xp, ga1p, sc2p, sh2p, wcat_p, sbias_p, w_out_bf,
        g_post_mix[l], g_pre_ffn[l], wr_cat, br, tm=tm_p, tiles_per_mod=sp // tm_p, group_dispatch=True)
    tm_moe = 512
    pos, tile_group, n_used, p_rows = _group_plan(info, counts, n_p, tm_moe)
    seg_p = lambda k: (jnp.arange(k, dtype=jnp.int32)[:, None] * p_rows + pos[None, :]).reshape(-1)
    sorted_rows = _sc_scatter_rows(payload.reshape(PAYLOAD_SEGS * n_p, LANES), seg_p(PAYLOAD_SEGS),
                                   PAYLOAD_SEGS * p_rows)
    rs = _group_moe(sorted_rows.reshape(PAYLOAD_SEGS, p_rows, LANES), tile_group, n_used, g_post_ffn[l],
                    wg_bf, wu_bf, wd_bf, tm=tm_moe)
    n_seg = d // LANES
    r_tok = _sc_gather_rows(rs.reshape(n_seg * p_rows, LANES), seg_p(n_seg))
    yp = _residual(x1, ga2p, r_tok.reshape(n_seg, n_p, LANES), tm=tm_p, tiles_per_mod=sp // tm_p)

    wcat_s, sbias_s = _sgu_weights(sgu_ws[l], sgu_bs[l], ts)
    x1s, h2s, gates_s = _outproj(a_s.reshape(n_s, w3), us, vsbs, xs, ga1s, sc2s, sh2s, wcat_s, sbias_s, w_out_bf,
                                 g_post_mix[l], g_pre_ffn[l], wr_cat, br, tm=n_s, tiles_per_mod=1,
                                 group_dispatch=False)
    ys = _moe(h2s, gates_s, x1s, ga2s, g_post_ffn[l], wg_bf, wu_bf, wd_bf, tm=n_s, tiles_per_mod=1)

    return (yp.reshape(bp, sp, d), ys.reshape(bs_, ts, d),
            kp[None], vp[None],
            vs_last.reshape(1, bp, CHUNK, -1),
            ks.reshape(1, bs_, ts, N_ATT_HEADS, 2 * D_QK), vs_.reshape(1, bs_, ts, N_ATT_HEADS, D_V),
            vs_last_s.reshape(1, bs_, ts, -1))
```

```python
import functools
import math
import struct

import jax
import jax.numpy as jnp
from jax import lax
from jax.experimental import pallas as pl
from jax.experimental.pallas import tpu as pltpu
from jax.experimental.pallas import tpu_sc as plsc

F32 = jnp.float32
BF16 = jnp.bfloat16

EPS = 1e-6
N_ATT_HEADS = 8
D_QK = 32
D_V = 64
ATT_W = N_ATT_HEADS * D_V
N_SGU_GROUPS = 8
SGU_GC = 64
SGU_W = N_SGU_GROUPS * SGU_GC
CHUNK = 128
N_EXPERT_GROUPS = 4
EXPERTS_PER_GROUP = 4
N_EXPERTS = 16
PAGE_SIZE = 128
LANES = 128
VMEM_LIMIT = 56 * 1024 * 1024
NEG_INF = float("-inf")
LOG2E = math.log2(math.e)
PAYLOAD_H2_SEGS = 4
PAYLOAD_SEGS = PAYLOAD_H2_SEGS + 1
INFO_W = 8
SC_WINDOW = 128
PAGE_SLOTS = 3


def _cparams(*sem):
    return pltpu.CompilerParams(dimension_semantics=sem, vmem_limit_bytes=VMEM_LIMIT)


def _rms(x, g):
    return x * lax.rsqrt(jnp.mean(x * x, axis=-1, keepdims=True) + EPS) * g


def _ada_kernel(c_ref, w_ref, b_ref, o_ref):
    c = c_ref[...]
    s = c * jax.nn.sigmoid(c)
    o_ref[...] = jnp.dot(s.astype(BF16), w_ref[...].astype(BF16), preferred_element_type=F32) + b_ref[...]


def _ada(c, w_ada, b_ada):
    n, d = c.shape
    nout = w_ada.shape[1]
    tn = d
    return pl.pallas_call(
        _ada_kernel,
        grid=(nout // tn,),
        in_specs=[pl.BlockSpec((n, d), lambda j: (0, 0)),
                  pl.BlockSpec((d, tn), lambda j: (0, j)),
                  pl.BlockSpec((1, tn), lambda j: (0, j))],
        out_specs=pl.BlockSpec((n, tn), lambda j: (0, j)),
        out_shape=jax.ShapeDtypeStruct((n, nout), F32),
        compiler_params=_cparams("arbitrary"),
        name="ada",
    )(c, w_ada, b_ada.reshape(1, nout))


Q_SCALE = D_QK ** -0.5 * LOG2E


def _inproj_kernel(x_ref, sc_ref, sh_ref, g_ref, w_ref, wqv_t_ref, lng_ref, lnb_ref, *out_refs,
                   last_rows, transposed_qv):
    x = x_ref[...]
    h = (_rms(x, g_ref[...]) * (1.0 + sc_ref[0]) + sh_ref[0]).astype(BF16)

    def proj(lo):
        return jnp.dot(h, w_ref[:, lo:lo + ATT_W], preferred_element_type=F32)

    def proj_t(lo):
        return lax.dot_general(wqv_t_ref[lo:lo + ATT_W, :], h, (((1,), (1,)), ((), ())),
                               preferred_element_type=F32)

    zk = proj(ATT_W)
    if transposed_qv:
        kt_ref, vt_ref, qtb_ref, kb_ref, vtb_ref, u_ref, vsb_ref, vsl_ref = out_refs
        kt_ref[0] = zk.T
        kb_ref[...] = zk.astype(BF16)
        qtb_ref[0] = (proj_t(0) * Q_SCALE).astype(BF16)
        zvt = proj_t(ATT_W)
        vt_ref[0] = zvt
        vtb_ref[0, 0] = zvt.astype(BF16)
    else:
        k_ref, v_ref, qb_ref, u_ref, vsb_ref, vsl_ref = out_refs
        k_ref[...] = zk
        qb_ref[...] = (proj(0) * Q_SCALE).astype(BF16)
        v_ref[...] = proj(2 * ATT_W)
    u_ref[...] = jax.nn.gelu(proj(3 * ATT_W)).astype(BF16)
    gs = jax.nn.gelu(proj(3 * ATT_W + SGU_W))
    mu = jnp.mean(gs, axis=-1, keepdims=True)
    xc = gs - mu
    vs = xc * lax.rsqrt(jnp.mean(xc * xc, axis=-1, keepdims=True) + EPS) * lng_ref[...] + lnb_ref[...]
    vsb_ref[...] = vs.astype(BF16)
    tm = vs.shape[0]
    vsl_ref[0] = vs[tm - last_rows:, :]


def _inproj(x, sc, sh, g, w_in_bf, wqv_t_bf, ln_g, ln_b, *, tm, tiles_per_seq, last_rows, transposed_qv):
    n, d = x.shape
    in_w = w_in_bf.shape[1]
    r = sc.shape[1]
    n_tiles = n // tm
    n_seq = n_tiles // tiles_per_seq
    s_len = tiles_per_seq * tm
    row = lambda i: (i, 0)
    modmap = lambda i: (i // tiles_per_seq, 0, 0)
    const = lambda i: (0, 0)
    col_t = lambda i: (i // tiles_per_seq, 0, i % tiles_per_seq)
    row_spec = pl.BlockSpec((tm, ATT_W), row)
    tail = [(pl.BlockSpec((tm, SGU_W), row), jax.ShapeDtypeStruct((n, SGU_W), BF16)),
            (pl.BlockSpec((tm, SGU_W), row), jax.ShapeDtypeStruct((n, SGU_W), BF16)),
            (pl.BlockSpec((1, last_rows, SGU_W), lambda i: (i // tiles_per_seq, 0, 0)),
             jax.ShapeDtypeStruct((n_seq, last_rows, SGU_W), F32))]
    if transposed_qv:
        head = [(pl.BlockSpec((1, ATT_W, tm), col_t), jax.ShapeDtypeStruct((n_seq, ATT_W, s_len), F32)),
                (pl.BlockSpec((1, ATT_W, tm), col_t), jax.ShapeDtypeStruct((n_seq, ATT_W, s_len), F32)),
                (pl.BlockSpec((1, ATT_W, tm), col_t), jax.ShapeDtypeStruct((n_seq, ATT_W, s_len), BF16)),
                (row_spec, jax.ShapeDtypeStruct((n, ATT_W), BF16)),
                (pl.BlockSpec((1, 1, ATT_W, tm), lambda i: (i // tiles_per_seq, i % tiles_per_seq, 0, 0)),
                 jax.ShapeDtypeStruct((n_seq, tiles_per_seq, ATT_W, tm), BF16))]
    else:
        head = [(row_spec, jax.ShapeDtypeStruct((n, ATT_W), F32)),
                (row_spec, jax.ShapeDtypeStruct((n, ATT_W), F32)),
                (row_spec, jax.ShapeDtypeStruct((n, ATT_W), BF16))]
    specs, shapes = zip(*(head + tail))
    return pl.pallas_call(
        functools.partial(_inproj_kernel, last_rows=last_rows, transposed_qv=transposed_qv),
        grid=(n_tiles,),
        in_specs=[pl.BlockSpec((tm, d), row),
                  pl.BlockSpec((1, r, d), modmap),
                  pl.BlockSpec((1, r, d), modmap),
                  pl.BlockSpec((1, d), const),
                  pl.BlockSpec((d, in_w), const),
                  pl.BlockSpec(wqv_t_bf.shape, const),
                  pl.BlockSpec((1, SGU_W), const),
                  pl.BlockSpec((1, SGU_W), const)],
        out_specs=list(specs),
        out_shape=list(shapes),
        compiler_params=_cparams("arbitrary"),
        name="inproj",
    )(x, sc, sh, g.reshape(1, d), w_in_bf, wqv_t_bf, ln_g.reshape(1, SGU_W), ln_b.reshape(1, SGU_W))


def _diff_lambda(lq1, lk1, lq2, lk2, lam_init):
    return (jnp.exp(jnp.sum(lq1 * lk1, axis=-1, keepdims=True))
            - jnp.exp(jnp.sum(lq2 * lk2, axis=-1, keepdims=True)) + lam_init)


def _pair_slopes(hp):
    s0 = jnp.where(hp == 0, 2.0 ** -1, jnp.where(hp == 1, 2.0 ** -3, jnp.where(hp == 2, 2.0 ** -5, 2.0 ** -7)))
    return s0.astype(F32), (s0 * 0.5).astype(F32)


def _bf16_round(x):
    bits = struct.unpack("<I", struct.pack("<f", x))[0]
    bits = (bits + 0x7FFF + ((bits >> 16) & 1)) & 0xFFFF0000
    return struct.unpack("<f", struct.pack("<I", bits))[0]


L2E_PARTS = []
_rest = LOG2E
for _ in range(3):
    L2E_PARTS.append(_bf16_round(_rest))
    _rest -= L2E_PARTS[-1]
POS_SPLIT = 64
N_POS_FEATS = 2 * len(L2E_PARTS)
EXTRA_ROWS = 16


def _attn_kernel(pt_ref, qt_ref, k_ref, vt_ref, pf_ref, rel_ref, lq1_ref, lk1_ref, lq2_ref, lk2_ref, g_ref,
                 qs_ref, kn_ref, vn_ref, g8_ref, ck_hbm, cv_hbm, o_ref, os_ref,
                 qst_ref, m_ref, acc_ref, qbd_ref, ms_ref, ls_ref, accs_ref, kbuf_ref, vbuf_ref, sem_ref,
                 *, tq, tk, n_hp, lam_init, pps, subs_per_batch, n_subs, past):
    qi = pl.program_id(2)
    cols = 4 * tq

    t_new, w = qs_ref.shape[1], qs_ref.shape[2]
    step = pl.program_id(0) * pl.num_programs(2) + qi
    subs_per_seq = past // (pps * PAGE_SIZE)
    half_q = qi // 2
    sub_base = pl.program_id(0) * subs_per_batch + half_q * half_q + (qi % 2) * half_q
    n_hj = 2 * N_ATT_HEADS
    srows = n_hj * t_new
    blk = pps * PAGE_SIZE
    srow_id = lax.broadcasted_iota(jnp.int32, (srows, 1), 0)
    shead = srow_id // (2 * t_new)
    sslope = jnp.zeros((srows, 1), F32)
    for h in range(N_ATT_HEADS):
        sslope = jnp.where(shead == h, 2.0 ** -(h + 1), sslope)
    sslope = sslope * LOG2E
    sqpos = (past + srow_id % t_new).astype(F32)

    def page_copies(k, slot):
        kc = jnp.minimum(k, n_subs - 1)
        seq_i, sub_i = kc // subs_per_seq, kc % subs_per_seq
        copies = []
        for i in range(pps):
            page = pt_ref[seq_i, sub_i * pps + i]
            copies.append(pltpu.make_async_copy(ck_hbm.at[page], kbuf_ref.at[slot, i], sem_ref.at[0, slot]))
            copies.append(pltpu.make_async_copy(cv_hbm.at[page], vbuf_ref.at[slot, i], sem_ref.at[1, slot]))
        return copies

    eye = (lax.broadcasted_iota(jnp.int32, (srows, srows), 0) == lax.broadcasted_iota(jnp.int32, (srows, srows), 1))

    def sample_update(s, vmat, from_pages):
        m_old = ms_ref[...]
        m_new = jnp.maximum(m_old, jnp.max(s, axis=-1, keepdims=True))
        p = jnp.exp2(s - m_new)
        corr = jnp.exp2(m_old - m_new)
        ls_ref[...] = corr * ls_ref[...] + jnp.sum(p, axis=-1, keepdims=True)
        corr_row = jnp.sum(jnp.where(eye, corr, 0.0), axis=0, keepdims=True)
        if from_pages:
            pv_t = jnp.dot(vmat, p.T.astype(BF16), preferred_element_type=F32)
        else:
            pv_t = jnp.dot(p, vmat, preferred_element_type=F32).T
        accs_ref[...] = accs_ref[...] * corr_row + pv_t
        ms_ref[...] = m_new

    def sample_begin(k):
        @pl.when(k % subs_per_seq == 0)
        def _():
            q = qs_ref[k // subs_per_seq]
            lane = lax.broadcasted_iota(jnp.int32, (t_new, w), 1)
            for hj in range(n_hj):
                qbd_ref[hj * t_new:(hj + 1) * t_new, :] = jnp.where(
                    (lane >= D_QK * hj) & (lane < D_QK * (hj + 1)), q, jnp.zeros_like(q))
            ms_ref[...] = jnp.full((srows, 1), NEG_INF, F32)
            ls_ref[...] = jnp.zeros((srows, 1), F32)
            accs_ref[...] = jnp.zeros((w, srows), F32)

    def sample_sub(k):
        slot = k % PAGE_SLOTS
        for c in page_copies(k, slot):
            c.wait()
        ahead = k + PAGE_SLOTS - 1
        for c in page_copies(ahead, ahead % PAGE_SLOTS):
            c.start()
        kt_all = jnp.concatenate([kbuf_ref[slot, i].astype(BF16) for i in range(pps)], axis=1)
        vt_all = jnp.concatenate([vbuf_ref[slot, i].astype(BF16) for i in range(pps)], axis=1)
        s = jnp.dot(qbd_ref[...], kt_all, preferred_element_type=F32)
        kpos = ((k % subs_per_seq) * blk + lax.broadcasted_iota(jnp.int32, (1, blk), 1)).astype(F32)
        sample_update(s - sslope * (sqpos - kpos), vt_all, True)

    @pl.when(step == 0)
    def _():
        for k0 in range(PAGE_SLOTS - 1):
            for c in page_copies(k0, k0):
                c.start()

    def sample_end(k):
        @pl.when(k % subs_per_seq == subs_per_seq - 1)
        def _():
            sample_finish(k // subs_per_seq)

        @pl.when(k == n_subs - 1)
        def _():
            for extra in range(n_subs, n_subs + PAGE_SLOTS - 1):
                for c in page_copies(extra, extra % PAGE_SLOTS):
                    c.wait()

    def sample_finish(seq):
        kn = kn_ref[seq]
        sn = lax.dot_general(qbd_ref[...].astype(F32), kn, (((1,), (1,)), ((), ())), preferred_element_type=F32)
        kposn = (past + lax.broadcasted_iota(jnp.int32, (1, t_new), 1)).astype(F32)
        dist = sqpos - kposn
        sample_update(jnp.where(dist >= 0, sn - sslope * dist, NEG_INF), vn_ref[seq], False)
        lam_s = _diff_lambda(lq1_ref[...], lk1_ref[...], lq2_ref[...], lk2_ref[...], lam_init)
        o = accs_ref[...].T / ls_ref[...]
        lane = lax.broadcasted_iota(jnp.int32, (t_new, w), 1)
        a = jnp.zeros((t_new, w), F32)
        for h in range(N_ATT_HEADS):
            o1 = o[(2 * h) * t_new:(2 * h + 1) * t_new]
            o2 = o[(2 * h + 1) * t_new:(2 * h + 2) * t_new]
            a = jnp.where((lane >= D_V * h) & (lane < D_V * (h + 1)), o1 - lam_s * o2, a)
        a2 = a * a
        msq = jnp.zeros((t_new, w), F32)
        for h in range(N_ATT_HEADS):
            in_h = (lane >= D_V * h) & (lane < D_V * (h + 1))
            ssh = jnp.sum(jnp.where(in_h, a2, 0.0), axis=-1, keepdims=True)
            msq = jnp.where(in_h, ssh, msq)
        os_ref[seq] = (a * lax.rsqrt(msq * (1.0 / D_V) + EPS) * g8_ref[...] * (1.0 - lam_init)).astype(BF16)

    sub = lax.broadcasted_iota(jnp.int32, (LANES, tq), 0)
    col = lax.broadcasted_iota(jnp.int32, (EXTRA_ROWS, cols), 1)
    row = lax.broadcasted_iota(jnp.int32, (EXTRA_ROWS, cols), 0)
    feat = jnp.zeros((EXTRA_ROWS, cols), F32)
    for i, part in enumerate(L2E_PARTS):
        feat = jnp.where(row == i, POS_SPLIT * part, feat)
        feat = jnp.where(row == len(L2E_PARTS) + i, part, feat)
    for h in range(n_hp):
        qt = qt_ref[0, h * LANES:(h + 1) * LANES, :]
        for c in range(4):
            qst_ref[h, 0:LANES, c * tq:(c + 1) * tq] = jnp.where((sub >= D_QK * c) & (sub < D_QK * (c + 1)), qt,
                                                                 jnp.zeros_like(qt))
        s0, s1 = _pair_slopes(pl.program_id(1) * n_hp + h)
        qst_ref[h, LANES:LANES + EXTRA_ROWS, :] = (jnp.where(col < 2 * tq, s0, s1) * feat).astype(BF16)
        qst_ref[h, LANES + EXTRA_ROWS:, :] = jnp.zeros((LANES - EXTRA_ROWS, cols), BF16)
    m_ref[...] = jnp.full(m_ref.shape, NEG_INF, F32)
    acc_ref[...] = jnp.zeros(acc_ref.shape, F32)

    def chunks(kj0, n_sub, masked, rows=tk, between=None):
        def scores(h):
            hl = slice(h * LANES, (h + 1) * LANES)
            sts = []
            for i in range(n_sub):
                start = pl.multiple_of((kj0 + i) * tk, tk)
                kaug = jnp.concatenate([k_ref[0, pl.ds(start, rows), hl], pf_ref[pl.ds(start, rows), :]], axis=1)
                st = jnp.dot(kaug, qst_ref[h], preferred_element_type=F32)
                if masked:
                    off = ((kj0 + i) * tk - qi * tq).astype(F32)
                    st = jnp.where(rel_ref[0:rows, :] + off <= 0.0, st, NEG_INF)
                sts.append(st)
            return sts

        all_sts = [scores(h) for h in range(n_hp)]
        if between is not None:
            between()
        for h in range(n_hp):
            hl = slice(h * LANES, (h + 1) * LANES)
            sts = all_sts[h]
            m_old = m_ref[h]
            m_new = m_old
            for st in sts:
                m_new = jnp.maximum(m_new, jnp.max(st, axis=0, keepdims=True))
            acc = acc_ref[h] * jnp.exp2(m_old - m_new)
            for i, st in enumerate(sts):
                pt = jnp.exp2(st - m_new).astype(BF16)
                vaug = jnp.concatenate([vt_ref[0, kj0 + i, hl, 0:rows], jnp.ones((EXTRA_ROWS, rows), BF16)], axis=0)
                acc = acc + jnp.dot(vaug, pt, preferred_element_type=F32)
            acc_ref[h] = acc
            m_ref[h] = m_new

    n_full = (qi * tq) // tk

    def chunk_with_sample_sub(k, kj, masked):
        sample_begin(k)
        chunks(kj, 1, masked, between=lambda: sample_sub(k))
        sample_end(k)

    def body(i, carry):
        chunk_with_sample_sub(sub_base + i, i, False)
        return carry

    lax.fori_loop(0, n_full, body, 0)

    assert tk == 2 * tq
    tile_at_chunk_start = (qi * tq) % tk == 0

    @pl.when(tile_at_chunk_start)
    def _():
        chunks(n_full, 1, True, rows=tq)

    @pl.when(jnp.logical_not(tile_at_chunk_start))
    def _():
        chunk_with_sample_sub(sub_base + n_full, n_full, True)

    lam = _diff_lambda(lq1_ref[...], lk1_ref[...], lq2_ref[...], lk2_ref[...], lam_init)
    for h in range(n_hp):
        ot = acc_ref[h, 0:LANES, :] / acc_ref[h, LANES:LANES + 1, :]
        at = jnp.where(sub < D_V, ot[:, 0:tq] - lam * ot[:, tq:2 * tq], ot[:, 2 * tq:3 * tq] - lam * ot[:, 3 * tq:])
        a2 = at * at
        ss0 = jnp.sum(a2[0:D_V], axis=0, keepdims=True)
        ss1 = jnp.sum(a2[D_V:], axis=0, keepdims=True)
        ms = jnp.where(sub < D_V, ss0, ss1) * (1.0 / D_V)
        an = at * lax.rsqrt(ms + EPS) * g_ref[...] * (1.0 - lam_init)
        o_ref[0, :, h * LANES:(h + 1) * LANES] = an.T.astype(BF16)


def _attn(qtb, kb, vtb, page_table, qsb, ks, vs, cache_k, cache_v, lam_params, g_subln, lam_init,
          *, tq=256, pps=16):
    b, w, s = qtb.shape
    n_chunks, tk = vtb.shape[1], vtb.shape[3]
    n_hp = w // LANES
    hw = n_hp * LANES
    cols = 4 * tq
    bd, t_new, _ = qsb.shape
    n_pages = page_table.shape[1]
    n_pool = cache_k.shape[0]
    past = n_pages * PAGE_SIZE
    n_subs = bd * (n_pages // pps)
    nq = s // tq
    subs_per_batch = (nq // 2) ** 2
    assert tk == 2 * tq and nq % 2 == 0 and n_pages % pps == 0 and b * subs_per_batch == n_subs
    ck = jnp.transpose(cache_k, (0, 2, 3, 1)).reshape(n_pool, w, PAGE_SIZE)
    cv = jnp.transpose(cache_v, (0, 2, 3, 1)).reshape(n_pool, w, PAGE_SIZE)
    g8 = jnp.tile(g_subln, N_ATT_HEADS).reshape(1, w)
    srows = 2 * N_ATT_HEADS * t_new
    g2 = jnp.concatenate([g_subln, g_subln]).reshape(LANES, 1)
    pos = lax.broadcasted_iota(jnp.int32, (s, LANES), 0)
    lane = lax.broadcasted_iota(jnp.int32, (s, LANES), 1)
    n_parts = len(L2E_PARTS)
    pf = jnp.where(lane < n_parts, pos // POS_SPLIT,
                   jnp.where(lane < N_POS_FEATS, pos % POS_SPLIT, 0)).astype(BF16)
    rel = (jnp.arange(tk, dtype=jnp.int32)[:, None] - (jnp.arange(cols, dtype=jnp.int32) % tq)[None, :]).astype(F32)
    const2 = lambda bi, hp, qi, pt: (0, 0)
    const3 = lambda bi, hp, qi, pt: (0, 0, 0)
    lam_specs = [pl.BlockSpec((1, D_QK), const2)] * 4
    grid_spec = pltpu.PrefetchScalarGridSpec(
        num_scalar_prefetch=1,
        grid=(b, w // hw, s // tq),
        in_specs=[pl.BlockSpec((1, hw, tq), lambda bi, hp, qi, pt: (bi, hp, qi)),
                  pl.BlockSpec((1, s, hw), lambda bi, hp, qi, pt: (bi, 0, hp), pipeline_mode=pl.Buffered(1)),
                  pl.BlockSpec((1, n_chunks, hw, tk), lambda bi, hp, qi, pt: (bi, 0, hp, 0),
                               pipeline_mode=pl.Buffered(1)),
                  pl.BlockSpec((s, LANES), const2),
                  pl.BlockSpec((tk, cols), const2),
                  *lam_specs,
                  pl.BlockSpec((LANES, 1), const2),
                  pl.BlockSpec((bd, t_new, w), const3),
                  pl.BlockSpec((bd, t_new, w), const3),
                  pl.BlockSpec((bd, t_new, w), const3),
                  pl.BlockSpec((1, w), const2),
                  pl.BlockSpec(memory_space=pl.ANY),
                  pl.BlockSpec(memory_space=pl.ANY)],
        out_specs=[pl.BlockSpec((1, tq, hw), lambda bi, hp, qi, pt: (bi, qi, hp)),
                   pl.BlockSpec((bd, t_new, w), const3)],
        scratch_shapes=[pltpu.VMEM((n_hp, 2 * LANES, cols), BF16),
                        pltpu.VMEM((n_hp, 1, cols), F32),
                        pltpu.VMEM((n_hp, LANES + EXTRA_ROWS, cols), F32),
                        pltpu.VMEM((srows, w), BF16),
                        pltpu.VMEM((srows, 1), F32),
                        pltpu.VMEM((srows, 1), F32),
                        pltpu.VMEM((w, srows), F32),
                        pltpu.VMEM((PAGE_SLOTS, pps, w, PAGE_SIZE), F32),
                        pltpu.VMEM((PAGE_SLOTS, pps, w, PAGE_SIZE), F32),
                        pltpu.SemaphoreType.DMA((2, PAGE_SLOTS))],
    )
    return pl.pallas_call(
        functools.partial(_attn_kernel, tq=tq, tk=tk, n_hp=n_hp, lam_init=lam_init,
                          pps=pps, subs_per_batch=subs_per_batch, n_subs=n_subs, past=past),
        grid_spec=grid_spec,
        out_shape=[jax.ShapeDtypeStruct((b, s, w), BF16), jax.ShapeDtypeStruct((bd, t_new, w), BF16)],
        compiler_params=_cparams("arbitrary", "arbitrary", "arbitrary"),
        name="attn",
    )(page_table, qtb, kb, vtb, pf, rel, *[p.reshape(1, D_QK) for p in lam_params], g2,
      qsb, ks, vs, g8, ck, cv)


def _outproj_kernel(a_ref, u_ref, vs_ref, x_ref, ga1_ref, sc2_ref, sh2_ref, wcat_ref, sbias_ref, wout_ref,
                    gpost_ref, gpre_ref, wr_cat_ref, br_ref, tril_ref, *rest, group_dispatch):
    if group_dispatch:
        x1_ref, pay_ref, info_ref, counts_ref, sg_ref, run_ref = rest
    else:
        x1_ref, h2_ref, gates_ref, sg_ref = rest
    tm = x_ref.shape[0]
    sgu_w = u_ref.shape[1]
    lane = lax.broadcasted_iota(jnp.int32, (CHUNK, LANES), 1)
    for ci in range(tm // CHUNK):
        rs = slice(ci * CHUNK, (ci + 1) * CHUNK)
        for gp in range(sgu_w // LANES):
            cs = slice(gp * LANES, (gp + 1) * LANES)
            vp = vs_ref[rs, cs]
            zero = jnp.zeros_like(vp)
            rhs = jnp.concatenate([jnp.where(lane < SGU_GC, vp, zero), jnp.where(lane >= SGU_GC, vp, zero)], axis=0)
            mixed = jnp.dot(wcat_ref[gp], rhs, preferred_element_type=F32) + sbias_ref[:, cs]
            sg_ref[rs, cs] = (u_ref[rs, cs].astype(F32) * mixed).astype(BF16)
    att_w = a_ref.shape[1]
    mix = (jnp.dot(a_ref[...], wout_ref[0:att_w, :], preferred_element_type=F32)
           + jnp.dot(sg_ref[...], wout_ref[att_w:, :], preferred_element_type=F32))
    x1 = x_ref[...] + ga1_ref[0] * _rms(mix, gpost_ref[...])
    x1_ref[...] = x1
    h2 = _rms(x1, gpre_ref[...]) * (1.0 + sc2_ref[0]) + sh2_ref[0]
    h2_hi = h2.astype(BF16)
    h2_lo = (h2 - h2_hi.astype(F32)).astype(BF16)
    hh = jnp.dot(h2_hi, wr_cat_ref[...], preferred_element_type=F32)
    logits = (hh[:, :LANES] + hh[:, LANES:]
              + jnp.dot(h2_lo, wr_cat_ref[:, :LANES], preferred_element_type=F32)) + br_ref[...]
    ln = lax.broadcasted_iota(jnp.int32, (tm, LANES), 1)
    big = jnp.int32(LANES)
    is_g = (ln >= N_EXPERTS) & (ln < N_EXPERTS + N_EXPERT_GROUPS)
    gl = jnp.where(is_g, logits, NEG_INF)
    gmax = jnp.max(gl, axis=-1, keepdims=True)
    g_idx = jnp.min(jnp.where(gl == gmax, ln, big), axis=-1, keepdims=True) - N_EXPERTS
    g_w = 1.0 / jnp.sum(jnp.exp(gl - gmax), axis=-1, keepdims=True)
    in_grp = (ln >= g_idx * EXPERTS_PER_GROUP) & (ln < (g_idx + 1) * EXPERTS_PER_GROUP)
    el = jnp.where(in_grp, logits, NEG_INF)
    emax = jnp.max(el, axis=-1, keepdims=True)
    ez = jnp.exp(el - emax)
    prob = ez / jnp.sum(ez, axis=-1, keepdims=True)
    p1 = jnp.max(prob, axis=-1, keepdims=True)
    i1 = jnp.min(jnp.where((prob == p1) & in_grp, ln, big), axis=-1, keepdims=True)
    rest = jnp.where(in_grp & (ln != i1), prob, -1.0)
    p2 = jnp.max(rest, axis=-1, keepdims=True)
    i2 = jnp.min(jnp.where(rest == p2, ln, big), axis=-1, keepdims=True)
    den = p1 + p2
    gates = jnp.where(ln == i1, p1 / den * g_w, 0.0) + jnp.where(ln == i2, p2 / den * g_w, 0.0)
    if not group_dispatch:
        h2_ref[...] = h2_hi
        gates_ref[...] = gates[:, :N_EXPERTS]
        return

    bits = pltpu.bitcast(h2_hi.astype(F32), jnp.uint32)
    half = PAYLOAD_H2_SEGS * LANES
    for s in range(PAYLOAD_H2_SEGS):
        lo = bits[:, s * LANES:(s + 1) * LANES]
        hi = bits[:, half + s * LANES:half + (s + 1) * LANES]
        pay_ref[s] = hi | lax.shift_right_logical(lo, jnp.uint32(16))
    pay_ref[PAYLOAD_H2_SEGS] = pltpu.bitcast(gates, jnp.uint32)

    @pl.when(pl.program_id(0) == 0)
    def _():
        run_ref[...] = jnp.zeros_like(run_ref)

    onehot = jnp.where(ln == g_idx, 1.0, 0.0).astype(BF16)
    csum = jnp.dot(tril_ref[...], onehot, preferred_element_type=F32) + run_ref[...]
    rank = jnp.sum(jnp.where(ln == g_idx, csum, 0.0), axis=-1, keepdims=True) - 1.0
    run_ref[...] = csum[tm - 1:tm, :]
    counts_ref[...] = csum[tm - 1:tm, :]
    info = jnp.where(ln == 0, g_idx, jnp.where(ln == 1, rank.astype(jnp.int32), 0))
    info_ref[...] = info[:, :INFO_W]


def _outproj(a, u, vsb, x, ga1, sc2, sh2, wcat, sbias, w_out_bf, g_post, g_pre, wr_cat, br,
             *, tm, tiles_per_mod, group_dispatch):
    n, d = x.shape
    att_w = a.shape[1]
    sgu_w = u.shape[1]
    r = ga1.shape[1]
    row = lambda i: (i, 0)
    modmap = lambda i: (i // tiles_per_mod, 0, 0)
    const = lambda i: (0, 0)
    tril = jnp.tril(jnp.ones((tm, tm), BF16))
    if group_dispatch:
        out_specs = [pl.BlockSpec((tm, d), row),
                     pl.BlockSpec((PAYLOAD_SEGS, tm, LANES), lambda i: (0, i, 0)),
                     pl.BlockSpec((tm, INFO_W), row),
                     pl.BlockSpec((1, LANES), const)]
        out_shape = [jax.ShapeDtypeStruct((n, d), F32),
                     jax.ShapeDtypeStruct((PAYLOAD_SEGS, n, LANES), jnp.uint32),
                     jax.ShapeDtypeStruct((n, INFO_W), jnp.int32),
                     jax.ShapeDtypeStruct((1, LANES), F32)]
        scratch = [pltpu.VMEM((tm, sgu_w), BF16), pltpu.VMEM((1, LANES), F32)]
    else:
        out_specs = [pl.BlockSpec((tm, d), row),
                     pl.BlockSpec((tm, d), row),
                     pl.BlockSpec((tm, N_EXPERTS), row)]
        out_shape = [jax.ShapeDtypeStruct((n, d), F32),
                     jax.ShapeDtypeStruct((n, d), BF16),
                     jax.ShapeDtypeStruct((n, N_EXPERTS), F32)]
        scratch = [pltpu.VMEM((tm, sgu_w), BF16)]
    return pl.pallas_call(
        functools.partial(_outproj_kernel, group_dispatch=group_dispatch),
        grid=(n // tm,),
        in_specs=[pl.BlockSpec((tm, att_w), row),
                  pl.BlockSpec((tm, sgu_w), row),
                  pl.BlockSpec((tm, sgu_w), row),
                  pl.BlockSpec((tm, d), row),
                  pl.BlockSpec((1, r, d), modmap),
                  pl.BlockSpec((1, r, d), modmap),
                  pl.BlockSpec((1, r, d), modmap),
                  pl.BlockSpec(wcat.shape, lambda i: (0, 0, 0)),
                  pl.BlockSpec(sbias.shape, const),
                  pl.BlockSpec(w_out_bf.shape, const),
                  pl.BlockSpec((1, d), const),
                  pl.BlockSpec((1, d), const),
                  pl.BlockSpec(wr_cat.shape, const),
                  pl.BlockSpec((1, LANES), const),
                  pl.BlockSpec((tm, tm), const)],
        out_specs=out_specs,
        out_shape=out_shape,
        scratch_shapes=scratch,
        compiler_params=_cparams("arbitrary"),
        name="outproj",
    )(a, u, vsb, x, ga1, sc2, sh2, wcat, sbias, w_out_bf, g_post.reshape(1, d), g_pre.reshape(1, d),
      wr_cat, br, tril)


def _moe_kernel(h_ref, gates_ref, x1_ref, ga2_ref, gpost_ref, wg_ref, wu_ref, wd_ref, o_ref, acc_ref):
    e = pl.program_id(1)

    @pl.when(e == 0)
    def _():
        acc_ref[...] = jnp.zeros_like(acc_ref)

    h = h_ref[...]
    ln = lax.broadcasted_iota(jnp.int32, gates_ref.shape, 1)
    per_step = wg_ref.shape[0]
    y = jnp.zeros(acc_ref.shape, F32)
    for el in range(per_step):
        gate = jnp.dot(h, wg_ref[el], preferred_element_type=F32)
        up = jnp.dot(h, wu_ref[el], preferred_element_type=F32)
        hdn = (gate * jax.nn.sigmoid(gate) * up).astype(BF16)
        ge = jnp.sum(jnp.where(ln == e * per_step + el, gates_ref[...], 0.0), axis=-1, keepdims=True)
        y = y + ge * jnp.dot(hdn, wd_ref[el], preferred_element_type=F32)
    acc_ref[...] += y

    @pl.when(e == pl.num_programs(1) - 1)
    def _():
        o_ref[...] = x1_ref[...] + ga2_ref[0] * _rms(acc_ref[...], gpost_ref[...])


def _moe(h2, gates, x1, ga2, g_post, wg_bf, wu_bf, wd_bf, *, tm, tiles_per_mod):
    n, d = x1.shape
    n_e, _, de = wg_bf.shape
    r = ga2.shape[1]
    row = lambda i, e: (i, 0)
    return pl.pallas_call(
        _moe_kernel,
        grid=(n // tm, n_e // EXPERTS_PER_GROUP),
        in_specs=[pl.BlockSpec((tm, d), row),
                  pl.BlockSpec((tm, n_e), row),
                  pl.BlockSpec((tm, d), row),
                  pl.BlockSpec((1, r, d), lambda i, e: (i // tiles_per_mod, 0, 0)),
                  pl.BlockSpec((1, d), lambda i, e: (0, 0)),
                  pl.BlockSpec((EXPERTS_PER_GROUP, d, de), lambda i, e: (e, 0, 0)),
                  pl.BlockSpec((EXPERTS_PER_GROUP, d, de), lambda i, e: (e, 0, 0)),
                  pl.BlockSpec((EXPERTS_PER_GROUP, de, d), lambda i, e: (e, 0, 0))],
        out_specs=pl.BlockSpec((tm, d), row),
        out_shape=jax.ShapeDtypeStruct((n, d), F32),
        scratch_shapes=[pltpu.VMEM((tm, d), F32)],
        compiler_params=_cparams("arbitrary", "arbitrary"),
        name="moe",
    )(h2, gates, x1, ga2, g_post.reshape(1, d), wg_bf, wu_bf, wd_bf)


def _sc_mesh():
    return plsc.VectorSubcoreMesh(core_axis_name="core", subcore_axis_name="subcore")


def _sc_scatter_rows(x, idx, n_out_rows):
    n_rows, width = x.shape

    @pl.kernel(out_type=jax.ShapeDtypeStruct((n_out_rows, width), x.dtype), mesh=_sc_mesh(), scratch_types=[])
    def scatter(x_hbm, i_hbm, o_hbm):
        def body(x_vmem, i_vmem):
            pltpu.sync_copy(x_vmem, o_hbm.at[i_vmem.at[0]])

        pltpu.emit_pipeline(
            body,
            grid=(n_rows // SC_WINDOW,),
            in_specs=[pl.BlockSpec((SC_WINDOW, width), lambda i: (i, 0)),
                      pl.BlockSpec((1, SC_WINDOW), lambda i: (0, i))],
            out_specs=[],
            core_axis_name=("core", "subcore"),
            dimension_semantics=(pltpu.PARALLEL,),
        )(x_hbm, i_hbm)

    return scatter(x, idx.reshape(1, n_rows))


def _sc_gather_rows(x, idx):
    n_rows = idx.shape[0]
    width = x.shape[1]

    @pl.kernel(out_type=jax.ShapeDtypeStruct((n_rows, width), x.dtype), mesh=_sc_mesh(), scratch_types=[])
    def gather(x_hbm, i_hbm, o_hbm):
        def body(i_vmem, o_vmem):
            pltpu.sync_copy(x_hbm.at[i_vmem.at[0]], o_vmem)

        pltpu.emit_pipeline(
            body,
            grid=(n_rows // SC_WINDOW,),
            in_specs=[pl.BlockSpec((1, SC_WINDOW), lambda i: (0, i))],
            out_specs=[pl.BlockSpec((SC_WINDOW, width), lambda i: (i, 0))],
            core_axis_name=("core", "subcore"),
            dimension_semantics=(pltpu.PARALLEL,),
        )(i_hbm, o_hbm)

    return gather(x, idx.reshape(1, n_rows))


def _group_moe_kernel(tg_ref, nused_ref, xs_ref, gpost_ref, wg_ref, wu_ref, wd_ref, rs_ref):
    t = pl.program_id(0)
    tm = xs_ref.shape[1]

    @pl.when(t >= nused_ref[0])
    def _():
        rs_ref[...] = jnp.zeros_like(rs_ref)

    @pl.when(t < nused_ref[0])
    def _():
        words = [xs_ref[s] for s in range(PAYLOAD_H2_SEGS)]
        lo = [pltpu.bitcast(lax.shift_left(w, jnp.uint32(16)), F32) for w in words]
        hi = [pltpu.bitcast(w & jnp.uint32(0xFFFF0000), F32) for w in words]
        x = jnp.concatenate(lo + hi, axis=1).astype(BF16)
        gates = pltpu.bitcast(xs_ref[PAYLOAD_H2_SEGS], F32)
        ln = lax.broadcasted_iota(jnp.int32, (tm, LANES), 1)
        first = tg_ref[t] * EXPERTS_PER_GROUP
        y = jnp.zeros((tm, x.shape[1]), F32)
        for el in range(EXPERTS_PER_GROUP):
            ge = jnp.sum(jnp.where(ln == first + el, gates, 0.0), axis=-1, keepdims=True)
            gate = jnp.dot(x, wg_ref[el], preferred_element_type=F32)
            up = jnp.dot(x, wu_ref[el], preferred_element_type=F32)
            hdn = (gate * jax.nn.sigmoid(gate) * up).astype(BF16)
            y = y + ge * jnp.dot(hdn, wd_ref[el], preferred_element_type=F32)
        r = _rms(y, gpost_ref[...])
        for s in range(rs_ref.shape[0]):
            rs_ref[s] = r[:, s * LANES:(s + 1) * LANES]


def _group_moe(xs, tile_group, n_used, g_post, wg_bf, wu_bf, wd_bf, *, tm):
    _, p_rows, _ = xs.shape
    n_e, d, de = wg_bf.shape
    g = EXPERTS_PER_GROUP
    grid_spec = pltpu.PrefetchScalarGridSpec(
        num_scalar_prefetch=2,
        grid=(p_rows // tm,),
        in_specs=[pl.BlockSpec((PAYLOAD_SEGS, tm, LANES), lambda t, tg, nu: (0, t, 0)),
                  pl.BlockSpec((1, d), lambda t, tg, nu: (0, 0)),
                  pl.BlockSpec((g, d, de), lambda t, tg, nu: (tg[t], 0, 0)),
                  pl.BlockSpec((g, d, de), lambda t, tg, nu: (tg[t], 0, 0)),
                  pl.BlockSpec((g, de, d), lambda t, tg, nu: (tg[t], 0, 0))],
        out_specs=pl.BlockSpec((d // LANES, tm, LANES), lambda t, tg, nu: (0, t, 0)),
    )
    return pl.pallas_call(
        _group_moe_kernel,
        grid_spec=grid_spec,
        out_shape=jax.ShapeDtypeStruct((d // LANES, p_rows, LANES), F32),
        compiler_params=_cparams("arbitrary"),
        name="group_moe",
    )(tile_group, n_used, xs, g_post.reshape(1, d), wg_bf, wu_bf, wd_bf)


def _residual_kernel(x1_ref, ga2_ref, r_ref, o_ref):
    r = jnp.concatenate([r_ref[s] for s in range(r_ref.shape[0])], axis=1)
    o_ref[...] = x1_ref[...] + ga2_ref[0] * r


def _residual(x1, ga2, r_segs, *, tm, tiles_per_mod):
    n, d = x1.shape
    return pl.pallas_call(
        _residual_kernel,
        grid=(n // tm,),
        in_specs=[pl.BlockSpec((tm, d), lambda i: (i, 0)),
                  pl.BlockSpec((1, ga2.shape[1], d), lambda i: (i // tiles_per_mod, 0, 0)),
                  pl.BlockSpec((d // LANES, tm, LANES), lambda i: (0, i, 0))],
        out_specs=pl.BlockSpec((tm, d), lambda i: (i, 0)),
        out_shape=jax.ShapeDtypeStruct((n, d), F32),
        compiler_params=_cparams("arbitrary"),
        name="residual",
    )(x1, ga2, r_segs)


def _group_plan(info, counts, n_tokens, tm):
    n_tiles = n_tokens // tm + N_EXPERT_GROUPS
    cnt = counts[0, :N_EXPERT_GROUPS].astype(jnp.int32)
    padded = (cnt + tm - 1) // tm * tm
    ends = jnp.cumsum(padded)
    starts = ends - padded
    gid, rank = info[:, 0], info[:, 1]
    onehot = (gid[:, None] == jnp.arange(N_EXPERT_GROUPS, dtype=jnp.int32)[None, :]).astype(jnp.int32)
    pos = jnp.sum(onehot * starts[None, :], axis=1) + rank
    tile_ends = ends // tm
    t = jnp.arange(n_tiles, dtype=jnp.int32)
    tile_group = jnp.minimum(jnp.sum((t[:, None] >= tile_ends[None, :]).astype(jnp.int32), axis=1),
                             N_EXPERT_GROUPS - 1)
    return pos, tile_group, tile_ends[-1:], n_tiles * tm


def _sgu_weights(ws, bs, chunk_len):
    causal = jnp.tril(jnp.ones((chunk_len, chunk_len), ws.dtype))
    wm = ws[:, :chunk_len, :chunk_len] * causal
    reps = CHUNK // chunk_len
    if reps > 1:
        eye = jnp.eye(reps, dtype=ws.dtype)
        wm = jnp.einsum("ab,gts->gatbs", eye, wm).reshape(N_SGU_GROUPS, CHUNK, CHUNK)
    wcat = jnp.concatenate([wm[0::2], wm[1::2]], axis=2).astype(BF16)
    bt = jnp.tile(bs[:, :chunk_len].T, (reps, 1))
    sbias = jnp.repeat(bt, SGU_GC, axis=1)
    return wcat, sbias


def _split_mod(mod):
    return [m[:, None, :] for m in jnp.split(mod, 6, axis=-1)]


def kernel(x_prompt, x_sample, cache_k, cache_v, page_table, c_prompt, c_sample, w_ada, b_ada, g_pre_mix, g_post_mix, g_pre_ffn, g_post_ffn, w_in, lam_q1, lam_k1, lam_q2, lam_k2, g_subln, sgu_ln_g, sgu_ln_b, sgu_ws, sgu_bs, w_out, w_rg, b_rg, w_re, b_re, w_gate, w_up, w_down):
    depth = w_in.shape[0]
    assert depth == 1
    l = 0
    lam_init = 0.8 - 0.6 * math.exp(-0.3 * l)
    bp, sp, d = x_prompt.shape
    bs_, ts, _ = x_sample.shape
    n_s = bs_ * ts

    mod = _ada(jnp.concatenate([c_prompt, c_sample], axis=0), w_ada[l], b_ada[l])
    sh1p, sc1p, ga1p, sh2p, sc2p, ga2p = _split_mod(mod[:bp])
    rep = lambda m: jnp.repeat(m, ts, axis=0).reshape(1, n_s, d)
    sh1s, sc1s, ga1s, sh2s, sc2s, ga2s = [rep(m) for m in jnp.split(mod[bp:], 6, axis=-1)]

    w_in_bf = w_in[l].astype(BF16)
    wqv_t_bf = jnp.concatenate([w_in_bf[:, 0:ATT_W], w_in_bf[:, 2 * ATT_W:3 * ATT_W]], axis=1).T
    w_out_bf = w_out[l].astype(BF16)
    wg_bf = w_gate[l].astype(BF16)
    wu_bf = w_up[l].astype(BF16)
    wd_bf = w_down[l].astype(BF16)
    wr = jnp.concatenate([w_re[l], w_rg[l]], axis=1)
    wr = jnp.pad(wr, ((0, 0), (0, LANES - wr.shape[1])))
    wr_hi = wr.astype(BF16)
    wr_cat = jnp.concatenate([wr_hi, (wr - wr_hi.astype(F32)).astype(BF16)], axis=1)
    br = jnp.pad(jnp.concatenate([b_re[l], b_rg[l]]), (0, LANES - N_EXPERTS - N_EXPERT_GROUPS)).reshape(1, LANES)
    lam_params = (lam_q1[l], lam_k1[l], lam_q2[l], lam_k2[l])

    tm_p = 512
    xp = x_prompt.reshape(bp * sp, d)
    ktp, vtp, qtb, kb, vtb, u, vsb, vs_last = _inproj(
        xp, sc1p, sh1p, g_pre_mix[l], w_in_bf, wqv_t_bf, sgu_ln_g[l], sgu_ln_b[l],
        tm=tm_p, tiles_per_seq=sp // tm_p, last_rows=CHUNK, transposed_qv=True)
    w3 = ATT_W
    xs = x_sample.reshape(n_s, d)
    ks, vs_, qbs, us, vsbs, vs_last_s = _inproj(
        xs, sc1s, sh1s, g_pre_mix[l], w_in_bf, wqv_t_bf, sgu_ln_g[l], sgu_ln_b[l],
        tm=n_s, tiles_per_seq=1, last_rows=n_s, transposed_qv=False)
    a, a_s = _attn(qtb, kb.reshape(bp, sp, w3), vtb, page_table, qbs.reshape(bs_, ts, w3),
                   ks.reshape(bs_, ts, w3), vs_.reshape(bs_, ts, w3), cache_k[l], cache_v[l],
                   lam_params, g_subln[l], lam_init)
    kp = jnp.transpose(ktp.reshape(bp, N_ATT_HEADS, 2 * D_QK, sp), (0, 3, 1, 2))
    vp = jnp.transpose(vtp.reshape(bp, N_ATT_HEADS, D_V, sp), (0, 3, 1, 2))
    wcat_p, sbias_p = _sgu_weights(sgu_ws[l], sgu_bs[l], CHUNK)
    n_p = bp * sp
    x1, payload, info, counts = _outproj(
        a.reshape(n_p, w3), u, vsb, xp, ga1p, sc2p, sh2p, wcat_p, sbias_p, w_out_bf,
        g_post_mix[l], g_pre_ffn[l], wr_cat, br, tm=tm_p, tiles_per_mod=sp // tm_p, group_dispatch=True)
    tm_moe = 1024
    pos, tile_group, n_used, p_rows = _group_plan(info, counts, n_p, tm_moe)
    seg_p = lambda k: (jnp.arange(k, dtype=jnp.int32)[:, None] * p_rows + pos[None, :]).reshape(-1)
    sorted_rows = _sc_scatter_rows(payload.reshape(PAYLOAD_SEGS * n_p, LANES), seg_p(PAYLOAD_SEGS),
                                   PAYLOAD_SEGS * p_rows)
    rs = _group_moe(sorted_rows.reshape(PAYLOAD_SEGS, p_rows, LANES), tile_group, n_used, g_post_ffn[l],
                    wg_bf, wu_bf, wd_bf, tm=tm_moe)
    n_seg = d // LANES
    r_tok = _sc_gather_rows(rs.reshape(n_seg * p_rows, LANES), seg_p(n_seg))
    yp = _residual(x1, ga2p, r_tok.reshape(n_seg, n_p, LANES), tm=tm_p, tiles_per_mod=sp // tm_p)

    wcat_s, sbias_s = _sgu_weights(sgu_ws[l], sgu_bs[l], ts)
    x1s, h2s, gates_s = _outproj(a_s.reshape(n_s, w3), us, vsbs, xs, ga1s, sc2s, sh2s, wcat_s, sbias_s, w_out_bf,
                                 g_post_mix[l], g_pre_ffn[l], wr_cat, br, tm=n_s, tiles_per_mod=1,
                                 group_dispatch=False)
    ys = _moe(h2s, gates_s, x1s, ga2s, g_post_ffn[l], wg_bf, wu_bf, wd_bf, tm=n_s, tiles_per_mod=1)

    return (yp.reshape(bp, sp, d), ys.reshape(bs_, ts, d),
            kp[None], vp[None],
            vs_last.reshape(1, bp, CHUNK, -1),
            ks.reshape(1, bs_, ts, N_ATT_HEADS, 2 * D_QK), vs_.reshape(1, bs_, ts, N_ATT_HEADS, D_V),
            vs_last_s.reshape(1, bs_, ts, -1))
```

```python
import functools
import math
import struct

import jax
import jax.numpy as jnp
from jax import lax
from jax.experimental import pallas as pl
from jax.experimental.pallas import tpu as pltpu
from jax.experimental.pallas import tpu_sc as plsc

F32 = jnp.float32
BF16 = jnp.bfloat16

EPS = 1e-6
N_ATT_HEADS = 8
D_QK = 32
D_V = 64
ATT_W = N_ATT_HEADS * D_V
N_SGU_GROUPS = 8
SGU_GC = 64
SGU_W = N_SGU_GROUPS * SGU_GC
CHUNK = 128
N_EXPERT_GROUPS = 4
EXPERTS_PER_GROUP = 4
N_EXPERTS = 16
PAGE_SIZE = 128
LANES = 128
VMEM_LIMIT = 56 * 1024 * 1024
NEG_INF = float("-inf")
LOG2E = math.log2(math.e)
PAYLOAD_H2_SEGS = 4
PAYLOAD_SEGS = PAYLOAD_H2_SEGS + 1
INFO_W = 8
SC_WINDOW = 128
PAGE_SLOTS = 2


def _cparams(*sem):
    return pltpu.CompilerParams(dimension_semantics=sem, vmem_limit_bytes=VMEM_LIMIT)


def _rms(x, g):
    return x * lax.rsqrt(jnp.mean(x * x, axis=-1, keepdims=True) + EPS) * g


def _ada_kernel(c_ref, w_ref, b_ref, o_ref):
    c = c_ref[...]
    s = c * jax.nn.sigmoid(c)
    o_ref[...] = jnp.dot(s.astype(BF16), w_ref[...].astype(BF16), preferred_element_type=F32) + b_ref[...]


def _ada(c, w_ada, b_ada):
    n, d = c.shape
    nout = w_ada.shape[1]
    tn = d
    return pl.pallas_call(
        _ada_kernel,
        grid=(nout // tn,),
        in_specs=[pl.BlockSpec((n, d), lambda j: (0, 0)),
                  pl.BlockSpec((d, tn), lambda j: (0, j)),
                  pl.BlockSpec((1, tn), lambda j: (0, j))],
        out_specs=pl.BlockSpec((n, tn), lambda j: (0, j)),
        out_shape=jax.ShapeDtypeStruct((n, nout), F32),
        compiler_params=_cparams("arbitrary"),
        name="ada",
    )(c, w_ada, b_ada.reshape(1, nout))


Q_SCALE = D_QK ** -0.5 * LOG2E


def _inproj_kernel(x_ref, sc_ref, sh_ref, g_ref, w_ref, wqv_t_ref, lng_ref, lnb_ref, *out_refs,
                   last_rows, transposed_qv):
    x = x_ref[...]
    h = (_rms(x, g_ref[...]) * (1.0 + sc_ref[0]) + sh_ref[0]).astype(BF16)

    def proj(lo):
        return jnp.dot(h, w_ref[:, lo:lo + ATT_W], preferred_element_type=F32)

    def proj_t(lo):
        return lax.dot_general(wqv_t_ref[lo:lo + ATT_W, :], h, (((1,), (1,)), ((), ())),
                               preferred_element_type=F32)

    zk = proj(ATT_W)
    if transposed_qv:
        kt_ref, vt_ref, qtb_ref, kb_ref, vtb_ref, u_ref, vsb_ref, vsl_ref = out_refs
        kt_ref[0] = zk.T
        kb_ref[...] = zk.astype(BF16)
        qtb_ref[0] = (proj_t(0) * Q_SCALE).astype(BF16)
        zvt = proj_t(ATT_W)
        vt_ref[0] = zvt
        vtb_ref[0, 0] = zvt.astype(BF16)
    else:
        k_ref, v_ref, qb_ref, u_ref, vsb_ref, vsl_ref = out_refs
        k_ref[...] = zk
        qb_ref[...] = (proj(0) * Q_SCALE).astype(BF16)
        v_ref[...] = proj(2 * ATT_W)
    u_ref[...] = jax.nn.gelu(proj(3 * ATT_W)).astype(BF16)
    gs = jax.nn.gelu(proj(3 * ATT_W + SGU_W))
    mu = jnp.mean(gs, axis=-1, keepdims=True)
    xc = gs - mu
    vs = xc * lax.rsqrt(jnp.mean(xc * xc, axis=-1, keepdims=True) + EPS) * lng_ref[...] + lnb_ref[...]
    vsb_ref[...] = vs.astype(BF16)
    tm = vs.shape[0]
    vsl_ref[0] = vs[tm - last_rows:, :]


def _inproj(x, sc, sh, g, w_in_bf, wqv_t_bf, ln_g, ln_b, *, tm, tiles_per_seq, last_rows, transposed_qv):
    n, d = x.shape
    in_w = w_in_bf.shape[1]
    r = sc.shape[1]
    n_tiles = n // tm
    n_seq = n_tiles // tiles_per_seq
    s_len = tiles_per_seq * tm
    row = lambda i: (i, 0)
    modmap = lambda i: (i // tiles_per_seq, 0, 0)
    const = lambda i: (0, 0)
    col_t = lambda i: (i // tiles_per_seq, 0, i % tiles_per_seq)
    row_spec = pl.BlockSpec((tm, ATT_W), row)
    tail = [(pl.BlockSpec((tm, SGU_W), row), jax.ShapeDtypeStruct((n, SGU_W), BF16)),
            (pl.BlockSpec((tm, SGU_W), row), jax.ShapeDtypeStruct((n, SGU_W), BF16)),
            (pl.BlockSpec((1, last_rows, SGU_W), lambda i: (i // tiles_per_seq, 0, 0)),
             jax.ShapeDtypeStruct((n_seq, last_rows, SGU_W), F32))]
    if transposed_qv:
        head = [(pl.BlockSpec((1, ATT_W, tm), col_t), jax.ShapeDtypeStruct((n_seq, ATT_W, s_len), F32)),
                (pl.BlockSpec((1, ATT_W, tm), col_t), jax.ShapeDtypeStruct((n_seq, ATT_W, s_len), F32)),
                (pl.BlockSpec((1, ATT_W, tm), col_t), jax.ShapeDtypeStruct((n_seq, ATT_W, s_len), BF16)),
                (row_spec, jax.ShapeDtypeStruct((n, ATT_W), BF16)),
                (pl.BlockSpec((1, 1, ATT_W, tm), lambda i: (i // tiles_per_seq, i % tiles_per_seq, 0, 0)),
                 jax.ShapeDtypeStruct((n_seq, tiles_per_seq, ATT_W, tm), BF16))]
    else:
        head = [(row_spec, jax.ShapeDtypeStruct((n, ATT_W), F32)),
                (row_spec, jax.ShapeDtypeStruct((n, ATT_W), F32)),
                (row_spec, jax.ShapeDtypeStruct((n, ATT_W), BF16))]
    specs, shapes = zip(*(head + tail))
    return pl.pallas_call(
        functools.partial(_inproj_kernel, last_rows=last_rows, transposed_qv=transposed_qv),
        grid=(n_tiles,),
        in_specs=[pl.BlockSpec((tm, d), row),
                  pl.BlockSpec((1, r, d), modmap),
                  pl.BlockSpec((1, r, d), modmap),
                  pl.BlockSpec((1, d), const),
                  pl.BlockSpec((d, in_w), const),
                  pl.BlockSpec(wqv_t_bf.shape, const),
                  pl.BlockSpec((1, SGU_W), const),
                  pl.BlockSpec((1, SGU_W), const)],
        out_specs=list(specs),
        out_shape=list(shapes),
        compiler_params=_cparams("arbitrary"),
        name="inproj",
    )(x, sc, sh, g.reshape(1, d), w_in_bf, wqv_t_bf, ln_g.reshape(1, SGU_W), ln_b.reshape(1, SGU_W))


def _diff_lambda(lq1, lk1, lq2, lk2, lam_init):
    return (jnp.exp(jnp.sum(lq1 * lk1, axis=-1, keepdims=True))
            - jnp.exp(jnp.sum(lq2 * lk2, axis=-1, keepdims=True)) + lam_init)


def _pair_slopes(hp):
    s0 = jnp.where(hp == 0, 2.0 ** -1, jnp.where(hp == 1, 2.0 ** -3, jnp.where(hp == 2, 2.0 ** -5, 2.0 ** -7)))
    return s0.astype(F32), (s0 * 0.5).astype(F32)


def _bf16_round(x):
    bits = struct.unpack("<I", struct.pack("<f", x))[0]
    bits = (bits + 0x7FFF + ((bits >> 16) & 1)) & 0xFFFF0000
    return struct.unpack("<f", struct.pack("<I", bits))[0]


L2E_PARTS = []
_rest = LOG2E
for _ in range(3):
    L2E_PARTS.append(_bf16_round(_rest))
    _rest -= L2E_PARTS[-1]
POS_SPLIT = 64
N_POS_FEATS = 2 * len(L2E_PARTS)
EXTRA_ROWS = 16


def _attn_kernel(pt_ref, qt_ref, k_ref, vt_ref, pf_ref, rel_ref, lq1_ref, lk1_ref, lq2_ref, lk2_ref, g_ref,
                 qs_ref, kn_ref, vn_ref, g8_ref, ck_hbm, cv_hbm, o_ref, os_ref,
                 qst_ref, m_ref, acc_ref, qbd_ref, ms_ref, ls_ref, accs_ref, kbuf_ref, vbuf_ref, sem_ref,
                 *, tq, tk, n_hp, lam_init, pps, subs_per_batch, n_subs, past):
    qi = pl.program_id(2)
    cols = 4 * tq

    t_new, w = qs_ref.shape[1], qs_ref.shape[2]
    step = pl.program_id(0) * pl.num_programs(2) + qi
    subs_per_seq = past // (pps * PAGE_SIZE)
    half_q = qi // 2
    sub_base = pl.program_id(0) * subs_per_batch + half_q * half_q + (qi % 2) * half_q
    n_hj = 2 * N_ATT_HEADS
    srows = n_hj * t_new
    blk = pps * PAGE_SIZE
    srow_id = lax.broadcasted_iota(jnp.int32, (srows, 1), 0)
    shead = srow_id // (2 * t_new)
    sslope = jnp.zeros((srows, 1), F32)
    for h in range(N_ATT_HEADS):
        sslope = jnp.where(shead == h, 2.0 ** -(h + 1), sslope)
    sslope = sslope * LOG2E
    sqpos = (past + srow_id % t_new).astype(F32)

    def page_copies(k, slot):
        kc = jnp.minimum(k, n_subs - 1)
        seq_i, sub_i = kc // subs_per_seq, kc % subs_per_seq
        copies = []
        for i in range(pps):
            page = pt_ref[seq_i, sub_i * pps + i]
            copies.append(pltpu.make_async_copy(ck_hbm.at[page], kbuf_ref.at[slot, i], sem_ref.at[0, slot]))
            copies.append(pltpu.make_async_copy(cv_hbm.at[page], vbuf_ref.at[slot, i], sem_ref.at[1, slot]))
        return copies

    eye = (lax.broadcasted_iota(jnp.int32, (srows, srows), 0) == lax.broadcasted_iota(jnp.int32, (srows, srows), 1))

    def sample_update(s, vmat, from_pages):
        m_old = ms_ref[...]
        m_new = jnp.maximum(m_old, jnp.max(s, axis=-1, keepdims=True))
        p = jnp.exp2(s - m_new)
        corr = jnp.exp2(m_old - m_new)
        ls_ref[...] = corr * ls_ref[...] + jnp.sum(p, axis=-1, keepdims=True)
        corr_row = jnp.sum(jnp.where(eye, corr, 0.0), axis=0, keepdims=True)
        if from_pages:
            pv_t = jnp.dot(vmat, p.T.astype(BF16), preferred_element_type=F32)
        else:
            pv_t = jnp.dot(p, vmat, preferred_element_type=F32).T
        accs_ref[...] = accs_ref[...] * corr_row + pv_t
        ms_ref[...] = m_new

    def sample_begin(k):
        @pl.when(k % subs_per_seq == 0)
        def _():
            q = qs_ref[k // subs_per_seq]
            lane = lax.broadcasted_iota(jnp.int32, (t_new, w), 1)
            for hj in range(n_hj):
                qbd_ref[hj * t_new:(hj + 1) * t_new, :] = jnp.where(
                    (lane >= D_QK * hj) & (lane < D_QK * (hj + 1)), q, jnp.zeros_like(q))
            ms_ref[...] = jnp.full((srows, 1), NEG_INF, F32)
            ls_ref[...] = jnp.zeros((srows, 1), F32)
            accs_ref[...] = jnp.zeros((w, srows), F32)

    def sample_sub(k):
        slot = k % PAGE_SLOTS
        for c in page_copies(k, slot):
            c.wait()
        ahead = k + PAGE_SLOTS - 1
        for c in page_copies(ahead, ahead % PAGE_SLOTS):
            c.start()
        kt_all = jnp.concatenate([kbuf_ref[slot, i].astype(BF16) for i in range(pps)], axis=1)
        vt_all = jnp.concatenate([vbuf_ref[slot, i].astype(BF16) for i in range(pps)], axis=1)
        s = jnp.dot(qbd_ref[...], kt_all, preferred_element_type=F32)
        kpos = ((k % subs_per_seq) * blk + lax.broadcasted_iota(jnp.int32, (1, blk), 1)).astype(F32)
        sample_update(s - sslope * (sqpos - kpos), vt_all, True)

    @pl.when(step == 0)
    def _():
        for k0 in range(PAGE_SLOTS - 1):
            for c in page_copies(k0, k0):
                c.start()

    def sample_end(k):
        @pl.when(k % subs_per_seq == subs_per_seq - 1)
        def _():
            sample_finish(k // subs_per_seq)

        @pl.when(k == n_subs - 1)
        def _():
            for extra in range(n_subs, n_subs + PAGE_SLOTS - 1):
                for c in page_copies(extra, extra % PAGE_SLOTS):
                    c.wait()

    def sample_finish(seq):
        kn = kn_ref[seq]
        sn = lax.dot_general(qbd_ref[...].astype(F32), kn, (((1,), (1,)), ((), ())), preferred_element_type=F32)
        kposn = (past + lax.broadcasted_iota(jnp.int32, (1, t_new), 1)).astype(F32)
        dist = sqpos - kposn
        sample_update(jnp.where(dist >= 0, sn - sslope * dist, NEG_INF), vn_ref[seq], False)
        lam_s = _diff_lambda(lq1_ref[...], lk1_ref[...], lq2_ref[...], lk2_ref[...], lam_init)
        o = accs_ref[...].T / ls_ref[...]
        lane = lax.broadcasted_iota(jnp.int32, (t_new, w), 1)
        a = jnp.zeros((t_new, w), F32)
        for h in range(N_ATT_HEADS):
            o1 = o[(2 * h) * t_new:(2 * h + 1) * t_new]
            o2 = o[(2 * h + 1) * t_new:(2 * h + 2) * t_new]
            a = jnp.where((lane >= D_V * h) & (lane < D_V * (h + 1)), o1 - lam_s * o2, a)
        a2 = a * a
        msq = jnp.zeros((t_new, w), F32)
        for h in range(N_ATT_HEADS):
            in_h = (lane >= D_V * h) & (lane < D_V * (h + 1))
            ssh = jnp.sum(jnp.where(in_h, a2, 0.0), axis=-1, keepdims=True)
            msq = jnp.where(in_h, ssh, msq)
        os_ref[seq] = (a * lax.rsqrt(msq * (1.0 / D_V) + EPS) * g8_ref[...] * (1.0 - lam_init)).astype(BF16)

    sub = lax.broadcasted_iota(jnp.int32, (LANES, tq), 0)
    col = lax.broadcasted_iota(jnp.int32, (EXTRA_ROWS, cols), 1)
    row = lax.broadcasted_iota(jnp.int32, (EXTRA_ROWS, cols), 0)
    feat = jnp.zeros((EXTRA_ROWS, cols), F32)
    for i, part in enumerate(L2E_PARTS):
        feat = jnp.where(row == i, POS_SPLIT * part, feat)
        feat = jnp.where(row == len(L2E_PARTS) + i, part, feat)
    for h in range(n_hp):
        qt = qt_ref[0, h * LANES:(h + 1) * LANES, :]
        for c in range(4):
            qst_ref[h, 0:LANES, c * tq:(c + 1) * tq] = jnp.where((sub >= D_QK * c) & (sub < D_QK * (c + 1)), qt,
                                                                 jnp.zeros_like(qt))
        s0, s1 = _pair_slopes(pl.program_id(1) * n_hp + h)
        qst_ref[h, LANES:LANES + EXTRA_ROWS, :] = (jnp.where(col < 2 * tq, s0, s1) * feat).astype(BF16)
        qst_ref[h, LANES + EXTRA_ROWS:, :] = jnp.zeros((LANES - EXTRA_ROWS, cols), BF16)
    m_ref[...] = jnp.full(m_ref.shape, NEG_INF, F32)
    acc_ref[...] = jnp.zeros(acc_ref.shape, F32)

    def chunks(kj0, n_sub, masked, rows=tk, between=None):
        def scores(h):
            hl = slice(h * LANES, (h + 1) * LANES)
            sts = []
            for i in range(n_sub):
                start = pl.multiple_of((kj0 + i) * tk, tk)
                kaug = jnp.concatenate([k_ref[0, pl.ds(start, rows), hl], pf_ref[pl.ds(start, rows), :]], axis=1)
                st = jnp.dot(kaug, qst_ref[h], preferred_element_type=F32)
                if masked:
                    off = ((kj0 + i) * tk - qi * tq).astype(F32)
                    st = jnp.where(rel_ref[0:rows, :] + off <= 0.0, st, NEG_INF)
                sts.append(st)
            return sts

        all_sts = [scores(h) for h in range(n_hp)]
        if between is not None:
            between()
        for h in range(n_hp):
            hl = slice(h * LANES, (h + 1) * LANES)
            sts = all_sts[h]
            m_old = m_ref[h]
            m_new = m_old
            for st in sts:
                m_new = jnp.maximum(m_new, jnp.max(st, axis=0, keepdims=True))
            acc = acc_ref[h] * jnp.exp2(m_old - m_new)
            for i, st in enumerate(sts):
                pt = jnp.exp2(st - m_new).astype(BF16)
                vaug = jnp.concatenate([vt_ref[0, kj0 + i, hl, 0:rows], jnp.ones((EXTRA_ROWS, rows), BF16)], axis=0)
                acc = acc + jnp.dot(vaug, pt, preferred_element_type=F32)
            acc_ref[h] = acc
            m_ref[h] = m_new

    n_full = (qi * tq) // tk

    def chunk_with_sample_sub(k, kj, masked):
        sample_begin(k)
        chunks(kj, 1, masked, between=lambda: sample_sub(k))
        sample_end(k)

    def body(i, carry):
        chunk_with_sample_sub(sub_base + i, i, False)
        return carry

    lax.fori_loop(0, n_full, body, 0)

    assert tk == 2 * tq
    tile_at_chunk_start = (qi * tq) % tk == 0

    @pl.when(tile_at_chunk_start)
    def _():
        chunks(n_full, 1, True, rows=tq)

    @pl.when(jnp.logical_not(tile_at_chunk_start))
    def _():
        chunk_with_sample_sub(sub_base + n_full, n_full, True)

    lam = _diff_lambda(lq1_ref[...], lk1_ref[...], lq2_ref[...], lk2_ref[...], lam_init)
    for h in range(n_hp):
        ot = acc_ref[h, 0:LANES, :] / acc_ref[h, LANES:LANES + 1, :]
        at = jnp.where(sub < D_V, ot[:, 0:tq] - lam * ot[:, tq:2 * tq], ot[:, 2 * tq:3 * tq] - lam * ot[:, 3 * tq:])
        a2 = at * at
        ss0 = jnp.sum(a2[0:D_V], axis=0, keepdims=True)
        ss1 = jnp.sum(a2[D_V:], axis=0, keepdims=True)
        ms = jnp.where(sub < D_V, ss0, ss1) * (1.0 / D_V)
        an = at * lax.rsqrt(ms + EPS) * g_ref[...] * (1.0 - lam_init)
        o_ref[0, :, h * LANES:(h + 1) * LANES] = an.T.astype(BF16)


def _attn(qtb, kb, vtb, page_table, qsb, ks, vs, cache_k, cache_v, lam_params, g_subln, lam_init,
          *, tq=256, pps=16):
    b, w, s = qtb.shape
    n_chunks, tk = vtb.shape[1], vtb.shape[3]
    n_hp = w // LANES
    hw = n_hp * LANES
    cols = 4 * tq
    bd, t_new, _ = qsb.shape
    n_pages = page_table.shape[1]
    n_pool = cache_k.shape[0]
    past = n_pages * PAGE_SIZE
    n_subs = bd * (n_pages // pps)
    nq = s // tq
    subs_per_batch = (nq // 2) ** 2
    assert tk == 2 * tq and nq % 2 == 0 and n_pages % pps == 0 and b * subs_per_batch == n_subs
    ck = jnp.transpose(cache_k, (0, 2, 3, 1)).reshape(n_pool, w, PAGE_SIZE)
    cv = jnp.transpose(cache_v, (0, 2, 3, 1)).reshape(n_pool, w, PAGE_SIZE)
    g8 = jnp.tile(g_subln, N_ATT_HEADS).reshape(1, w)
    srows = 2 * N_ATT_HEADS * t_new
    g2 = jnp.concatenate([g_subln, g_subln]).reshape(LANES, 1)
    pos = lax.broadcasted_iota(jnp.int32, (s, LANES), 0)
    lane = lax.broadcasted_iota(jnp.int32, (s, LANES), 1)
    n_parts = len(L2E_PARTS)
    pf = jnp.where(lane < n_parts, pos // POS_SPLIT,
                   jnp.where(lane < N_POS_FEATS, pos % POS_SPLIT, 0)).astype(BF16)
    rel = (jnp.arange(tk, dtype=jnp.int32)[:, None] - (jnp.arange(cols, dtype=jnp.int32) % tq)[None, :]).astype(F32)
    const2 = lambda bi, hp, qi, pt: (0, 0)
    const3 = lambda bi, hp, qi, pt: (0, 0, 0)
    lam_specs = [pl.BlockSpec((1, D_QK), const2)] * 4
    grid_spec = pltpu.PrefetchScalarGridSpec(
        num_scalar_prefetch=1,
        grid=(b, w // hw, s // tq),
        in_specs=[pl.BlockSpec((1, hw, tq), lambda bi, hp, qi, pt: (bi, hp, qi)),
                  pl.BlockSpec((1, s, hw), lambda bi, hp, qi, pt: (bi, 0, hp)),
                  pl.BlockSpec((1, n_chunks, hw, tk), lambda bi, hp, qi, pt: (bi, 0, hp, 0)),
                  pl.BlockSpec((s, LANES), const2),
                  pl.BlockSpec((tk, cols), const2),
                  *lam_specs,
                  pl.BlockSpec((LANES, 1), const2),
                  pl.BlockSpec((bd, t_new, w), const3),
                  pl.BlockSpec((bd, t_new, w), const3),
                  pl.BlockSpec((bd, t_new, w), const3),
                  pl.BlockSpec((1, w), const2),
                  pl.BlockSpec(memory_space=pl.ANY),
                  pl.BlockSpec(memory_space=pl.ANY)],
        out_specs=[pl.BlockSpec((1, tq, hw), lambda bi, hp, qi, pt: (bi, qi, hp)),
                   pl.BlockSpec((bd, t_new, w), const3)],
        scratch_shapes=[pltpu.VMEM((n_hp, 2 * LANES, cols), BF16),
                        pltpu.VMEM((n_hp, 1, cols), F32),
                        pltpu.VMEM((n_hp, LANES + EXTRA_ROWS, cols), F32),
                        pltpu.VMEM((srows, w), BF16),
                        pltpu.VMEM((srows, 1), F32),
                        pltpu.VMEM((srows, 1), F32),
                        pltpu.VMEM((w, srows), F32),
                        pltpu.VMEM((PAGE_SLOTS, pps, w, PAGE_SIZE), F32),
                        pltpu.VMEM((PAGE_SLOTS, pps, w, PAGE_SIZE), F32),
                        pltpu.SemaphoreType.DMA((2, PAGE_SLOTS))],
    )
    return pl.pallas_call(
        functools.partial(_attn_kernel, tq=tq, tk=tk, n_hp=n_hp, lam_init=lam_init,
                          pps=pps, subs_per_batch=subs_per_batch, n_subs=n_subs, past=past),
        grid_spec=grid_spec,
        out_shape=[jax.ShapeDtypeStruct((b, s, w), BF16), jax.ShapeDtypeStruct((bd, t_new, w), BF16)],
        compiler_params=_cparams("arbitrary", "arbitrary", "arbitrary"),
        name="attn",
    )(page_table, qtb, kb, vtb, pf, rel, *[p.reshape(1, D_QK) for p in lam_params], g2,
      qsb, ks, vs, g8, ck, cv)


def _outproj_kernel(a_ref, u_ref, vs_ref, x_ref, ga1_ref, sc2_ref, sh2_ref, wcat_ref, sbias_ref, wout_ref,
                    gpost_ref, gpre_ref, wr_cat_ref, br_ref, tril_ref, *rest, group_dispatch):
    if group_dispatch:
        x1_ref, pay_ref, info_ref, counts_ref, sg_ref, run_ref = rest
    else:
        x1_ref, h2_ref, gates_ref, sg_ref = rest
    tm = x_ref.shape[0]
    sgu_w = u_ref.shape[1]
    lane = lax.broadcasted_iota(jnp.int32, (CHUNK, LANES), 1)
    for ci in range(tm // CHUNK):
        rs = slice(ci * CHUNK, (ci + 1) * CHUNK)
        for gp in range(sgu_w // LANES):
            cs = slice(gp * LANES, (gp + 1) * LANES)
            vp = vs_ref[rs, cs]
            zero = jnp.zeros_like(vp)
            rhs = jnp.concatenate([jnp.where(lane < SGU_GC, vp, zero), jnp.where(lane >= SGU_GC, vp, zero)], axis=0)
            mixed = jnp.dot(wcat_ref[gp], rhs, preferred_element_type=F32) + sbias_ref[:, cs]
            sg_ref[rs, cs] = (u_ref[rs, cs].astype(F32) * mixed).astype(BF16)
    att_w = a_ref.shape[1]
    mix = (jnp.dot(a_ref[...], wout_ref[0:att_w, :], preferred_element_type=F32)
           + jnp.dot(sg_ref[...], wout_ref[att_w:, :], preferred_element_type=F32))
    x1 = x_ref[...] + ga1_ref[0] * _rms(mix, gpost_ref[...])
    x1_ref[...] = x1
    h2 = _rms(x1, gpre_ref[...]) * (1.0 + sc2_ref[0]) + sh2_ref[0]
    h2_hi = h2.astype(BF16)
    h2_lo = (h2 - h2_hi.astype(F32)).astype(BF16)
    hh = jnp.dot(h2_hi, wr_cat_ref[...], preferred_element_type=F32)
    logits = (hh[:, :LANES] + hh[:, LANES:]
              + jnp.dot(h2_lo, wr_cat_ref[:, :LANES], preferred_element_type=F32)) + br_ref[...]
    ln = lax.broadcasted_iota(jnp.int32, (tm, LANES), 1)
    big = jnp.int32(LANES)
    is_g = (ln >= N_EXPERTS) & (ln < N_EXPERTS + N_EXPERT_GROUPS)
    gl = jnp.where(is_g, logits, NEG_INF)
    gmax = jnp.max(gl, axis=-1, keepdims=True)
    g_idx = jnp.min(jnp.where(gl == gmax, ln, big), axis=-1, keepdims=True) - N_EXPERTS
    g_w = 1.0 / jnp.sum(jnp.exp(gl - gmax), axis=-1, keepdims=True)
    in_grp = (ln >= g_idx * EXPERTS_PER_GROUP) & (ln < (g_idx + 1) * EXPERTS_PER_GROUP)
    el = jnp.where(in_grp, logits, NEG_INF)
    emax = jnp.max(el, axis=-1, keepdims=True)
    ez = jnp.exp(el - emax)
    prob = ez / jnp.sum(ez, axis=-1, keepdims=True)
    p1 = jnp.max(prob, axis=-1, keepdims=True)
    i1 = jnp.min(jnp.where((prob == p1) & in_grp, ln, big), axis=-1, keepdims=True)
    rest = jnp.where(in_grp & (ln != i1), prob, -1.0)
    p2 = jnp.max(rest, axis=-1, keepdims=True)
    i2 = jnp.min(jnp.where(rest == p2, ln, big), axis=-1, keepdims=True)
    den = p1 + p2
    gates = jnp.where(ln == i1, p1 / den * g_w, 0.0) + jnp.where(ln == i2, p2 / den * g_w, 0.0)
    if not group_dispatch:
        h2_ref[...] = h2_hi
        gates_ref[...] = gates[:, :N_EXPERTS]
        return

    bits = pltpu.bitcast(h2_hi.astype(F32), jnp.uint32)
    half = PAYLOAD_H2_SEGS * LANES
    for s in range(PAYLOAD_H2_SEGS):
        lo = bits[:, s * LANES:(s + 1) * LANES]
        hi = bits[:, half + s * LANES:half + (s + 1) * LANES]
        pay_ref[s] = hi | lax.shift_right_logical(lo, jnp.uint32(16))
    pay_ref[PAYLOAD_H2_SEGS] = pltpu.bitcast(gates, jnp.uint32)

    @pl.when(pl.program_id(0) == 0)
    def _():
        run_ref[...] = jnp.zeros_like(run_ref)

    onehot = jnp.where(ln == g_idx, 1.0, 0.0).astype(BF16)
    csum = jnp.dot(tril_ref[...], onehot, preferred_element_type=F32) + run_ref[...]
    rank = jnp.sum(jnp.where(ln == g_idx, csum, 0.0), axis=-1, keepdims=True) - 1.0
    run_ref[...] = csum[tm - 1:tm, :]
    counts_ref[...] = csum[tm - 1:tm, :]
    info = jnp.where(ln == 0, g_idx, jnp.where(ln == 1, rank.astype(jnp.int32), 0))
    info_ref[...] = info[:, :INFO_W]


def _outproj(a, u, vsb, x, ga1, sc2, sh2, wcat, sbias, w_out_bf, g_post, g_pre, wr_cat, br,
             *, tm, tiles_per_mod, group_dispatch):
    n, d = x.shape
    att_w = a.shape[1]
    sgu_w = u.shape[1]
    r = ga1.shape[1]
    row = lambda i: (i, 0)
    modmap = lambda i: (i // tiles_per_mod, 0, 0)
    const = lambda i: (0, 0)
    tril = jnp.tril(jnp.ones((tm, tm), BF16))
    if group_dispatch:
        out_specs = [pl.BlockSpec((tm, d), row),
                     pl.BlockSpec((PAYLOAD_SEGS, tm, LANES), lambda i: (0, i, 0)),
                     pl.BlockSpec((tm, INFO_W), row),
                     pl.BlockSpec((1, LANES), const)]
        out_shape = [jax.ShapeDtypeStruct((n, d), F32),
                     jax.ShapeDtypeStruct((PAYLOAD_SEGS, n, LANES), jnp.uint32),
                     jax.ShapeDtypeStruct((n, INFO_W), jnp.int32),
                     jax.ShapeDtypeStruct((1, LANES), F32)]
        scratch = [pltpu.VMEM((tm, sgu_w), BF16), pltpu.VMEM((1, LANES), F32)]
    else:
        out_specs = [pl.BlockSpec((tm, d), row),
                     pl.BlockSpec((tm, d), row),
                     pl.BlockSpec((tm, N_EXPERTS), row)]
        out_shape = [jax.ShapeDtypeStruct((n, d), F32),
                     jax.ShapeDtypeStruct((n, d), BF16),
                     jax.ShapeDtypeStruct((n, N_EXPERTS), F32)]
        scratch = [pltpu.VMEM((tm, sgu_w), BF16)]
    return pl.pallas_call(
        functools.partial(_outproj_kernel, group_dispatch=group_dispatch),
        grid=(n // tm,),
        in_specs=[pl.BlockSpec((tm, att_w), row),
                  pl.BlockSpec((tm, sgu_w), row),
                  pl.BlockSpec((tm, sgu_w), row),
                  pl.BlockSpec((tm, d), row),
                  pl.BlockSpec((1, r, d), modmap),
                  pl.BlockSpec((1, r, d), modmap),
                  pl.BlockSpec((1, r, d), modmap),
                  pl.BlockSpec(wcat.shape, lambda i: (0, 0, 0)),
                  pl.BlockSpec(sbias.shape, const),
                  pl.BlockSpec(w_out_bf.shape, const),
                  pl.BlockSpec((1, d), const),
                  pl.BlockSpec((1, d), const),
                  pl.BlockSpec(wr_cat.shape, const),
                  pl.BlockSpec((1, LANES), const),
                  pl.BlockSpec((tm, tm), const)],
        out_specs=out_specs,
        out_shape=out_shape,
        scratch_shapes=scratch,
        compiler_params=_cparams("arbitrary"),
        name="outproj",
    )(a, u, vsb, x, ga1, sc2, sh2, wcat, sbias, w_out_bf, g_post.reshape(1, d), g_pre.reshape(1, d),
      wr_cat, br, tril)


def _moe_kernel(h_ref, gates_ref, x1_ref, ga2_ref, gpost_ref, wg_ref, wu_ref, wd_ref, o_ref, acc_ref):
    e = pl.program_id(1)

    @pl.when(e == 0)
    def _():
        acc_ref[...] = jnp.zeros_like(acc_ref)

    h = h_ref[...]
    ln = lax.broadcasted_iota(jnp.int32, gates_ref.shape, 1)
    per_step = wg_ref.shape[0]
    y = jnp.zeros(acc_ref.shape, F32)
    for el in range(per_step):
        gate = jnp.dot(h, wg_ref[el], preferred_element_type=F32)
        up = jnp.dot(h, wu_ref[el], preferred_element_type=F32)
        hdn = (gate * jax.nn.sigmoid(gate) * up).astype(BF16)
        ge = jnp.sum(jnp.where(ln == e * per_step + el, gates_ref[...], 0.0), axis=-1, keepdims=True)
        y = y + ge * jnp.dot(hdn, wd_ref[el], preferred_element_type=F32)
    acc_ref[...] += y

    @pl.when(e == pl.num_programs(1) - 1)
    def _():
        o_ref[...] = x1_ref[...] + ga2_ref[0] * _rms(acc_ref[...], gpost_ref[...])


def _moe(h2, gates, x1, ga2, g_post, wg_bf, wu_bf, wd_bf, *, tm, tiles_per_mod):
    n, d = x1.shape
    n_e, _, de = wg_bf.shape
    r = ga2.shape[1]
    row = lambda i, e: (i, 0)
    return pl.pallas_call(
        _moe_kernel,
        grid=(n // tm, n_e // EXPERTS_PER_GROUP),
        in_specs=[pl.BlockSpec((tm, d), row),
                  pl.BlockSpec((tm, n_e), row),
                  pl.BlockSpec((tm, d), row),
                  pl.BlockSpec((1, r, d), lambda i, e: (i // tiles_per_mod, 0, 0)),
                  pl.BlockSpec((1, d), lambda i, e: (0, 0)),
                  pl.BlockSpec((EXPERTS_PER_GROUP, d, de), lambda i, e: (e, 0, 0)),
                  pl.BlockSpec((EXPERTS_PER_GROUP, d, de), lambda i, e: (e, 0, 0)),
                  pl.BlockSpec((EXPERTS_PER_GROUP, de, d), lambda i, e: (e, 0, 0))],
        out_specs=pl.BlockSpec((tm, d), row),
        out_shape=jax.ShapeDtypeStruct((n, d), F32),
        scratch_shapes=[pltpu.VMEM((tm, d), F32)],
        compiler_params=_cparams("arbitrary", "arbitrary"),
        name="moe",
    )(h2, gates, x1, ga2, g_post.reshape(1, d), wg_bf, wu_bf, wd_bf)


def _sc_mesh():
    return plsc.VectorSubcoreMesh(core_axis_name="core", subcore_axis_name="subcore")


def _sc_scatter_rows(x, idx, n_out_rows):
    n_rows, width = x.shape

    @pl.kernel(out_type=jax.ShapeDtypeStruct((n_out_rows, width), x.dtype), mesh=_sc_mesh(), scratch_types=[])
    def scatter(x_hbm, i_hbm, o_hbm):
        def body(x_vmem, i_vmem):
            pltpu.sync_copy(x_vmem, o_hbm.at[i_vmem.at[0]])

        pltpu.emit_pipeline(
            body,
            grid=(n_rows // SC_WINDOW,),
            in_specs=[pl.BlockSpec((SC_WINDOW, width), lambda i: (i, 0)),
                      pl.BlockSpec((1, SC_WINDOW), lambda i: (0, i))],
            out_specs=[],
            core_axis_name=("core", "subcore"),
            dimension_semantics=(pltpu.PARALLEL,),
        )(x_hbm, i_hbm)

    return scatter(x, idx.reshape(1, n_rows))


def _sc_gather_rows(x, idx):
    n_rows = idx.shape[0]
    width = x.shape[1]

    @pl.kernel(out_type=jax.ShapeDtypeStruct((n_rows, width), x.dtype), mesh=_sc_mesh(), scratch_types=[])
    def gather(x_hbm, i_hbm, o_hbm):
        def body(i_vmem, o_vmem):
            pltpu.sync_copy(x_hbm.at[i_vmem.at[0]], o_vmem)

        pltpu.emit_pipeline(
            body,
            grid=(n_rows // SC_WINDOW,),
            in_specs=[pl.BlockSpec((1, SC_WINDOW), lambda i: (0, i))],
            out_specs=[pl.BlockSpec((SC_WINDOW, width), lambda i: (i, 0))],
            core_axis_name=("core", "subcore"),
            dimension_semantics=(pltpu.PARALLEL,),
        )(i_hbm, o_hbm)

    return gather(x, idx.reshape(1, n_rows))


def _group_moe_kernel(tg_ref, nused_ref, xs_ref, gpost_ref, wg_ref, wu_ref, wd_ref, rs_ref):
    t = pl.program_id(0)
    tm = xs_ref.shape[1]

    @pl.when(t >= nused_ref[0])
    def _():
        rs_ref[...] = jnp.zeros_like(rs_ref)

    @pl.when(t < nused_ref[0])
    def _():
        words = [xs_ref[s] for s in range(PAYLOAD_H2_SEGS)]
        lo = [pltpu.bitcast(lax.shift_left(w, jnp.uint32(16)), F32) for w in words]
        hi = [pltpu.bitcast(w & jnp.uint32(0xFFFF0000), F32) for w in words]
        x = jnp.concatenate(lo + hi, axis=1).astype(BF16)
        gates = pltpu.bitcast(xs_ref[PAYLOAD_H2_SEGS], F32)
        ln = lax.broadcasted_iota(jnp.int32, (tm, LANES), 1)
        first = tg_ref[t] * EXPERTS_PER_GROUP
        y = jnp.zeros((tm, x.shape[1]), F32)
        for el in range(EXPERTS_PER_GROUP):
            ge = jnp.sum(jnp.where(ln == first + el, gates, 0.0), axis=-1, keepdims=True)
            gate = jnp.dot(x, wg_ref[el], preferred_element_type=F32)
            up = jnp.dot(x, wu_ref[el], preferred_element_type=F32)
            hdn = (gate * jax.nn.sigmoid(gate) * up).astype(BF16)
            y = y + ge * jnp.dot(hdn, wd_ref[el], preferred_element_type=F32)
        r = _rms(y, gpost_ref[...])
        for s in range(rs_ref.shape[0]):
            rs_ref[s] = r[:, s * LANES:(s + 1) * LANES]


def _group_moe(xs, tile_group, n_used, g_post, wg_bf, wu_bf, wd_bf, *, tm):
    _, p_rows, _ = xs.shape
    n_e, d, de = wg_bf.shape
    g = EXPERTS_PER_GROUP
    grid_spec = pltpu.PrefetchScalarGridSpec(
        num_scalar_prefetch=2,
        grid=(p_rows // tm,),
        in_specs=[pl.BlockSpec((PAYLOAD_SEGS, tm, LANES), lambda t, tg, nu: (0, t, 0)),
                  pl.BlockSpec((1, d), lambda t, tg, nu: (0, 0)),
                  pl.BlockSpec((g, d, de), lambda t, tg, nu: (tg[t], 0, 0)),
                  pl.BlockSpec((g, d, de), lambda t, tg, nu: (tg[t], 0, 0)),
                  pl.BlockSpec((g, de, d), lambda t, tg, nu: (tg[t], 0, 0))],
        out_specs=pl.BlockSpec((d // LANES, tm, LANES), lambda t, tg, nu: (0, t, 0)),
    )
    return pl.pallas_call(
        _group_moe_kernel,
        grid_spec=grid_spec,
        out_shape=jax.ShapeDtypeStruct((d // LANES, p_rows, LANES), F32),
        compiler_params=_cparams("arbitrary"),
        name="group_moe",
    )(tile_group, n_used, xs, g_post.reshape(1, d), wg_bf, wu_bf, wd_bf)


def _residual_kernel(x1_ref, ga2_ref, r_ref, o_ref):
    r = jnp.concatenate([r_ref[s] for s in range(r_ref.shape[0])], axis=1)
    o_ref[...] = x1_ref[...] + ga2_ref[0] * r


def _residual(x1, ga2, r_segs, *, tm, tiles_per_mod):
    n, d = x1.shape
    return pl.pallas_call(
        _residual_kernel,
        grid=(n // tm,),
        in_specs=[pl.BlockSpec((tm, d), lambda i: (i, 0)),
                  pl.BlockSpec((1, ga2.shape[1], d), lambda i: (i // tiles_per_mod, 0, 0)),
                  pl.BlockSpec((d // LANES, tm, LANES), lambda i: (0, i, 0))],
        out_specs=pl.BlockSpec((tm, d), lambda i: (i, 0)),
        out_shape=jax.ShapeDtypeStruct((n, d), F32),
        compiler_params=_cparams("arbitrary"),
        name="residual",
    )(x1, ga2, r_segs)


def _group_plan(info, counts, n_tokens, tm):
    n_tiles = n_tokens // tm + N_EXPERT_GROUPS
    cnt = counts[0, :N_EXPERT_GROUPS].astype(jnp.int32)
    padded = (cnt + tm - 1) // tm * tm
    ends = jnp.cumsum(padded)
    starts = ends - padded
    gid, rank = info[:, 0], info[:, 1]
    onehot = (gid[:, None] == jnp.arange(N_EXPERT_GROUPS, dtype=jnp.int32)[None, :]).astype(jnp.int32)
    pos = jnp.sum(onehot * starts[None, :], axis=1) + rank
    tile_ends = ends // tm
    t = jnp.arange(n_tiles, dtype=jnp.int32)
    tile_group = jnp.minimum(jnp.sum((t[:, None] >= tile_ends[None, :]).astype(jnp.int32), axis=1),
                             N_EXPERT_GROUPS - 1)
    return pos, tile_group, tile_ends[-1:], n_tiles * tm


def _sgu_weights(ws, bs, chunk_len):
    causal = jnp.tril(jnp.ones((chunk_len, chunk_len), ws.dtype))
    wm = ws[:, :chunk_len, :chunk_len] * causal
    reps = CHUNK // chunk_len
    if reps > 1:
        eye = jnp.eye(reps, dtype=ws.dtype)
        wm = jnp.einsum("ab,gts->gatbs", eye, wm).reshape(N_SGU_GROUPS, CHUNK, CHUNK)
    wcat = jnp.concatenate([wm[0::2], wm[1::2]], axis=2).astype(BF16)
    bt = jnp.tile(bs[:, :chunk_len].T, (reps, 1))
    sbias = jnp.repeat(bt, SGU_GC, axis=1)
    return wcat, sbias


def _split_mod(mod):
    return [m[:, None, :] for m in jnp.split(mod, 6, axis=-1)]


def kernel(x_prompt, x_sample, cache_k, cache_v, page_table, c_prompt, c_sample, w_ada, b_ada, g_pre_mix, g_post_mix, g_pre_ffn, g_post_ffn, w_in, lam_q1, lam_k1, lam_q2, lam_k2, g_subln, sgu_ln_g, sgu_ln_b, sgu_ws, sgu_bs, w_out, w_rg, b_rg, w_re, b_re, w_gate, w_up, w_down):
    depth = w_in.shape[0]
    assert depth == 1
    l = 0
    lam_init = 0.8 - 0.6 * math.exp(-0.3 * l)
    bp, sp, d = x_prompt.shape
    bs_, ts, _ = x_sample.shape
    n_s = bs_ * ts

    mod = _ada(jnp.concatenate([c_prompt, c_sample], axis=0), w_ada[l], b_ada[l])
    sh1p, sc1p, ga1p, sh2p, sc2p, ga2p = _split_mod(mod[:bp])
    rep = lambda m: jnp.repeat(m, ts, axis=0).reshape(1, n_s, d)
    sh1s, sc1s, ga1s, sh2s, sc2s, ga2s = [rep(m) for m in jnp.split(mod[bp:], 6, axis=-1)]

    w_in_bf = w_in[l].astype(BF16)
    wqv_t_bf = jnp.concatenate([w_in_bf[:, 0:ATT_W], w_in_bf[:, 2 * ATT_W:3 * ATT_W]], axis=1).T
    w_out_bf = w_out[l].astype(BF16)
    wg_bf = w_gate[l].astype(BF16)
    wu_bf = w_up[l].astype(BF16)
    wd_bf = w_down[l].astype(BF16)
    wr = jnp.concatenate([w_re[l], w_rg[l]], axis=1)
    wr = jnp.pad(wr, ((0, 0), (0, LANES - wr.shape[1])))
    wr_hi = wr.astype(BF16)
    wr_cat = jnp.concatenate([wr_hi, (wr - wr_hi.astype(F32)).astype(BF16)], axis=1)
    br = jnp.pad(jnp.concatenate([b_re[l], b_rg[l]]), (0, LANES - N_EXPERTS - N_EXPERT_GROUPS)).reshape(1, LANES)
    lam_params = (lam_q1[l], lam_k1[l], lam_q2[l], lam_k2[l])

    tm_p = 512
    xp = x_prompt.reshape(bp * sp, d)
    ktp, vtp, qtb, kb, vtb, u, vsb, vs_last = _inproj(
        xp, sc1p, sh1p, g_pre_mix[l], w_in_bf, wqv_t_bf, sgu_ln_g[l], sgu_ln_b[l],
        tm=tm_p, tiles_per_seq=sp // tm_p, last_rows=CHUNK, transposed_qv=True)
    w3 = ATT_W
    xs = x_sample.reshape(n_s, d)
    ks, vs_, qbs, us, vsbs, vs_last_s = _inproj(
        xs, sc1s, sh1s, g_pre_mix[l], w_in_bf, wqv_t_bf, sgu_ln_g[l], sgu_ln_b[l],
        tm=n_s, tiles_per_seq=1, last_rows=n_s, transposed_qv=False)
    a, a_s = _attn(qtb, kb.reshape(bp, sp, w3), vtb, page_table, qbs.reshape(bs_, ts, w3),
                   ks.reshape(bs_, ts, w3), vs_.reshape(bs_, ts, w3), cache_k[l], cache_v[l],
                   lam_params, g_subln[l], lam_init)
    kp = jnp.transpose(ktp.reshape(bp, N_ATT_HEADS, 2 * D_QK, sp), (0, 3, 1, 2))
    vp = jnp.transpose(vtp.reshape(bp, N_ATT_HEADS, D_V, sp), (0, 3, 1, 2))
    wcat_p, sbias_p = _sgu_weights(sgu_ws[l], sgu_bs[l], CHUNK)
    n_p = bp * sp
    x1, payload, info, counts = _outproj(
        a.reshape(n_p, w3), u, vsb, xp, ga1p, sc2p, sh2p, wcat_p, sbias_p, w_out_bf,
        g_post_mix[l], g_pre_ffn[l], wr_cat, br, tm=tm_p, tiles_per_mod=sp // tm_p, group_dispatch=True)
    tm_moe = 512
    pos, tile_group, n_used, p_rows = _group_plan(info, counts, n_p, tm_moe)
    seg_p = lambda k: (jnp.arange(k, dtype=jnp.int32)[:, None] * p_rows + pos[None, :]).reshape(-1)
    sorted_rows = _sc_scatter_rows(payload.reshape(PAYLOAD_SEGS * n_p, LANES), seg_p(PAYLOAD_SEGS),
                                   PAYLOAD_SEGS * p_rows)
    rs = _group_moe(sorted_rows.reshape(PAYLOAD_SEGS, p_rows, LANES), tile_group, n_used, g_post_ffn[l],
                    wg_bf, wu_bf, wd_bf, tm=tm_moe)
    n_seg = d // LANES
    r_tok = _sc_gather_rows(rs.reshape(n_seg * p_rows, LANES), seg_p(n_seg))
    yp = _residual(x1, ga2p, r_tok.reshape(n_seg, n_p, LANES), tm=tm_p, tiles_per_mod=sp // tm_p)

    wcat_s, sbias_s = _sgu_weights(sgu_ws[l], sgu_bs[l], ts)
    x1s, h2s, gates_s = _outproj(a_s.reshape(n_s, w3), us, vsbs, xs, ga1s, sc2s, sh2s, wcat_s, sbias_s, w_out_bf,
                                 g_post_mix[l], g_pre_ffn[l], wr_cat, br, tm=n_s, tiles_per_mod=1,
                                 group_dispatch=False)
    ys = _moe(h2s, gates_s, x1s, ga2s, g_post_ffn[l], wg_bf, wu_bf, wd_bf, tm=n_s, tiles_per_mod=1)

    return (yp.reshape(bp, sp, d), ys.reshape(bs_, ts, d),
            kp[None], vp[None],
            vs_last.reshape(1, bp, CHUNK, -1),
            ks.reshape(1, bs_, ts, N_ATT_HEADS, 2 * D_QK), vs_.reshape(1, bs_, ts, N_ATT_HEADS, D_V),
            vs_last_s.reshape(1, bs_, ts, -1))
```

```python
import functools
import math
import struct

import jax
import jax.numpy as jnp
from jax import lax
from jax.experimental import pallas as pl
from jax.experimental.pallas import tpu as pltpu
from jax.experimental.pallas import tpu_sc as plsc

F32 = jnp.float32
BF16 = jnp.bfloat16

EPS = 1e-6
N_ATT_HEADS = 8
D_QK = 32
D_V = 64
ATT_W = N_ATT_HEADS * D_V
N_SGU_GROUPS = 8
SGU_GC = 64
SGU_W = N_SGU_GROUPS * SGU_GC
CHUNK = 128
N_EXPERT_GROUPS = 4
EXPERTS_PER_GROUP = 4
N_EXPERTS = 16
PAGE_SIZE = 128
LANES = 128
VMEM_LIMIT = 56 * 1024 * 1024
NEG_INF = float("-inf")
LOG2E = math.log2(math.e)
PAYLOAD_H2_SEGS = 4
PAYLOAD_SEGS = PAYLOAD_H2_SEGS + 1
INFO_W = 8
SC_WINDOW = 128
PAGE_SLOTS = 2


def _cparams(*sem):
    return pltpu.CompilerParams(dimension_semantics=sem, vmem_limit_bytes=VMEM_LIMIT)


def _rms(x, g):
    return x * lax.rsqrt(jnp.mean(x * x, axis=-1, keepdims=True) + EPS) * g


def _ada_kernel(c_ref, w_ref, b_ref, o_ref):
    c = c_ref[...]
    s = c * jax.nn.sigmoid(c)
    o_ref[...] = jnp.dot(s.astype(BF16), w_ref[...].astype(BF16), preferred_element_type=F32) + b_ref[...]


def _ada(c, w_ada, b_ada):
    n, d = c.shape
    nout = w_ada.shape[1]
    tn = d
    return pl.pallas_call(
        _ada_kernel,
        grid=(nout // tn,),
        in_specs=[pl.BlockSpec((n, d), lambda j: (0, 0)),
                  pl.BlockSpec((d, tn), lambda j: (0, j)),
                  pl.BlockSpec((1, tn), lambda j: (0, j))],
        out_specs=pl.BlockSpec((n, tn), lambda j: (0, j)),
        out_shape=jax.ShapeDtypeStruct((n, nout), F32),
        compiler_params=_cparams("arbitrary"),
        name="ada",
    )(c, w_ada, b_ada.reshape(1, nout))


Q_SCALE = D_QK ** -0.5 * LOG2E


def _inproj_kernel(x_ref, sc_ref, sh_ref, g_ref, w_ref, wqv_t_ref, lng_ref, lnb_ref, *out_refs,
                   last_rows, transposed_qv):
    x = x_ref[...]
    h = (_rms(x, g_ref[...]) * (1.0 + sc_ref[0]) + sh_ref[0]).astype(BF16)

    def proj(lo):
        return jnp.dot(h, w_ref[:, lo:lo + ATT_W], preferred_element_type=F32)

    def proj_t(lo):
        return lax.dot_general(wqv_t_ref[lo:lo + ATT_W, :], h, (((1,), (1,)), ((), ())),
                               preferred_element_type=F32)

    zk = proj(ATT_W)
    if transposed_qv:
        kt_ref, vt_ref, qtb_ref, kb_ref, vtb_ref, u_ref, vsb_ref, vsl_ref = out_refs
        kt_ref[0] = zk.T
        kb_ref[...] = zk.astype(BF16)
        qtb_ref[0] = (proj_t(0) * Q_SCALE).astype(BF16)
        zvt = proj_t(ATT_W)
        vt_ref[0] = zvt
        vtb_ref[0, 0] = zvt.astype(BF16)
    else:
        k_ref, v_ref, qb_ref, u_ref, vsb_ref, vsl_ref = out_refs
        k_ref[...] = zk
        qb_ref[...] = (proj(0) * Q_SCALE).astype(BF16)
        v_ref[...] = proj(2 * ATT_W)
    u_ref[...] = jax.nn.gelu(proj(3 * ATT_W)).astype(BF16)
    gs = jax.nn.gelu(proj(3 * ATT_W + SGU_W))
    mu = jnp.mean(gs, axis=-1, keepdims=True)
    xc = gs - mu
    vs = xc * lax.rsqrt(jnp.mean(xc * xc, axis=-1, keepdims=True) + EPS) * lng_ref[...] + lnb_ref[...]
    vsb_ref[...] = vs.astype(BF16)
    tm = vs.shape[0]
    vsl_ref[0] = vs[tm - last_rows:, :]


def _inproj(x, sc, sh, g, w_in_bf, wqv_t_bf, ln_g, ln_b, *, tm, tiles_per_seq, last_rows, transposed_qv):
    n, d = x.shape
    in_w = w_in_bf.shape[1]
    r = sc.shape[1]
    n_tiles = n // tm
    n_seq = n_tiles // tiles_per_seq
    s_len = tiles_per_seq * tm
    row = lambda i: (i, 0)
    modmap = lambda i: (i // tiles_per_seq, 0, 0)
    const = lambda i: (0, 0)
    col_t = lambda i: (i // tiles_per_seq, 0, i % tiles_per_seq)
    row_spec = pl.BlockSpec((tm, ATT_W), row)
    tail = [(pl.BlockSpec((tm, SGU_W), row), jax.ShapeDtypeStruct((n, SGU_W), BF16)),
            (pl.BlockSpec((tm, SGU_W), row), jax.ShapeDtypeStruct((n, SGU_W), BF16)),
            (pl.BlockSpec((1, last_rows, SGU_W), lambda i: (i // tiles_per_seq, 0, 0)),
             jax.ShapeDtypeStruct((n_seq, last_rows, SGU_W), F32))]
    if transposed_qv:
        head = [(pl.BlockSpec((1, ATT_W, tm), col_t), jax.ShapeDtypeStruct((n_seq, ATT_W, s_len), F32)),
                (pl.BlockSpec((1, ATT_W, tm), col_t), jax.ShapeDtypeStruct((n_seq, ATT_W, s_len), F32)),
                (pl.BlockSpec((1, ATT_W, tm), col_t), jax.ShapeDtypeStruct((n_seq, ATT_W, s_len), BF16)),
                (row_spec, jax.ShapeDtypeStruct((n, ATT_W), BF16)),
                (pl.BlockSpec((1, 1, ATT_W, tm), lambda i: (i // tiles_per_seq, i % tiles_per_seq, 0, 0)),
                 jax.ShapeDtypeStruct((n_seq, tiles_per_seq, ATT_W, tm), BF16))]
    else:
        head = [(row_spec, jax.ShapeDtypeStruct((n, ATT_W), F32)),
                (row_spec, jax.ShapeDtypeStruct((n, ATT_W), F32)),
                (row_spec, jax.ShapeDtypeStruct((n, ATT_W), BF16))]
    specs, shapes = zip(*(head + tail))
    return pl.pallas_call(
        functools.partial(_inproj_kernel, last_rows=last_rows, transposed_qv=transposed_qv),
        grid=(n_tiles,),
        in_specs=[pl.BlockSpec((tm, d), row),
                  pl.BlockSpec((1, r, d), modmap),
                  pl.BlockSpec((1, r, d), modmap),
                  pl.BlockSpec((1, d), const),
                  pl.BlockSpec((d, in_w), const),
                  pl.BlockSpec(wqv_t_bf.shape, const),
                  pl.BlockSpec((1, SGU_W), const),
                  pl.BlockSpec((1, SGU_W), const)],
        out_specs=list(specs),
        out_shape=list(shapes),
        compiler_params=_cparams("arbitrary"),
        name="inproj",
    )(x, sc, sh, g.reshape(1, d), w_in_bf, wqv_t_bf, ln_g.reshape(1, SGU_W), ln_b.reshape(1, SGU_W))


def _diff_lambda(lq1, lk1, lq2, lk2, lam_init):
    return (jnp.exp(jnp.sum(lq1 * lk1, axis=-1, keepdims=True))
            - jnp.exp(jnp.sum(lq2 * lk2, axis=-1, keepdims=True)) + lam_init)


def _pair_slopes(hp):
    s0 = jnp.where(hp == 0, 2.0 ** -1, jnp.where(hp == 1, 2.0 ** -3, jnp.where(hp == 2, 2.0 ** -5, 2.0 ** -7)))
    return s0.astype(F32), (s0 * 0.5).astype(F32)


def _bf16_round(x):
    bits = struct.unpack("<I", struct.pack("<f", x))[0]
    bits = (bits + 0x7FFF + ((bits >> 16) & 1)) & 0xFFFF0000
    return struct.unpack("<f", struct.pack("<I", bits))[0]


L2E_PARTS = []
_rest = LOG2E
for _ in range(3):
    L2E_PARTS.append(_bf16_round(_rest))
    _rest -= L2E_PARTS[-1]
POS_SPLIT = 64
N_POS_FEATS = 2 * len(L2E_PARTS)
EXTRA_ROWS = 16


def _attn_kernel(pt_ref, qt_ref, k_ref, vt_ref, pf_ref, rel_ref, lq1_ref, lk1_ref, lq2_ref, lk2_ref, g_ref,
                 qs_ref, kn_ref, vn_ref, g8_ref, ck_hbm, cv_hbm, o_ref, os_ref,
                 qst_ref, m_ref, acc_ref, qbd_ref, ms_ref, ls_ref, accs_ref, kbuf_ref, vbuf_ref, sem_ref,
                 *, tq, tk, n_hp, lam_init, pps, subs_per_batch, n_subs, past):
    qi = pl.program_id(2)
    cols = 4 * tq

    t_new, w = qs_ref.shape[1], qs_ref.shape[2]
    step = pl.program_id(0) * pl.num_programs(2) + qi
    subs_per_seq = past // (pps * PAGE_SIZE)
    half_q = qi // 2
    sub_base = pl.program_id(0) * subs_per_batch + half_q * half_q + (qi % 2) * half_q
    n_hj = 2 * N_ATT_HEADS
    srows = n_hj * t_new
    blk = pps * PAGE_SIZE
    srow_id = lax.broadcasted_iota(jnp.int32, (srows, 1), 0)
    shead = srow_id // (2 * t_new)
    sslope = jnp.zeros((srows, 1), F32)
    for h in range(N_ATT_HEADS):
        sslope = jnp.where(shead == h, 2.0 ** -(h + 1), sslope)
    sslope = sslope * LOG2E
    sqpos = (past + srow_id % t_new).astype(F32)

    def page_copies(k, slot):
        kc = jnp.minimum(k, n_subs - 1)
        seq_i, sub_i = kc // subs_per_seq, kc % subs_per_seq
        copies = []
        for i in range(pps):
            page = pt_ref[seq_i, sub_i * pps + i]
            copies.append(pltpu.make_async_copy(ck_hbm.at[page], kbuf_ref.at[slot, i], sem_ref.at[0, slot]))
            copies.append(pltpu.make_async_copy(cv_hbm.at[page], vbuf_ref.at[slot, i], sem_ref.at[1, slot]))
        return copies

    eye = (lax.broadcasted_iota(jnp.int32, (srows, srows), 0) == lax.broadcasted_iota(jnp.int32, (srows, srows), 1))

    def sample_update(s, vmat, from_pages):
        m_old = ms_ref[...]
        m_new = jnp.maximum(m_old, jnp.max(s, axis=-1, keepdims=True))
        p = jnp.exp2(s - m_new)
        corr = jnp.exp2(m_old - m_new)
        ls_ref[...] = corr * ls_ref[...] + jnp.sum(p, axis=-1, keepdims=True)
        corr_row = jnp.sum(jnp.where(eye, corr, 0.0), axis=0, keepdims=True)
        if from_pages:
            pv_t = jnp.dot(vmat, p.T.astype(BF16), preferred_element_type=F32)
        else:
            pv_t = jnp.dot(p, vmat, preferred_element_type=F32).T
        accs_ref[...] = accs_ref[...] * corr_row + pv_t
        ms_ref[...] = m_new

    def sample_begin(k):
        @pl.when(k % subs_per_seq == 0)
        def _():
            q = qs_ref[k // subs_per_seq]
            lane = lax.broadcasted_iota(jnp.int32, (t_new, w), 1)
            for hj in range(n_hj):
                qbd_ref[hj * t_new:(hj + 1) * t_new, :] = jnp.where(
                    (lane >= D_QK * hj) & (lane < D_QK * (hj + 1)), q, jnp.zeros_like(q))
            ms_ref[...] = jnp.full((srows, 1), NEG_INF, F32)
            ls_ref[...] = jnp.zeros((srows, 1), F32)
            accs_ref[...] = jnp.zeros((w, srows), F32)

    def sample_sub(k):
        slot = k % PAGE_SLOTS
        for c in page_copies(k, slot):
            c.wait()
        ahead = k + PAGE_SLOTS - 1
        for c in page_copies(ahead, ahead % PAGE_SLOTS):
            c.start()
        kt_all = jnp.concatenate([kbuf_ref[slot, i].astype(BF16) for i in range(pps)], axis=1)
        vt_all = jnp.concatenate([vbuf_ref[slot, i].astype(BF16) for i in range(pps)], axis=1)
        s = jnp.dot(qbd_ref[...], kt_all, preferred_element_type=F32)
        kpos = ((k % subs_per_seq) * blk + lax.broadcasted_iota(jnp.int32, (1, blk), 1)).astype(F32)
        sample_update(s - sslope * (sqpos - kpos), vt_all, True)

    @pl.when(step == 0)
    def _():
        for k0 in range(PAGE_SLOTS - 1):
            for c in page_copies(k0, k0):
                c.start()

    def sample_end(k):
        @pl.when(k % subs_per_seq == subs_per_seq - 1)
        def _():
            sample_finish(k // subs_per_seq)

        @pl.when(k == n_subs - 1)
        def _():
            for extra in range(n_subs, n_subs + PAGE_SLOTS - 1):
                for c in page_copies(extra, extra % PAGE_SLOTS):
                    c.wait()

    def sample_finish(seq):
        kn = kn_ref[seq]
        sn = lax.dot_general(qbd_ref[...].astype(F32), kn, (((1,), (1,)), ((), ())), preferred_element_type=F32)
        kposn = (past + lax.broadcasted_iota(jnp.int32, (1, t_new), 1)).astype(F32)
        dist = sqpos - kposn
        sample_update(jnp.where(dist >= 0, sn - sslope * dist, NEG_INF), vn_ref[seq], False)
        lam_s = _diff_lambda(lq1_ref[...], lk1_ref[...], lq2_ref[...], lk2_ref[...], lam_init)
        o = accs_ref[...].T / ls_ref[...]
        lane = lax.broadcasted_iota(jnp.int32, (t_new, w), 1)
        a = jnp.zeros((t_new, w), F32)
        for h in range(N_ATT_HEADS):
            o1 = o[(2 * h) * t_new:(2 * h + 1) * t_new]
            o2 = o[(2 * h + 1) * t_new:(2 * h + 2) * t_new]
            a = jnp.where((lane >= D_V * h) & (lane < D_V * (h + 1)), o1 - lam_s * o2, a)
        a2 = a * a
        msq = jnp.zeros((t_new, w), F32)
        for h in range(N_ATT_HEADS):
            in_h = (lane >= D_V * h) & (lane < D_V * (h + 1))
            ssh = jnp.sum(jnp.where(in_h, a2, 0.0), axis=-1, keepdims=True)
            msq = jnp.where(in_h, ssh, msq)
        os_ref[seq] = (a * lax.rsqrt(msq * (1.0 / D_V) + EPS) * g8_ref[...] * (1.0 - lam_init)).astype(BF16)

    sub = lax.broadcasted_iota(jnp.int32, (LANES, tq), 0)
    col = lax.broadcasted_iota(jnp.int32, (EXTRA_ROWS, cols), 1)
    row = lax.broadcasted_iota(jnp.int32, (EXTRA_ROWS, cols), 0)
    feat = jnp.zeros((EXTRA_ROWS, cols), F32)
    for i, part in enumerate(L2E_PARTS):
        feat = jnp.where(row == i, POS_SPLIT * part, feat)
        feat = jnp.where(row == len(L2E_PARTS) + i, part, feat)
    for h in range(n_hp):
        qt = qt_ref[0, h * LANES:(h + 1) * LANES, :]
        for c in range(4):
            qst_ref[h, 0:LANES, c * tq:(c + 1) * tq] = jnp.where((sub >= D_QK * c) & (sub < D_QK * (c + 1)), qt,
                                                                 jnp.zeros_like(qt))
        s0, s1 = _pair_slopes(pl.program_id(1) * n_hp + h)
        qst_ref[h, LANES:LANES + EXTRA_ROWS, :] = (jnp.where(col < 2 * tq, s0, s1) * feat).astype(BF16)
        qst_ref[h, LANES + EXTRA_ROWS:, :] = jnp.zeros((LANES - EXTRA_ROWS, cols), BF16)
    m_ref[...] = jnp.full(m_ref.shape, NEG_INF, F32)
    acc_ref[...] = jnp.zeros(acc_ref.shape, F32)

    def chunks(kj0, n_sub, masked, rows=tk, between=None):
        def scores(h):
            hl = slice(h * LANES, (h + 1) * LANES)
            sts = []
            for i in range(n_sub):
                start = pl.multiple_of((kj0 + i) * tk, tk)
                kaug = jnp.concatenate([k_ref[0, pl.ds(start, rows), hl], pf_ref[pl.ds(start, rows), :]], axis=1)
                st = jnp.dot(kaug, qst_ref[h], preferred_element_type=F32)
                if masked:
                    off = ((kj0 + i) * tk - qi * tq).astype(F32)
                    st = jnp.where(rel_ref[0:rows, :] + off <= 0.0, st, NEG_INF)
                sts.append(st)
            return sts

        all_sts = [scores(h) for h in range(n_hp)]
        if between is not None:
            between()
        for h in range(n_hp):
            hl = slice(h * LANES, (h + 1) * LANES)
            sts = all_sts[h]
            m_old = m_ref[h]
            m_new = m_old
            for st in sts:
                m_new = jnp.maximum(m_new, jnp.max(st, axis=0, keepdims=True))
            acc = acc_ref[h] * jnp.exp2(m_old - m_new)
            for i, st in enumerate(sts):
                pt = jnp.exp2(st - m_new).astype(BF16)
                vaug = jnp.concatenate([vt_ref[0, kj0 + i, hl, 0:rows], jnp.ones((EXTRA_ROWS, rows), BF16)], axis=0)
                acc = acc + jnp.dot(vaug, pt, preferred_element_type=F32)
            acc_ref[h] = acc
            m_ref[h] = m_new

    n_full = (qi * tq) // tk

    def chunk_with_sample_sub(k, kj, masked):
        sample_begin(k)
        chunks(kj, 1, masked, between=lambda: sample_sub(k))
        sample_end(k)

    def body(i, carry):
        chunk_with_sample_sub(sub_base + i, i, False)
        return carry

    lax.fori_loop(0, n_full, body, 0)

    assert tk == 2 * tq
    tile_at_chunk_start = (qi * tq) % tk == 0

    @pl.when(tile_at_chunk_start)
    def _():
        chunks(n_full, 1, True, rows=tq)

    @pl.when(jnp.logical_not(tile_at_chunk_start))
    def _():
        chunk_with_sample_sub(sub_base + n_full, n_full, True)

    lam = _diff_lambda(lq1_ref[...], lk1_ref[...], lq2_ref[...], lk2_ref[...], lam_init)
    for h in range(n_hp):
        ot = acc_ref[h, 0:LANES, :] / acc_ref[h, LANES:LANES + 1, :]
        at = jnp.where(sub < D_V, ot[:, 0:tq] - lam * ot[:, tq:2 * tq], ot[:, 2 * tq:3 * tq] - lam * ot[:, 3 * tq:])
        a2 = at * at
        ss0 = jnp.sum(a2[0:D_V], axis=0, keepdims=True)
        ss1 = jnp.sum(a2[D_V:], axis=0, keepdims=True)
        ms = jnp.where(sub < D_V, ss0, ss1) * (1.0 / D_V)
        an = at * lax.rsqrt(ms + EPS) * g_ref[...] * (1.0 - lam_init)
        o_ref[0, :, h * LANES:(h + 1) * LANES] = an.T.astype(BF16)


def _attn(qtb, kb, vtb, page_table, qsb, ks, vs, cache_k, cache_v, lam_params, g_subln, lam_init,
          *, tq=256, pps=16):
    b, w, s = qtb.shape
    n_chunks, tk = vtb.shape[1], vtb.shape[3]
    n_hp = w // LANES
    hw = n_hp * LANES
    cols = 4 * tq
    bd, t_new, _ = qsb.shape
    n_pages = page_table.shape[1]
    n_pool = cache_k.shape[0]
    past = n_pages * PAGE_SIZE
    n_subs = bd * (n_pages // pps)
    nq = s // tq
    subs_per_batch = (nq // 2) ** 2
    assert tk == 2 * tq and nq % 2 == 0 and n_pages % pps == 0 and b * subs_per_batch == n_subs
    ck = jnp.transpose(cache_k, (0, 2, 3, 1)).reshape(n_pool, w, PAGE_SIZE)
    cv = jnp.transpose(cache_v, (0, 2, 3, 1)).reshape(n_pool, w, PAGE_SIZE)
    g8 = jnp.tile(g_subln, N_ATT_HEADS).reshape(1, w)
    srows = 2 * N_ATT_HEADS * t_new
    g2 = jnp.concatenate([g_subln, g_subln]).reshape(LANES, 1)
    pos = lax.broadcasted_iota(jnp.int32, (s, LANES), 0)
    lane = lax.broadcasted_iota(jnp.int32, (s, LANES), 1)
    n_parts = len(L2E_PARTS)
    pf = jnp.where(lane < n_parts, pos // POS_SPLIT,
                   jnp.where(lane < N_POS_FEATS, pos % POS_SPLIT, 0)).astype(BF16)
    rel = (jnp.arange(tk, dtype=jnp.int32)[:, None] - (jnp.arange(cols, dtype=jnp.int32) % tq)[None, :]).astype(F32)
    const2 = lambda bi, hp, qi, pt: (0, 0)
    const3 = lambda bi, hp, qi, pt: (0, 0, 0)
    lam_specs = [pl.BlockSpec((1, D_QK), const2)] * 4
    grid_spec = pltpu.PrefetchScalarGridSpec(
        num_scalar_prefetch=1,
        grid=(b, w // hw, s // tq),
        in_specs=[pl.BlockSpec((1, hw, tq), lambda bi, hp, qi, pt: (bi, hp, qi)),
                  pl.BlockSpec((1, s, hw), lambda bi, hp, qi, pt: (bi, 0, hp)),
                  pl.BlockSpec((1, n_chunks, hw, tk), lambda bi, hp, qi, pt: (bi, 0, hp, 0)),
                  pl.BlockSpec((s, LANES), const2),
                  pl.BlockSpec((tk, cols), const2),
                  *lam_specs,
                  pl.BlockSpec((LANES, 1), const2),
                  pl.BlockSpec((bd, t_new, w), const3),
                  pl.BlockSpec((bd, t_new, w), const3),
                  pl.BlockSpec((bd, t_new, w), const3),
                  pl.BlockSpec((1, w), const2),
                  pl.BlockSpec(memory_space=pl.ANY),
                  pl.BlockSpec(memory_space=pl.ANY)],
        out_specs=[pl.BlockSpec((1, tq, hw), lambda bi, hp, qi, pt: (bi, qi, hp)),
                   pl.BlockSpec((bd, t_new, w), const3)],
        scratch_shapes=[pltpu.VMEM((n_hp, 2 * LANES, cols), BF16),
                        pltpu.VMEM((n_hp, 1, cols), F32),
                        pltpu.VMEM((n_hp, LANES + EXTRA_ROWS, cols), F32),
                        pltpu.VMEM((srows, w), BF16),
                        pltpu.VMEM((srows, 1), F32),
                        pltpu.VMEM((srows, 1), F32),
                        pltpu.VMEM((w, srows), F32),
                        pltpu.VMEM((PAGE_SLOTS, pps, w, PAGE_SIZE), F32),
                        pltpu.VMEM((PAGE_SLOTS, pps, w, PAGE_SIZE), F32),
                        pltpu.SemaphoreType.DMA((2, PAGE_SLOTS))],
    )
    return pl.pallas_call(
        functools.partial(_attn_kernel, tq=tq, tk=tk, n_hp=n_hp, lam_init=lam_init,
                          pps=pps, subs_per_batch=subs_per_batch, n_subs=n_subs, past=past),
        grid_spec=grid_spec,
        out_shape=[jax.ShapeDtypeStruct((b, s, w), BF16), jax.ShapeDtypeStruct((bd, t_new, w), BF16)],
        compiler_params=_cparams("arbitrary", "arbitrary", "arbitrary"),
        name="attn",
    )(page_table, qtb, kb, vtb, pf, rel, *[p.reshape(1, D_QK) for p in lam_params], g2,
      qsb, ks, vs, g8, ck, cv)


def _outproj_kernel(a_ref, u_ref, vs_ref, x_ref, ga1_ref, sc2_ref, sh2_ref, wcat_ref, sbias_ref, wout_ref,
                    gpost_ref, gpre_ref, wr_cat_ref, br_ref, tril_ref, *rest, group_dispatch):
    if group_dispatch:
        x1_ref, pay_ref, info_ref, counts_ref, sg_ref, run_ref = rest
    else:
        x1_ref, h2_ref, gates_ref, sg_ref = rest
    tm = x_ref.shape[0]
    sgu_w = u_ref.shape[1]
    lane = lax.broadcasted_iota(jnp.int32, (CHUNK, LANES), 1)
    for ci in range(tm // CHUNK):
        rs = slice(ci * CHUNK, (ci + 1) * CHUNK)
        for gp in range(sgu_w // LANES):
            cs = slice(gp * LANES, (gp + 1) * LANES)
            vp = vs_ref[rs, cs]
            zero = jnp.zeros_like(vp)
            rhs = jnp.concatenate([jnp.where(lane < SGU_GC, vp, zero), jnp.where(lane >= SGU_GC, vp, zero)], axis=0)
            mixed = jnp.dot(wcat_ref[gp], rhs, preferred_element_type=F32) + sbias_ref[:, cs]
            sg_ref[rs, cs] = (u_ref[rs, cs].astype(F32) * mixed).astype(BF16)
    att_w = a_ref.shape[1]
    mix = (jnp.dot(a_ref[...], wout_ref[0:att_w, :], preferred_element_type=F32)
           + jnp.dot(sg_ref[...], wout_ref[att_w:, :], preferred_element_type=F32))
    x1 = x_ref[...] + ga1_ref[0] * _rms(mix, gpost_ref[...])
    x1_ref[...] = x1
    h2 = _rms(x1, gpre_ref[...]) * (1.0 + sc2_ref[0]) + sh2_ref[0]
    h2_hi = h2.astype(BF16)
    h2_lo = (h2 - h2_hi.astype(F32)).astype(BF16)
    hh = jnp.dot(h2_hi, wr_cat_ref[...], preferred_element_type=F32)
    logits = (hh[:, :LANES] + hh[:, LANES:]
              + jnp.dot(h2_lo, wr_cat_ref[:, :LANES], preferred_element_type=F32)) + br_ref[...]
    ln = lax.broadcasted_iota(jnp.int32, (tm, LANES), 1)
    big = jnp.int32(LANES)
    is_g = (ln >= N_EXPERTS) & (ln < N_EXPERTS + N_EXPERT_GROUPS)
    gl = jnp.where(is_g, logits, NEG_INF)
    gmax = jnp.max(gl, axis=-1, keepdims=True)
    g_idx = jnp.min(jnp.where(gl == gmax, ln, big), axis=-1, keepdims=True) - N_EXPERTS
    g_w = 1.0 / jnp.sum(jnp.exp(gl - gmax), axis=-1, keepdims=True)
    in_grp = (ln >= g_idx * EXPERTS_PER_GROUP) & (ln < (g_idx + 1) * EXPERTS_PER_GROUP)
    el = jnp.where(in_grp, logits, NEG_INF)
    emax = jnp.max(el, axis=-1, keepdims=True)
    ez = jnp.exp(el - emax)
    prob = ez / jnp.sum(ez, axis=-1, keepdims=True)
    p1 = jnp.max(prob, axis=-1, keepdims=True)
    i1 = jnp.min(jnp.where((prob == p1) & in_grp, ln, big), axis=-1, keepdims=True)
    rest = jnp.where(in_grp & (ln != i1), prob, -1.0)
    p2 = jnp.max(rest, axis=-1, keepdims=True)
    i2 = jnp.min(jnp.where(rest == p2, ln, big), axis=-1, keepdims=True)
    den = p1 + p2
    gates = jnp.where(ln == i1, p1 / den * g_w, 0.0) + jnp.where(ln == i2, p2 / den * g_w, 0.0)
    if not group_dispatch:
        h2_ref[...] = h2_hi
        gates_ref[...] = gates[:, :N_EXPERTS]
        return

    bits = pltpu.bitcast(h2_hi.astype(F32), jnp.uint32)
    half = PAYLOAD_H2_SEGS * LANES
    for s in range(PAYLOAD_H2_SEGS):
        lo = bits[:, s * LANES:(s + 1) * LANES]
        hi = bits[:, half + s * LANES:half + (s + 1) * LANES]
        pay_ref[s] = hi | lax.shift_right_logical(lo, jnp.uint32(16))
    pay_ref[PAYLOAD_H2_SEGS] = pltpu.bitcast(gates, jnp.uint32)

    @pl.when(pl.program_id(0) == 0)
    def _():
        run_ref[...] = jnp.zeros_like(run_ref)

    onehot = jnp.where(ln == g_idx, 1.0, 0.0).astype(BF16)
    csum = jnp.dot(tril_ref[...], onehot, preferred_element_type=F32) + run_ref[...]
    rank = jnp.sum(jnp.where(ln == g_idx, csum, 0.0), axis=-1, keepdims=True) - 1.0
    run_ref[...] = csum[tm - 1:tm, :]
    counts_ref[...] = csum[tm - 1:tm, :]
    info = jnp.where(ln == 0, g_idx, jnp.where(ln == 1, rank.astype(jnp.int32), 0))
    info_ref[...] = info[:, :INFO_W]


def _outproj(a, u, vsb, x, ga1, sc2, sh2, wcat, sbias, w_out_bf, g_post, g_pre, wr_cat, br,
             *, tm, tiles_per_mod, group_dispatch):
    n, d = x.shape
    att_w = a.shape[1]
    sgu_w = u.shape[1]
    r = ga1.shape[1]
    row = lambda i: (i, 0)
    modmap = lambda i: (i // tiles_per_mod, 0, 0)
    const = lambda i: (0, 0)
    tril = jnp.tril(jnp.ones((tm, tm), BF16))
    if group_dispatch:
        out_specs = [pl.BlockSpec((tm, d), row),
                     pl.BlockSpec((PAYLOAD_SEGS, tm, LANES), lambda i: (0, i, 0)),
                     pl.BlockSpec((tm, INFO_W), row),
                     pl.BlockSpec((1, LANES), const)]
        out_shape = [jax.ShapeDtypeStruct((n, d), F32),
                     jax.ShapeDtypeStruct((PAYLOAD_SEGS, n, LANES), jnp.uint32),
                     jax.ShapeDtypeStruct((n, INFO_W), jnp.int32),
                     jax.ShapeDtypeStruct((1, LANES), F32)]
        scratch = [pltpu.VMEM((tm, sgu_w), BF16), pltpu.VMEM((1, LANES), F32)]
    else:
        out_specs = [pl.BlockSpec((tm, d), row),
                     pl.BlockSpec((tm, d), row),
                     pl.BlockSpec((tm, N_EXPERTS), row)]
        out_shape = [jax.ShapeDtypeStruct((n, d), F32),
                     jax.ShapeDtypeStruct((n, d), BF16),
                     jax.ShapeDtypeStruct((n, N_EXPERTS), F32)]
        scratch = [pltpu.VMEM((tm, sgu_w), BF16)]
    return pl.pallas_call(
        functools.partial(_outproj_kernel, group_dispatch=group_dispatch),
        grid=(n // tm,),
        in_specs=[pl.BlockSpec((tm, att_w), row),
                  pl.BlockSpec((tm, sgu_w), row),
                  pl.BlockSpec((tm, sgu_w), row),
                  pl.BlockSpec((tm, d), row),
                  pl.BlockSpec((1, r, d), modmap),
                  pl.BlockSpec((1, r, d), modmap),
                  pl.BlockSpec((1, r, d), modmap),
                  pl.BlockSpec(wcat.shape, lambda i: (0, 0, 0)),
                  pl.BlockSpec(sbias.shape, const),
                  pl.BlockSpec(w_out_bf.shape, const),
                  pl.BlockSpec((1, d), const),
                  pl.BlockSpec((1, d), const),
                  pl.BlockSpec(wr_cat.shape, const),
                  pl.BlockSpec((1, LANES), const),
                  pl.BlockSpec((tm, tm), const)],
        out_specs=out_specs,
        out_shape=out_shape,
        scratch_shapes=scratch,
        compiler_params=_cparams("arbitrary"),
        name="outproj",
    )(a, u, vsb, x, ga1, sc2, sh2, wcat, sbias, w_out_bf, g_post.reshape(1, d), g_pre.reshape(1, d),
      wr_cat, br, tril)


def _moe_kernel(h_ref, gates_ref, x1_ref, ga2_ref, gpost_ref, wg_ref, wu_ref, wd_ref, o_ref, acc_ref):
    e = pl.program_id(1)

    @pl.when(e == 0)
    def _():
        acc_ref[...] = jnp.zeros_like(acc_ref)

    h = h_ref[...]
    ln = lax.broadcasted_iota(jnp.int32, gates_ref.shape, 1)
    per_step = wg_ref.shape[0]
    y = jnp.zeros(acc_ref.shape, F32)
    for el in range(per_step):
        gate = jnp.dot(h, wg_ref[el], preferred_element_type=F32)
        up = jnp.dot(h, wu_ref[el], preferred_element_type=F32)
        hdn = (gate * jax.nn.sigmoid(gate) * up).astype(BF16)
        ge = jnp.sum(jnp.where(ln == e * per_step + el, gates_ref[...], 0.0), axis=-1, keepdims=True)
        y = y + ge * jnp.dot(hdn, wd_ref[el], preferred_element_type=F32)
    acc_ref[...] += y

    @pl.when(e == pl.num_programs(1) - 1)
    def _():
        o_ref[...] = x1_ref[...] + ga2_ref[0] * _rms(acc_ref[...], gpost_ref[...])


def _moe(h2, gates, x1, ga2, g_post, wg_bf, wu_bf, wd_bf, *, tm, tiles_per_mod):
    n, d = x1.shape
    n_e, _, de = wg_bf.shape
    r = ga2.shape[1]
    row = lambda i, e: (i, 0)
    return pl.pallas_call(
        _moe_kernel,
        grid=(n // tm, n_e // EXPERTS_PER_GROUP),
        in_specs=[pl.BlockSpec((tm, d), row),
                  pl.BlockSpec((tm, n_e), row),
                  pl.BlockSpec((tm, d), row),
                  pl.BlockSpec((1, r, d), lambda i, e: (i // tiles_per_mod, 0, 0)),
                  pl.BlockSpec((1, d), lambda i, e: (0, 0)),
                  pl.BlockSpec((EXPERTS_PER_GROUP, d, de), lambda i, e: (e, 0, 0)),
                  pl.BlockSpec((EXPERTS_PER_GROUP, d, de), lambda i, e: (e, 0, 0)),
                  pl.BlockSpec((EXPERTS_PER_GROUP, de, d), lambda i, e: (e, 0, 0))],
        out_specs=pl.BlockSpec((tm, d), row),
        out_shape=jax.ShapeDtypeStruct((n, d), F32),
        scratch_shapes=[pltpu.VMEM((tm, d), F32)],
        compiler_params=_cparams("arbitrary", "arbitrary"),
        name="moe",
    )(h2, gates, x1, ga2, g_post.reshape(1, d), wg_bf, wu_bf, wd_bf)


def _sc_mesh():
    return plsc.VectorSubcoreMesh(core_axis_name="core", subcore_axis_name="subcore")


def _sc_scatter_rows(x, idx, n_out_rows):
    n_rows, width = x.shape

    @pl.kernel(out_type=jax.ShapeDtypeStruct((n_out_rows, width), x.dtype), mesh=_sc_mesh(), scratch_types=[])
    def scatter(x_hbm, i_hbm, o_hbm):
        def body(x_vmem, i_vmem):
            pltpu.sync_copy(x_vmem, o_hbm.at[i_vmem.at[0]])

        pltpu.emit_pipeline(
            body,
            grid=(n_rows // SC_WINDOW,),
            in_specs=[pl.BlockSpec((SC_WINDOW, width), lambda i: (i, 0)),
                      pl.BlockSpec((1, SC_WINDOW), lambda i: (0, i))],
            out_specs=[],
            core_axis_name=("core", "subcore"),
            dimension_semantics=(pltpu.PARALLEL,),
        )(x_hbm, i_hbm)

    return scatter(x, idx.reshape(1, n_rows))


def _sc_gather_rows(x, idx):
    n_rows = idx.shape[0]
    width = x.shape[1]

    @pl.kernel(out_type=jax.ShapeDtypeStruct((n_rows, width), x.dtype), mesh=_sc_mesh(), scratch_types=[])
    def gather(x_hbm, i_hbm, o_hbm):
        def body(i_vmem, o_vmem):
            pltpu.sync_copy(x_hbm.at[i_vmem.at[0]], o_vmem)

        pltpu.emit_pipeline(
            body,
            grid=(n_rows // SC_WINDOW,),
            in_specs=[pl.BlockSpec((1, SC_WINDOW), lambda i: (0, i))],
            out_specs=[pl.BlockSpec((SC_WINDOW, width), lambda i: (i, 0))],
            core_axis_name=("core", "subcore"),
            dimension_semantics=(pltpu.PARALLEL,),
        )(i_hbm, o_hbm)

    return gather(x, idx.reshape(1, n_rows))


def _group_moe_kernel(tg_ref, nused_ref, xs_ref, gpost_ref, wg_ref, wu_ref, wd_ref, rs_ref):
    t = pl.program_id(0)
    tm = xs_ref.shape[1]

    @pl.when(t >= nused_ref[0])
    def _():
        rs_ref[...] = jnp.zeros_like(rs_ref)

    @pl.when(t < nused_ref[0])
    def _():
        words = [xs_ref[s] for s in range(PAYLOAD_H2_SEGS)]
        lo = [pltpu.bitcast(lax.shift_left(w, jnp.uint32(16)), F32) for w in words]
        hi = [pltpu.bitcast(w & jnp.uint32(0xFFFF0000), F32) for w in words]
        x = jnp.concatenate(lo + hi, axis=1).astype(BF16)
        gates = pltpu.bitcast(xs_ref[PAYLOAD_H2_SEGS], F32)
        ln = lax.broadcasted_iota(jnp.int32, (tm, LANES), 1)
        first = tg_ref[t] * EXPERTS_PER_GROUP
        y = jnp.zeros((tm, x.shape[1]), F32)
        for el in range(EXPERTS_PER_GROUP):
            ge = jnp.sum(jnp.where(ln == first + el, gates, 0.0), axis=-1, keepdims=True)
            gate = jnp.dot(x, wg_ref[el], preferred_element_type=F32)
            up = jnp.dot(x, wu_ref[el], preferred_element_type=F32)
            hdn = (gate * jax.nn.sigmoid(gate) * up).astype(BF16)
            y = y + ge * jnp.dot(hdn, wd_ref[el], preferred_element_type=F32)
        r = _rms(y, gpost_ref[...])
        for s in range(rs_ref.shape[0]):
            rs_ref[s] = r[:, s * LANES:(s + 1) * LANES]


def _group_moe(xs, tile_group, n_used, g_post, wg_bf, wu_bf, wd_bf, *, tm):
    _, p_rows, _ = xs.shape
    n_e, d, de = wg_bf.shape
    g = EXPERTS_PER_GROUP
    grid_spec = pltpu.PrefetchScalarGridSpec(
        num_scalar_prefetch=2,
        grid=(p_rows // tm,),
        in_specs=[pl.BlockSpec((PAYLOAD_SEGS, tm, LANES), lambda t, tg, nu: (0, t, 0)),
                  pl.BlockSpec((1, d), lambda t, tg, nu: (0, 0)),
                  pl.BlockSpec((g, d, de), lambda t, tg, nu: (tg[t], 0, 0)),
                  pl.BlockSpec((g, d, de), lambda t, tg, nu: (tg[t], 0, 0)),
                  pl.BlockSpec((g, de, d), lambda t, tg, nu: (tg[t], 0, 0))],
        out_specs=pl.BlockSpec((d // LANES, tm, LANES), lambda t, tg, nu: (0, t, 0)),
    )
    return pl.pallas_call(
        _group_moe_kernel,
        grid_spec=grid_spec,
        out_shape=jax.ShapeDtypeStruct((d // LANES, p_rows, LANES), F32),
        compiler_params=_cparams("arbitrary"),
        name="group_moe",
    )(tile_group, n_used, xs, g_post.reshape(1, d), wg_bf, wu_bf, wd_bf)


def _residual_kernel(x1_ref, ga2_ref, r_ref, o_ref):
    r = jnp.concatenate([r_ref[s] for s in range(r_ref.shape[0])], axis=1)
    o_ref[...] = x1_ref[...] + ga2_ref[0] * r


def _residual(x1, ga2, r_segs, *, tm, tiles_per_mod):
    n, d = x1.shape
    return pl.pallas_call(
        _residual_kernel,
        grid=(n // tm,),
        in_specs=[pl.BlockSpec((tm, d), lambda i: (i, 0)),
                  pl.BlockSpec((1, ga2.shape[1], d), lambda i: (i // tiles_per_mod, 0, 0)),
                  pl.BlockSpec((d // LANES, tm, LANES), lambda i: (0, i, 0))],
        out_specs=pl.BlockSpec((tm, d), lambda i: (i, 0)),
        out_shape=jax.ShapeDtypeStruct((n, d), F32),
        compiler_params=_cparams("arbitrary"),
        name="residual",
    )(x1, ga2, r_segs)


def _group_plan(info, counts, n_tokens, tm):
    n_tiles = n_tokens // tm + N_EXPERT_GROUPS
    cnt = counts[0, :N_EXPERT_GROUPS].astype(jnp.int32)
    padded = (cnt + tm - 1) // tm * tm
    ends = jnp.cumsum(padded)
    starts = ends - padded
    gid, rank = info[:, 0], info[:, 1]
    onehot = (gid[:, None] == jnp.arange(N_EXPERT_GROUPS, dtype=jnp.int32)[None, :]).astype(jnp.int32)
    pos = jnp.sum(onehot * starts[None, :], axis=1) + rank
    tile_ends = ends // tm
    t = jnp.arange(n_tiles, dtype=jnp.int32)
    tile_group = jnp.minimum(jnp.sum((t[:, None] >= tile_ends[None, :]).astype(jnp.int32), axis=1),
                             N_EXPERT_GROUPS - 1)
    return pos, tile_group, tile_ends[-1:], n_tiles * tm


def _sgu_weights(ws, bs, chunk_len):
    causal = jnp.tril(jnp.ones((chunk_len, chunk_len), ws.dtype))
    wm = ws[:, :chunk_len, :chunk_len] * causal
    reps = CHUNK // chunk_len
    if reps > 1:
        eye = jnp.eye(reps, dtype=ws.dtype)
        wm = jnp.einsum("ab,gts->gatbs", eye, wm).reshape(N_SGU_GROUPS, CHUNK, CHUNK)
    wcat = jnp.concatenate([wm[0::2], wm[1::2]], axis=2).astype(BF16)
    bt = jnp.tile(bs[:, :chunk_len].T, (reps, 1))
    sbias = jnp.repeat(bt, SGU_GC, axis=1)
    return wcat, sbias


def _split_mod(mod):
    return [m[:, None, :] for m in jnp.split(mod, 6, axis=-1)]


def kernel(x_prompt, x_sample, cache_k, cache_v, page_table, c_prompt, c_sample, w_ada, b_ada, g_pre_mix, g_post_mix, g_pre_ffn, g_post_ffn, w_in, lam_q1, lam_k1, lam_q2, lam_k2, g_subln, sgu_ln_g, sgu_ln_b, sgu_ws, sgu_bs, w_out, w_rg, b_rg, w_re, b_re, w_gate, w_up, w_down):
    depth = w_in.shape[0]
    assert depth == 1
    l = 0
    lam_init = 0.8 - 0.6 * math.exp(-0.3 * l)
    bp, sp, d = x_prompt.shape
    bs_, ts, _ = x_sample.shape
    n_s = bs_ * ts

    mod = _ada(jnp.concatenate([c_prompt, c_sample], axis=0), w_ada[l], b_ada[l])
    sh1p, sc1p, ga1p, sh2p, sc2p, ga2p = _split_mod(mod[:bp])
    rep = lambda m: jnp.repeat(m, ts, axis=0).reshape(1, n_s, d)
    sh1s, sc1s, ga1s, sh2s, sc2s, ga2s = [rep(m) for m in jnp.split(mod[bp:], 6, axis=-1)]

    w_in_bf = w_in[l].astype(BF16)
    wqv_t_bf = jnp.concatenate([w_in_bf[:, 0:ATT_W], w_in_bf[:, 2 * ATT_W:3 * ATT_W]], axis=1).T
    w_out_bf = w_out[l].astype(BF16)
    wg_bf = w_gate[l].astype(BF16)
    wu_bf = w_up[l].astype(BF16)
    wd_bf = w_down[l].astype(BF16)
    wr = jnp.concatenate([w_re[l], w_rg[l]], axis=1)
    wr = jnp.pad(wr, ((0, 0), (0, LANES - wr.shape[1])))
    wr_hi = wr.astype(BF16)
    wr_cat = jnp.concatenate([wr_hi, (wr - wr_hi.astype(F32)).astype(BF16)], axis=1)
    br = jnp.pad(jnp.concatenate([b_re[l], b_rg[l]]), (0, LANES - N_EXPERTS - N_EXPERT_GROUPS)).reshape(1, LANES)
    lam_params = (lam_q1[l], lam_k1[l], lam_q2[l], lam_k2[l])

    tm_p = 512
    xp = x_prompt.reshape(bp * sp, d)
    ktp, vtp, qtb, kb, vtb, u, vsb, vs_last = _inproj(
        xp, sc1p, sh1p, g_pre_mix[l], w_in_bf, wqv_t_bf, sgu_ln_g[l], sgu_ln_b[l],
        tm=tm_p, tiles_per_seq=sp // tm_p, last_rows=CHUNK, transposed_qv=True)
    w3 = ATT_W
    xs = x_sample.reshape(n_s, d)
    ks, vs_, qbs, us, vsbs, vs_last_s = _inproj(
        xs, sc1s, sh1s, g_pre_mix[l], w_in_bf, wqv_t_bf, sgu_ln_g[l], sgu_ln_b[l],
        tm=n_s, tiles_per_seq=1, last_rows=n_s, transposed_qv=False)
    a, a_s = _attn(qtb, kb.reshape(bp, sp, w3), vtb, page_table, qbs.reshape(bs_, ts, w3),
                   ks.reshape(bs_, ts, w3), vs_.reshape(bs_, ts, w3), cache_k[l], cache_v[l],
                   lam_params, g_subln[l], lam_init)
    kp = jnp.transpose(ktp.reshape(bp, N_ATT_HEADS, 2 * D_QK, sp), (0, 3, 1, 2))
    vp = jnp.transpose(vtp.reshape(bp, N_ATT_HEADS, D_V, sp), (0, 3, 1, 2))
    wcat_p, sbias_p = _sgu_weights(sgu_ws[l], sgu_bs[l], CHUNK)
    n_p = bp * sp
    x1, payload, info, counts = _outproj(
        a.reshape(n_p, w3), u, vsb, xp, ga1p, sc2p, sh2p, wcat_p, sbias_p, w_out_bf,
        g_post_mix[l], g_pre_ffn[l], wr_cat, br, tm=2 * tm_p, tiles_per_mod=sp // (2 * tm_p), group_dispatch=True)
    tm_moe = 512
    pos, tile_group, n_used, p_rows = _group_plan(info, counts, n_p, tm_moe)
    seg_p = lambda k: (jnp.arange(k, dtype=jnp.int32)[:, None] * p_rows + pos[None, :]).reshape(-1)
    sorted_rows = _sc_scatter_rows(payload.reshape(PAYLOAD_SEGS * n_p, LANES), seg_p(PAYLOAD_SEGS),
                                   PAYLOAD_SEGS * p_rows)
    rs = _group_moe(sorted_rows.reshape(PAYLOAD_SEGS, p_rows, LANES), tile_group, n_used, g_post_ffn[l],
                    wg_bf, wu_bf, wd_bf, tm=tm_moe)
    n_seg = d // LANES
    r_tok = _sc_gather_rows(rs.reshape(n_seg * p_rows, LANES), seg_p(n_seg))
    yp = _residual(x1, ga2p, r_tok.reshape(n_seg, n_p, LANES), tm=tm_p, tiles_per_mod=sp // tm_p)

    wcat_s, sbias_s = _sgu_weights(sgu_ws[l], sgu_bs[l], ts)
    x1s, h2s, gates_s = _outproj(a_s.reshape(n_s, w3), us, vsbs, xs, ga1s, sc2s, sh2s, wcat_s, sbias_s, w_out_bf,
                                 g_post_mix[l], g_pre_ffn[l], wr_cat, br, tm=n_s, tiles_per_mod=1,
                                 group_dispatch=False)
    ys = _moe(h2s, gates_s, x1s, ga2s, g_post_ffn[l], wg_bf, wu_bf, wd_bf, tm=n_s, tiles_per_mod=1)

    return (yp.reshape(bp, sp, d), ys.reshape(bs_, ts, d),
            kp[None], vp[None],
            vs_last.reshape(1, bp, CHUNK, -1),
            ks.reshape(1, bs_, ts, N_ATT_HEADS, 2 * D_QK), vs_.reshape(1, bs_, ts, N_ATT_HEADS, D_V),
            vs_last_s.reshape(1, bs_, ts, -1))
```
